```python
import math
import jax, jax.numpy as jnp
from jax import lax
import numpy as np

D_MODEL = 4096
BATCH = 4
SEQ = 2048
DEPTH = 4
DEC_BATCH = 128
DEC_SEQ = 8
PAST_LEN = 8192
PAGE_SIZE = 128

N_META = 16
BLOCK = 128
DH_A = 64
H_A = D_MODEL // 128
KVH_A = H_A // 8
WINDOW = 128
W_A = H_A * DH_A
KV_W = KVH_A * DH_A
P_B = 16
N_B = 64
W_B = D_MODEL // 2
G_B = W_B // P_B
DT_MIN = 0.001
DT_MAX = 0.1
POOL_WINDOWS = (2, 4, 8, 16)
POOL_MAX = 16
C_WIDTH = D_MODEL // 4
CG = C_WIDTH // len(POOL_WINDOWS)
H_D = D_MODEL // 256
Q_LORA = D_MODEL // 4
KV_LORA = 512
NOPE = 128
ROPE_DIM = 64
V_DIM = 128
ROPE_THETA = 10000.0
MLA_SCALE = (NOPE + ROPE_DIM) ** -0.5
MLA_ROW = KV_LORA + ROPE_DIM
N_GROUPS_E = 4
EPG = 4
N_EXPERTS = N_GROUPS_E * EPG
TOP_K = 2
D_EXPERT = D_MODEL // 4
EXPERT_BLOCK = 128
ALPHA = (2 * DEPTH) ** 0.25
BETA = (8 * DEPTH) ** -0.25
N_EVEN = (DEPTH + 1) // 2
N_ODD = DEPTH // 2
IN_AB = W_A + 2 * KV_W + 2 * W_B
OUT_AB = W_A + W_B
IN_CD = C_WIDTH + Q_LORA + KV_LORA + ROPE_DIM
OUT_CD = C_WIDTH + H_D * V_DIM
NEG = -1e30
EPS = 1e-5

kernel_name = 'hybrid_swa_s5_pool_mla_hmoe_step'

F32 = jnp.float32


def layer_norm(x, g, b):
    xf = x.astype(F32)
    xc = xf - jnp.mean(xf, -1, keepdims=True)
    var = jnp.mean(xc * xc, -1, keepdims=True)
    return (xc * lax.rsqrt(var + EPS) * g.astype(F32) + b.astype(F32)).astype(x.dtype)


def rms_norm(x, g):
    xf = x.astype(F32)
    return (xf * lax.rsqrt(jnp.mean(xf * xf, -1, keepdims=True) + EPS) * g.astype(F32)).astype(x.dtype)


def rope(x, pos):
    half = x.shape[-1] // 2
    inv = ROPE_THETA ** (-jnp.arange(half, dtype=F32) / half)
    ang = pos.astype(F32)[:, None] * inv
    ang = ang.reshape((ang.shape[0],) + (1,) * (x.ndim - 3) + (half,))
    cos, sin = jnp.cos(ang), jnp.sin(ang)
    xf = x.astype(F32)
    x1, x2 = xf[..., :half], xf[..., half:]
    return jnp.concatenate([x1 * cos - x2 * sin, x1 * sin + x2 * cos], -1).astype(x.dtype)


def alibi_slopes(n_heads):
    return jnp.exp2(-8.0 * jnp.arange(1, n_heads + 1, dtype=F32) / n_heads)


def sink_window_attention(q, q_pos, k_meta, v_meta, k_band, v_band, k_pos, sinks):
    Bt, N, Q, H, Dh = q.shape
    KVH = k_band.shape[3]
    G = H // KVH
    M = k_meta.shape[1]
    K = k_band.shape[2]
    qg = q.reshape(Bt, N, Q, KVH, G, Dh)
    s_meta = jnp.einsum('bnqhgd,bmhd->bhgnqm', qg, k_meta)
    s_band = jnp.einsum('bnqhgd,bnkhd->bhgnqk', qg, k_band)
    scores = jnp.concatenate([s_meta, s_band], -1).astype(F32) * (Dh ** -0.5)
    meta_pos = jnp.broadcast_to(jnp.arange(M, dtype=jnp.int32), (N, M))
    kpos_all = jnp.concatenate([meta_pos, k_pos], -1)
    dist = q_pos[:, :, None] - kpos_all[:, None, :]
    is_meta_col = (jnp.arange(M + K) < M)[None, None, :]
    in_band = (kpos_all >= N_META)[:, None, :] & (dist <= WINDOW)
    visible = (dist >= 0) & (is_meta_col | in_band)
    slopes = alibi_slopes(H).reshape(KVH, G)
    bias = -slopes[None, :, :, None, None, None] * jnp.abs(dist).astype(F32)[None, None, None]
    logits = jnp.where(visible[None, None, None], scores + bias, NEG)
    sink = jnp.broadcast_to(sinks.astype(F32).reshape(KVH, G)[None, :, :, None, None, None], logits.shape[:-1] + (1,))
    probs = jax.nn.softmax(jnp.concatenate([logits, sink], -1), -1)[..., :-1].astype(v_band.dtype)
    o = (jnp.einsum('bhgnqm,bmhd->bnqhgd', probs[..., :M], v_meta)
         + jnp.einsum('bhgnqk,bnkhd->bnqhgd', probs[..., M:], v_band))
    return o.reshape(Bt, N, Q, H * Dh)


def swa_prompt(q, k, v, sinks):
    Bt, T = q.shape[:2]
    pad = (-T) % BLOCK
    nb = (T + pad) // BLOCK

    def to_blocks(t):
        t = jnp.pad(t, ((0, 0), (pad, 0), (0, 0), (0, 0)))
        return t.reshape(Bt, nb, BLOCK, t.shape[2], t.shape[3])

    def with_prev(t):
        prev = jnp.pad(t[:, :-1], ((0, 0), (1, 0), (0, 0), (0, 0), (0, 0)))
        return jnp.concatenate([prev, t], axis=2)

    q_pos = (jnp.arange(nb * BLOCK, dtype=jnp.int32) - pad).reshape(nb, BLOCK)
    k_pos = jnp.concatenate([q_pos - BLOCK, q_pos], -1)
    o = sink_window_attention(to_blocks(q), q_pos, k[:, :N_META], v[:, :N_META],
                              with_prev(to_blocks(k)), with_prev(to_blocks(v)), k_pos, sinks)
    o = o.reshape(Bt, nb * BLOCK, W_A)[:, pad:]
    new_kv = jnp.stack([jnp.concatenate([k[:, :N_META], k[:, T - WINDOW:]], 1),
                        jnp.concatenate([v[:, :N_META], v[:, T - WINDOW:]], 1)], axis=1)
    return o, new_kv


def swa_sample(q, k, v, kv_buf, sinks):
    Bt, T = q.shape[:2]
    win = kv_buf.shape[2] - N_META
    k_meta, v_meta = kv_buf[:, 0, :N_META], kv_buf[:, 1, :N_META]
    k_band = jnp.concatenate([kv_buf[:, 0, N_META:].astype(k.dtype), k], 1)
    v_band = jnp.concatenate([kv_buf[:, 1, N_META:].astype(v.dtype), v], 1)
    q_pos = PAST_LEN + jnp.arange(T, dtype=jnp.int32)
    k_pos = jnp.concatenate([PAST_LEN - win + jnp.arange(win, dtype=jnp.int32), q_pos])
    o = sink_window_attention(q[:, None], q_pos[None], k_meta, v_meta, k_band[:, None], v_band[:, None],
                              k_pos[None], sinks)
    new_kv = jnp.stack([jnp.concatenate([k_meta, k_band[:, -win:]], 1),
                        jnp.concatenate([v_meta, v_band[:, -win:]], 1)], axis=1)
    return o.reshape(Bt, T, W_A), new_kv


def _ssm_combine(e1, e2):
    a1, b1 = e1
    a2, b2 = e2
    return a1 * a2, a2 * b1 + b2


def s5_glu(u, gate, h0, lam_re, lam_im, log_dt, b_re, b_im, c_re, c_im, d_skip):
    Bt, T, _ = u.shape
    lam = lax.complex(lam_re.astype(F32), lam_im.astype(F32))
    lam_dt = lam * jnp.exp(log_dt.astype(F32))[:, None]
    lam_bar = jnp.exp(lam_dt)
    b_bar = ((lam_bar - 1.0) / lam)[..., None] * lax.complex(b_re.astype(F32), b_im.astype(F32))
    c = lax.complex(c_re.astype(F32), c_im.astype(F32))
    ug = u.astype(F32).reshape(Bt, T, G_B, P_B)
    bu = jnp.einsum('btgp,gnp->btgn', ug.astype(jnp.complex64), b_bar)
    a = jnp.broadcast_to(lam_bar[None, None], (1, T, G_B, N_B))
    _, h = lax.associative_scan(_ssm_combine, (a, bu), axis=1)
    if h0 is not None:
        h0c = lax.complex(h0[..., 0].astype(F32), h0[..., 1].astype(F32))
        steps = jnp.arange(1, T + 1, dtype=F32)
        h = h + jnp.exp(lam_dt[None] * steps[:, None, None])[None] * h0c[:, None]
    y = jnp.einsum('btgn,gpn->btgp', h, c).real + d_skip.astype(F32).reshape(G_B, P_B) * ug
    y = y.reshape(Bt, T, W_B)
    out = jax.nn.gelu(y) * jax.nn.sigmoid(gate.astype(F32))
    h_last = jnp.stack([h[:, -1].real, h[:, -1].imag], -1)
    return out.astype(u.dtype), h_last.astype(u.dtype)


def even_mixer(x, kv_buf, h0, w_in, sinks, lam_re, lam_im, log_dt, b_re, b_im, c_re, c_im, d_skip, w_out):
    Bt, T, _ = x.shape
    z = x @ w_in
    q, k, v, u, gate = jnp.split(z, [W_A, W_A + KV_W, W_A + 2 * KV_W, W_A + 2 * KV_W + W_B], axis=-1)
    q = q.reshape(Bt, T, H_A, DH_A)
    k = k.reshape(Bt, T, KVH_A, DH_A)
    v = v.reshape(Bt, T, KVH_A, DH_A)
    if kv_buf is None:
        attn, new_kv = swa_prompt(q, k, v, sinks)
    else:
        attn, new_kv = swa_sample(q, k, v, kv_buf, sinks)
    y_ssm, h_last = s5_glu(u, gate, h0, lam_re, lam_im, log_dt, b_re, b_im, c_re, c_im, d_skip)
    out = jnp.concatenate([attn, y_ssm.astype(attn.dtype)], -1) @ w_out
    return out, new_kv, h_last


def multiscale_pool(u, pos, pool_w, pool_scale):
    Bt, T, C = u.shape
    uf = u.astype(F32)
    cs = jnp.pad(jnp.cumsum(uf, axis=1), ((0, 0), (POOL_MAX, 0), (0, 0)))
    parts = []
    for g, w in enumerate(POOL_WINDOWS):
        sl = slice(g * CG, (g + 1) * CG)
        win_sum = cs[:, POOL_MAX:, sl] - cs[:, POOL_MAX - w:POOL_MAX - w + T, sl]
        count = jnp.minimum(pos + 1, w).astype(F32)[None, :, None]
        parts.append(win_sum / count - uf[..., sl])
    pooled = jnp.stack(parts, axis=2)
    mixed = jnp.einsum('btgc,gcd->btgd', pooled, pool_w.astype(F32)).reshape(Bt, T, C)
    return (mixed * pool_scale.astype(F32)).astype(u.dtype)


def mla_prompt(q_nope, q_rope, ckv, kr, pos, w_uk, w_uv):
    Bt, T = q_nope.shape[:2]
    k_nope = jnp.einsum('btr,rhd->bthd', ckv, w_uk)
    v = jnp.einsum('btr,rhd->bthd', ckv, w_uv)
    pad = (-T) % BLOCK
    nb = (T + pad) // BLOCK

    def to_blocks(t):
        t = jnp.pad(t, ((0, 0), (pad, 0)) + ((0, 0),) * (t.ndim - 2))
        return jnp.moveaxis(t.reshape((Bt, nb, BLOCK) + t.shape[2:]), 1, 0)

    q_pos = (jnp.arange(nb * BLOCK, dtype=jnp.int32) - pad).reshape(nb, BLOCK)

    def one_block(args):
        qn, qr, qp = args
        s = jnp.einsum('bqhd,bkhd->bhqk', qn, k_nope) + jnp.einsum('bqhd,bkd->bhqk', qr, kr)
        s = jnp.where(pos[None, :] <= qp[:, None], s.astype(F32) * MLA_SCALE, NEG)
        p = jax.nn.softmax(s, -1).astype(v.dtype)
        return jnp.einsum('bhqk,bkhd->bqhd', p, v)

    o = lax.map(one_block, (to_blocks(q_nope), to_blocks(q_rope), q_pos))
    return jnp.moveaxis(o, 0, 1).reshape(Bt, nb * BLOCK, H_D * V_DIM)[:, pad:]


def mla_sample(q_nope, q_rope, ckv, kr, past_lat, past_rope, w_uk, w_uv):
    Bt, T = q_nope.shape[:2]
    P = past_lat.shape[1]
    q_lat = jnp.einsum('bqhd,rhd->bqhr', q_nope, w_uk)
    s_past = jnp.einsum('bqhr,bkr->bhqk', q_lat, past_lat) + jnp.einsum('bqhd,bkd->bhqk', q_rope, past_rope)
    s_new = jnp.einsum('bqhr,bkr->bhqk', q_lat, ckv) + jnp.einsum('bqhd,bkd->bhqk', q_rope, kr)
    s = jnp.concatenate([s_past, s_new], -1).astype(F32) * MLA_SCALE
    causal_new = jnp.arange(T)[None, :] <= jnp.arange(T)[:, None]
    mask = jnp.concatenate([jnp.ones((T, P), bool), causal_new], -1)
    p = jax.nn.softmax(jnp.where(mask, s, NEG), -1).astype(ckv.dtype)
    o_lat = (jnp.einsum('bhqk,bkr->bqhr', p[..., :P], past_lat)
             + jnp.einsum('bhqk,bkr->bqhr', p[..., P:], ckv))
    o = jnp.einsum('bqhr,rhd->bqhd', o_lat, w_uv)
    return o.reshape(Bt, T, H_D * V_DIM)


def odd_mixer(x, pos, pool_buf, past_lat, past_rope, w_in, pool_w, pool_scale,
              q_norm, w_uq, kv_norm, w_uk, w_uv, w_out):
    Bt, T, _ = x.shape
    z = x @ w_in
    u, cq, ckv, kr = jnp.split(z, [C_WIDTH, C_WIDTH + Q_LORA, C_WIDTH + Q_LORA + KV_LORA], axis=-1)
    if pool_buf is None:
        u_all, pos_all = u, pos
    else:
        nbuf = pool_buf.shape[1]
        u_all = jnp.concatenate([pool_buf.astype(u.dtype), u], 1)
        pos_all = jnp.concatenate([pos[0] - nbuf + jnp.arange(nbuf, dtype=jnp.int32), pos])
    y_pool = multiscale_pool(u_all, pos_all, pool_w, pool_scale)[:, -T:]
    new_pool = u_all[:, -(POOL_MAX - 1):]
    q = jnp.einsum('btr,rhd->bthd', rms_norm(cq, q_norm), w_uq)
    q_nope, q_rope = q[..., :NOPE], rope(q[..., NOPE:], pos)
    ckv = rms_norm(ckv, kv_norm)
    kr = rope(kr, pos)
    new_rows = jnp.concatenate([ckv, kr], -1)
    if past_lat is None:
        y_mla = mla_prompt(q_nope, q_rope, ckv, kr, pos, w_uk, w_uv)
    else:
        y_mla = mla_sample(q_nope, q_rope, ckv, kr, past_lat.astype(ckv.dtype), past_rope.astype(kr.dtype), w_uk, w_uv)
    out = jnp.concatenate([y_pool, y_mla.astype(y_pool.dtype)], -1) @ w_out
    return out, new_pool, new_rows


def routed_expert_mlp(xt, experts, w_gate, w_up, w_down):
    T, K = experts.shape
    A = T * K
    e_flat = experts.reshape(A)
    tok = jnp.arange(A, dtype=jnp.int32) // K
    order = jnp.argsort(e_flat)
    e_sorted = e_flat[order]
    counts = jnp.bincount(e_flat, length=N_EXPERTS)
    padded = (counts + EXPERT_BLOCK - 1) // EXPERT_BLOCK * EXPERT_BLOCK
    pad_end = jnp.cumsum(padded)
    pad_start = pad_end - padded
    start = jnp.cumsum(counts) - counts
    dest = pad_start[e_sorted] + jnp.arange(A, dtype=jnp.int32) - start[e_sorted]
    n_blocks = (A + N_EXPERTS * (EXPERT_BLOCK - 1) + EXPERT_BLOCK - 1) // EXPERT_BLOCK
    slot_tok = jnp.full((n_blocks * EXPERT_BLOCK,), T, jnp.int32).at[dest].set(tok[order])
    block_expert = jnp.minimum(
        jnp.searchsorted(pad_end, jnp.arange(n_blocks, dtype=jnp.int32) * EXPERT_BLOCK, side='right'),
        N_EXPERTS - 1)
    x_pad = jnp.concatenate([xt, jnp.zeros((1, xt.shape[1]), xt.dtype)], 0)

    def expert_block(args):
        toks, e = args
        xb = x_pad[toks]
        hb = jax.nn.silu(xb @ w_gate[e]) * (xb @ w_up[e])
        return hb @ w_down[e]

    y_slots = lax.map(expert_block, (slot_tok.reshape(n_blocks, EXPERT_BLOCK), block_expert))
    y_slots = y_slots.reshape(-1, xt.shape[1])
    y = jnp.zeros((A, xt.shape[1]), y_slots.dtype).at[order].set(y_slots[dest])
    return y.reshape(T, K, xt.shape[1])


def hierarchical_moe(x, rg_w, rg_b, re_w, re_b, w_gate, w_up, w_down):
    shp = x.shape
    xt = x.reshape(-1, D_MODEL)
    T = xt.shape[0]
    xf = xt.astype(F32)
    rows = jnp.arange(T)
    g_logits = xf @ rg_w.astype(F32) + rg_b.astype(F32)
    g_idx = jnp.argmax(g_logits, -1).astype(jnp.int32)
    g_gate = jax.nn.softmax(g_logits, -1)[rows, g_idx][:, None]
    e_logits = (xf @ re_w.astype(F32) + re_b.astype(F32)).reshape(T, N_GROUPS_E, EPG)[rows, g_idx]
    top_logit, top_j = lax.top_k(e_logits, TOP_K)
    gates = g_gate * jax.nn.softmax(top_logit, -1)
    experts = g_idx[:, None] * EPG + top_j.astype(jnp.int32)
    y = routed_expert_mlp(xt, experts, w_gate, w_up, w_down)
    out = jnp.einsum('tk,tkd->td', gates, y.astype(F32))
    return out.astype(x.dtype).reshape(shp)


def setup_inputs(seed: int = 0) -> dict:
    key = jax.random.key(seed)
    keys = list(jax.random.split(key, 64))

    def nxt():
        return keys.pop()

    def nrm(shape, scale=1.0):
        return jax.random.normal(nxt(), shape, F32) * scale

    def gain(shape):
        return 1.0 + nrm(shape, 0.02)

    n_pages = PAST_LEN // PAGE_SIZE
    n_used = DEC_BATCH * n_pages
    n_pool = (5 * n_used + 3) // 4
    page_table = jax.random.permutation(nxt(), n_pool)[:n_used].reshape(DEC_BATCH, n_pages).astype(jnp.int32)
    win_rows = min(WINDOW, PAST_LEN)
    lam_im0 = math.pi * jnp.arange(N_B, dtype=F32)
    return {
        'x_prompt': nrm((BATCH, SEQ, D_MODEL)),
        'x_sample': nrm((DEC_BATCH, DEC_SEQ, D_MODEL)),
        'cache_swa_kv': nrm((N_EVEN, DEC_BATCH, 2, N_META + win_rows, KVH_A, DH_A)),
        'state_ssm': nrm((N_EVEN, DEC_BATCH, G_B, N_B, 2), 0.5),
        'state_pool': nrm((N_ODD, DEC_BATCH, POOL_MAX - 1, C_WIDTH)),
        'cache_mla': nrm((N_ODD, n_pool, PAGE_SIZE, MLA_ROW)),
        'page_table': page_table,
        'meta_tokens': nrm((N_META, D_MODEL)),
        'w_in_ab': nrm((N_EVEN, D_MODEL, IN_AB), D_MODEL ** -0.5),
        'attn_sinks': nrm((N_EVEN, H_A), 0.5),
        'ssm_lambda_re': -0.5 + nrm((N_EVEN, G_B, N_B), 0.01),
        'ssm_lambda_im': lam_im0 + nrm((N_EVEN, G_B, N_B), 0.01),
        'ssm_log_dt': jax.random.uniform(nxt(), (N_EVEN, G_B), F32, math.log(DT_MIN), math.log(DT_MAX)),
        'ssm_b_re': nrm((N_EVEN, G_B, N_B, P_B), (2 * P_B) ** -0.5),
        'ssm_b_im': nrm((N_EVEN, G_B, N_B, P_B), (2 * P_B) ** -0.5),
        'ssm_c_re': nrm((N_EVEN, G_B, P_B, N_B), N_B ** -0.5),
        'ssm_c_im': nrm((N_EVEN, G_B, P_B, N_B), N_B ** -0.5),
        'ssm_d': nrm((N_EVEN, W_B)),
        'w_out_ab': nrm((N_EVEN, OUT_AB, D_MODEL), OUT_AB ** -0.5 * BETA),
        'w_in_cd': nrm((N_ODD, D_MODEL, IN_CD), D_MODEL ** -0.5),
        'pool_w': nrm((N_ODD, len(POOL_WINDOWS), CG, CG), CG ** -0.5),
        'pool_scale': gain((N_ODD, C_WIDTH)),
        'mla_q_norm': gain((N_ODD, Q_LORA)),
        'mla_w_uq': nrm((N_ODD, Q_LORA, H_D, NOPE + ROPE_DIM), Q_LORA ** -0.5),
        'mla_kv_norm': gain((N_ODD, KV_LORA)),
        'mla_w_uk': nrm((N_ODD, KV_LORA, H_D, NOPE), KV_LORA ** -0.5),
        'mla_w_uv': nrm((N_ODD, KV_LORA, H_D, V_DIM), KV_LORA ** -0.5),
        'w_out_cd': nrm((N_ODD, OUT_CD, D_MODEL), OUT_CD ** -0.5 * BETA),
        'ln_mix_g': gain((DEPTH, D_MODEL)),
        'ln_mix_b': nrm((DEPTH, D_MODEL), 0.02),
        'ln_ffn_g': gain((DEPTH, D_MODEL)),
        'ln_ffn_b': nrm((DEPTH, D_MODEL), 0.02),
        'router_group_w': nrm((DEPTH, D_MODEL, N_GROUPS_E), D_MODEL ** -0.5),
        'router_group_b': nrm((DEPTH, N_GROUPS_E), 0.01),
        'router_expert_w': nrm((DEPTH, D_MODEL, N_EXPERTS), D_MODEL ** -0.5),
        'router_expert_b': nrm((DEPTH, N_EXPERTS), 0.01),
        'expert_w_gate': nrm((DEPTH, N_EXPERTS, D_MODEL, D_EXPERT), D_MODEL ** -0.5),
        'expert_w_up': nrm((DEPTH, N_EXPERTS, D_MODEL, D_EXPERT), D_MODEL ** -0.5),
        'expert_w_down': nrm((DEPTH, N_EXPERTS, D_EXPERT, D_MODEL), D_EXPERT ** -0.5 * BETA),
    }


def reference(x_prompt, x_sample, cache_swa_kv, state_ssm, state_pool, cache_mla, page_table,
              meta_tokens, w_in_ab, attn_sinks, ssm_lambda_re, ssm_lambda_im, ssm_log_dt,
              ssm_b_re, ssm_b_im, ssm_c_re, ssm_c_im, ssm_d, w_out_ab,
              w_in_cd, pool_w, pool_scale, mla_q_norm, mla_w_uq, mla_kv_norm, mla_w_uk, mla_w_uv, w_out_cd,
              ln_mix_g, ln_mix_b, ln_ffn_g, ln_ffn_b, router_group_w, router_group_b,
              router_expert_w, router_expert_b, expert_w_gate, expert_w_up, expert_w_down):
    Bp = x_prompt.shape[0]
    n_dec = x_sample.shape[0]
    meta = jnp.broadcast_to(meta_tokens.astype(x_prompt.dtype)[None], (Bp, N_META, D_MODEL))
    xp = jnp.concatenate([meta, x_prompt], axis=1)
    xs = x_sample
    pos_p = jnp.arange(xp.shape[1], dtype=jnp.int32)
    pos_s = PAST_LEN + jnp.arange(xs.shape[1], dtype=jnp.int32)
    swa_p, swa_s, ssm_p, ssm_s, pool_p, pool_s, mla_p, mla_s = [], [], [], [], [], [], [], []
    for l in range(DEPTH):
        i = l // 2
        if l % 2 == 0:
            ssm_params = (ssm_lambda_re[i], ssm_lambda_im[i], ssm_log_dt[i], ssm_b_re[i], ssm_b_im[i],
                          ssm_c_re[i], ssm_c_im[i], ssm_d[i])
            mp, kv_new_p, h_new_p = even_mixer(xp, None, None, w_in_ab[i], attn_sinks[i], *ssm_params, w_out_ab[i])
            ms, kv_new_s, h_new_s = even_mixer(xs, cache_swa_kv[i], state_ssm[i], w_in_ab[i], attn_sinks[i],
                                               *ssm_params, w_out_ab[i])
            swa_p.append(kv_new_p)
            swa_s.append(kv_new_s)
            ssm_p.append(h_new_p)
            ssm_s.append(h_new_s)
        else:
            odd_params = (w_in_cd[i], pool_w[i], pool_scale[i], mla_q_norm[i], mla_w_uq[i],
                          mla_kv_norm[i], mla_w_uk[i], mla_w_uv[i], w_out_cd[i])
            past_lat = cache_mla[i, page_table, :, :KV_LORA].reshape(n_dec, -1, KV_LORA)
            past_rope = cache_mla[i, page_table, :, KV_LORA:].reshape(n_dec, -1, ROPE_DIM)
            mp, buf_new_p, rows_new_p = odd_mixer(xp, pos_p, None, None, None, *odd_params)
            ms, buf_new_s, rows_new_s = odd_mixer(xs, pos_s, state_pool[i], past_lat, past_rope, *odd_params)
            pool_p.append(buf_new_p)
            pool_s.append(buf_new_s)
            mla_p.append(rows_new_p)
            mla_s.append(rows_new_s)
        xp = layer_norm(ALPHA * xp + mp, ln_mix_g[l], ln_mix_b[l])
        xs = layer_norm(ALPHA * xs + ms, ln_mix_g[l], ln_mix_b[l])
        moe_params = (router_group_w[l], router_group_b[l], router_expert_w[l], router_expert_b[l],
                      expert_w_gate[l], expert_w_up[l], expert_w_down[l])
        xp = layer_norm(ALPHA * xp + hierarchical_moe(xp, *moe_params), ln_ffn_g[l], ln_ffn_b[l])
        xs = layer_norm(ALPHA * xs + hierarchical_moe(xs, *moe_params), ln_ffn_g[l], ln_ffn_b[l])
    return (xp[:, N_META:], xs, jnp.stack(swa_p), jnp.stack(swa_s), jnp.stack(ssm_p), jnp.stack(ssm_s),
            jnp.stack(pool_p), jnp.stack(pool_s), jnp.stack(mla_p), jnp.stack(mla_s))
```

```python
import functools
import math

import jax
import jax.numpy as jnp
from jax import lax
from jax.experimental import pallas as pl
from jax.experimental.pallas import tpu as pltpu

F32 = jnp.float32
BF16 = jnp.bfloat16

N_META = 16
BLOCK = 128
SUBLANES = 8
DH_A = 64
WINDOW = 128
P_B = 16
N_B = 64
POOL_WINDOWS = (2, 4, 8, 16)
POOL_MAX = 16
KV_LORA = 512
NOPE = 128
ROPE_DIM = 64
V_DIM = 128
ROPE_THETA = 10000.0
MLA_SCALE = (NOPE + ROPE_DIM) ** -0.5
PAGE_SIZE = 128
N_GROUPS_E = 4
EPG = 4
N_EXPERTS = N_GROUPS_E * EPG
TOP_K = 2
NEG = -1e30
EPS = 1e-5
SSM_CHUNK = 16
MOE_BLOCK = 256
PAGES_PER_STEP = 8
VMEM_LIMIT = 56 * 1024 * 1024


def _params(n_axes):
    return pltpu.CompilerParams(dimension_semantics=("arbitrary",) * n_axes,
                                vmem_limit_bytes=VMEM_LIMIT)


def _pick(n, cands):
    for c in cands:
        if n % c == 0:
            return c
    raise ValueError(f"no tile in {cands} divides {n}")


def _dot(a, b):
    return jnp.dot(a, b, preferred_element_type=F32)


def _dot_nt(a, b):
    return lax.dot_general(a, b, (((1,), (1,)), ((), ())), preferred_element_type=F32)


def _split(a):
    hi = a.astype(BF16)
    lo = (a - hi.astype(F32)).astype(BF16)
    return hi, lo


def _dot3(a, b):
    ah, al = _split(a)
    bh, bl = _split(b)
    return _dot(ah, bh) + (_dot(ah, bl) + _dot(al, bh))


def _mm_kernel(*refs, n_pieces):
    a_refs = refs[:n_pieces]
    w_refs = refs[n_pieces:2 * n_pieces]
    o_ref = refs[2 * n_pieces]
    wb_refs = refs[2 * n_pieces + 1:]

    @pl.when(pl.program_id(1) == 0)
    def _():
        for w_ref, wb_ref in zip(w_refs, wb_refs):
            wb_ref[...] = w_ref[...].astype(BF16)

    acc = _dot(a_refs[0][...].astype(BF16), wb_refs[0][...])
    for a_ref, wb_ref in zip(a_refs[1:], wb_refs[1:]):
        acc = acc + _dot(a_ref[...].astype(BF16), wb_ref[...])
    o_ref[...] = acc.astype(o_ref.dtype)


def matmul(pieces, w, *, kp, m_rows, tn, out_dtype, row_block0=0):
    n = w.shape[1]
    tm = _pick(m_rows, (512, 256, 128))
    assert n % tn == 0 and (row_block0 * tm) % tm == 0
    n_p = len(pieces)
    in_specs = []
    for _, cb in pieces:
        in_specs.append(pl.BlockSpec((tm, kp), lambda j, i, cb=cb: (i + row_block0, cb)))
    for p in range(n_p):
        in_specs.append(pl.BlockSpec((kp, tn), lambda j, i, p=p: (p, j)))
    return pl.pallas_call(
        functools.partial(_mm_kernel, n_pieces=n_p),
        grid=(n // tn, m_rows // tm),
        in_specs=in_specs,
        out_specs=pl.BlockSpec((tm, tn), lambda j, i: (i, j)),
        out_shape=jax.ShapeDtypeStruct((m_rows, n), out_dtype),
        scratch_shapes=[pltpu.VMEM((kp, tn), BF16) for _ in range(n_p)],
        compiler_params=_params(2),
        name="matmul",
    )(*[a for a, _ in pieces], *([w] * n_p))


def _ln_core(h, g_ref, b_ref, of_ref, ob_ref):
    mu = jnp.mean(h, -1, keepdims=True)
    hc = h - mu
    var = jnp.mean(hc * hc, -1, keepdims=True)
    y = hc * lax.rsqrt(var + EPS) * g_ref[...] + b_ref[...]
    of_ref[...] = y
    ob_ref[...] = y.astype(BF16)


def _ln_kernel(x_ref, m_ref, g_ref, b_ref, of_ref, ob_ref, *, alpha):
    _ln_core(alpha * x_ref[...] + m_ref[...].astype(F32), g_ref, b_ref, of_ref, ob_ref)


def _ln_moe_kernel(x_ref, y0_ref, y1_ref, gt_ref, g_ref, b_ref, of_ref, ob_ref, *, alpha):
    gt = gt_ref[...]
    moe = gt[:, 0:1] * y0_ref[...].astype(F32) + gt[:, 1:2] * y1_ref[...].astype(F32)
    _ln_core(alpha * x_ref[...] + moe, g_ref, b_ref, of_ref, ob_ref)


def _ln_call(kern, row_inputs, small_inputs, m, d):
    tm = _pick(m, (256, 128))
    row_spec = pl.BlockSpec((tm, d), lambda i: (i, 0))
    in_specs = [pl.BlockSpec((tm, a.shape[1]), lambda i: (i, 0)) for a in row_inputs]
    in_specs += [pl.BlockSpec(a.shape, lambda i: (0, 0)) for a in small_inputs]
    return pl.pallas_call(
        kern, grid=(m // tm,), in_specs=in_specs,
        out_specs=[row_spec, row_spec],
        out_shape=[jax.ShapeDtypeStruct((m, d), F32), jax.ShapeDtypeStruct((m, d), BF16)],
        compiler_params=_params(1), name="add_layer_norm",
    )(*row_inputs, *small_inputs)


def add_layer_norm(x, mix, g, b, alpha):
    m, d = x.shape
    return _ln_call(functools.partial(_ln_kernel, alpha=alpha), [x, mix],
                    [g.reshape(1, d), b.reshape(1, d)], m, d)


def moe_add_layer_norm(x, y0, y1, gates, g, b, alpha):
    m, d = x.shape
    return _ln_call(functools.partial(_ln_moe_kernel, alpha=alpha), [x, y0, y1, gates],
                    [g.reshape(1, d), b.reshape(1, d)], m, d)


def _alibi_slope(h, n_heads):
    return 2.0 ** (-8.0 * (h + 1) / n_heads)


def _swa_prompt_kernel(sink_ref, q_ref, kc_ref, vc_ref, kp_ref, vp_ref, km_ref, vm_ref, o_ref,
                       *, padf, n_heads, kvh):
    j = pl.program_id(1)
    base = j * BLOCK - padf
    r = lax.broadcasted_iota(jnp.int32, (BLOCK, 2 * BLOCK), 0)
    c = lax.broadcasted_iota(jnp.int32, (BLOCK, 2 * BLOCK), 1)
    dist = r + BLOCK - c
    vis_band = (dist >= 0) & (dist <= WINDOW) & (base - BLOCK + c >= N_META)
    dist_f = dist.astype(F32)
    rm = lax.broadcasted_iota(jnp.int32, (BLOCK, N_META), 0)
    cm = lax.broadcasted_iota(jnp.int32, (BLOCK, N_META), 1)
    dist_m = base + rm - cm
    vis_meta = dist_m >= 0
    dist_mf = jnp.abs(dist_m).astype(F32)
    group = n_heads // kvh
    scale = DH_A ** -0.5
    for h in range(kvh):
        sl = slice(h * DH_A, (h + 1) * DH_A)
        k_band = jnp.concatenate([kp_ref[:, sl], kc_ref[:, sl]], 0).astype(BF16)
        v_band = jnp.concatenate([vp_ref[:, sl], vc_ref[:, sl]], 0).astype(BF16)
        k_meta = km_ref[:, sl].astype(BF16)
        v_meta = vm_ref[:, sl].astype(BF16)
        for g in range(group):
            hh = h * group + g
            hs = slice(hh * DH_A, (hh + 1) * DH_A)
            slope = _alibi_slope(hh, n_heads)
            sink = sink_ref[hh]
            q = q_ref[:, hs].astype(BF16)
            lb = jnp.where(vis_band, _dot_nt(q, k_band) * scale - slope * dist_f, NEG)
            lm = jnp.where(vis_meta, _dot_nt(q, k_meta) * scale - slope * dist_mf, NEG)
            mx = jnp.maximum(jnp.maximum(jnp.max(lb, -1, keepdims=True),
                                         jnp.max(lm, -1, keepdims=True)), sink)
            eb = jnp.exp(lb - mx)
            em = jnp.exp(lm - mx)
            den = (jnp.sum(eb, -1, keepdims=True) + jnp.sum(em, -1, keepdims=True)
                   + jnp.exp(sink - mx))
            inv = 1.0 / den
            o = _dot((eb * inv).astype(BF16), v_band) + _dot((em * inv).astype(BF16), v_meta)
            o_ref[:, hs] = o.astype(o_ref.dtype)


def swa_prompt(z, sinks, *, n_batch, tp, padf, w_a, kv_w):
    nb = tp // BLOCK
    kcol = w_a // kv_w
    meta_blk = padf // N_META
    rows_meta = tp // N_META

    def cur(col):
        return pl.BlockSpec((BLOCK, kv_w), lambda b, j: (b * nb + j, col))

    def prev(col):
        return pl.BlockSpec((BLOCK, kv_w), lambda b, j: (jnp.maximum(b * nb + j - 1, 0), col))

    def meta(col):
        return pl.BlockSpec((N_META, kv_w), lambda b, j: (b * rows_meta + meta_blk, col))

    n_heads = w_a // DH_A
    return pl.pallas_call(
        functools.partial(_swa_prompt_kernel, padf=padf, n_heads=n_heads, kvh=kv_w // DH_A),
        grid=(n_batch, nb),
        in_specs=[pl.BlockSpec(memory_space=pltpu.SMEM),
                  pl.BlockSpec((BLOCK, w_a), lambda b, j: (b * nb + j, 0)),
                  cur(kcol), cur(kcol + 1), prev(kcol), prev(kcol + 1), meta(kcol), meta(kcol + 1)],
        out_specs=pl.BlockSpec((BLOCK, w_a), lambda b, j: (b * nb + j, 0)),
        out_shape=jax.ShapeDtypeStruct((n_batch * tp, w_a), BF16),
        compiler_params=_params(2), name="swa_prompt",
    )(sinks, z, z, z, z, z, z, z)


def _swa_sample_kernel(sink_ref, slope_ref, q_ref, kn_ref, vn_ref, cache_ref, o_ref, kv_ref,
                       *, bb, t_new, past_len, n_heads, kvh):
    group = n_heads // kvh
    win = WINDOW
    n_keys = 2 * BLOCK
    n_fill = n_keys - (N_META + win + t_new)
    rows = group * t_new
    r = lax.broadcasted_iota(jnp.int32, (rows, n_keys), 0)
    c = lax.broadcasted_iota(jnp.int32, (rows, n_keys), 1)
    q_pos = past_len + r % t_new
    k_pos = jnp.where(c < N_META, c,
                      jnp.where(c < N_META + win, past_len - win + (c - N_META),
                                past_len + (c - N_META - win)))
    dist = q_pos - k_pos
    visible = (dist >= 0) & ((c < N_META) | ((k_pos >= N_META) & (dist <= WINDOW)))
    fill = jnp.zeros((n_fill, DH_A), F32)
    dist_f = jnp.abs(dist).astype(F32)
    scale = DH_A ** -0.5
    for b in range(bb):
        rs = slice(b * t_new, (b + 1) * t_new)
        for kv in range(2):
            new_ref = kn_ref if kv == 0 else vn_ref
            kv_ref[b, kv, 0:N_META, :] = cache_ref[b, kv, 0:N_META, :]
            kv_ref[b, kv, N_META:N_META + win - t_new, :] = cache_ref[b, kv, N_META + t_new:N_META + win, :]
            kv_ref[b, kv, N_META + win - t_new:N_META + win, :] = new_ref[rs, :]
        for h in range(kvh):
            sl = slice(h * DH_A, (h + 1) * DH_A)
            k_all = jnp.concatenate([cache_ref[b, 0, :, sl], kn_ref[rs, sl], fill], 0).astype(BF16)
            v_all = jnp.concatenate([cache_ref[b, 1, :, sl], vn_ref[rs, sl], fill], 0).astype(BF16)
            q = jnp.concatenate(
                [q_ref[rs, (h * group + g) * DH_A:(h * group + g + 1) * DH_A] for g in range(group)],
                0).astype(BF16)
            sink = sink_ref[h]
            logits = jnp.where(visible, _dot_nt(q, k_all) * scale - slope_ref[h] * dist_f, NEG)
            mx = jnp.maximum(jnp.max(logits, -1, keepdims=True), sink)
            e = jnp.exp(logits - mx)
            den = jnp.sum(e, -1, keepdims=True) + jnp.exp(sink - mx)
            o = _dot((e * (1.0 / den)).astype(BF16), v_all)
            for g in range(group):
                hh = h * group + g
                o_ref[rs, hh * DH_A:(hh + 1) * DH_A] = o[g * t_new:(g + 1) * t_new].astype(o_ref.dtype)


def swa_sample(z, cache, sinks, *, row0, n_dec, t_new, past_len, w_a, kv_w):
    bb = _pick(n_dec, (8, 4, 2, 1))
    n_heads = w_a // DH_A
    kvh = kv_w // DH_A
    group = n_heads // kvh
    rb0 = row0 // (bb * t_new)
    assert row0 % (bb * t_new) == 0 and t_new == SUBLANES and cache.shape[2] == N_META + WINDOW
    kcol = w_a // kv_w
    sink_rows = jnp.repeat(sinks.astype(F32).reshape(kvh, group), t_new, axis=1)[..., None]
    slopes = jnp.asarray([_alibi_slope(h, n_heads) for h in range(n_heads)], F32)
    slope_rows = jnp.repeat(slopes.reshape(kvh, group), t_new, axis=1)[..., None]
    cache4 = cache.reshape(n_dec, 2, N_META + WINDOW, kv_w)
    small = pl.BlockSpec((kvh, group * t_new, 1), lambda i: (0, 0, 0))
    cache_spec = pl.BlockSpec((bb, 2, N_META + WINDOW, kv_w), lambda i: (i, 0, 0, 0))
    return pl.pallas_call(
        functools.partial(_swa_sample_kernel, bb=bb, t_new=t_new, past_len=past_len,
                          n_heads=n_heads, kvh=kvh),
        grid=(n_dec // bb,),
        in_specs=[small, small,
                  pl.BlockSpec((bb * t_new, w_a), lambda i: (rb0 + i, 0)),
                  pl.BlockSpec((bb * t_new, kv_w), lambda i: (rb0 + i, kcol)),
                  pl.BlockSpec((bb * t_new, kv_w), lambda i: (rb0 + i, kcol + 1)),
                  cache_spec],
        out_specs=[pl.BlockSpec((bb * t_new, w_a), lambda i: (i, 0)), cache_spec],
        out_shape=[jax.ShapeDtypeStruct((n_dec * t_new, w_a), BF16),
                   jax.ShapeDtypeStruct(cache4.shape, F32)],
        compiler_params=_params(1), name="swa_sample",
    )(sink_rows, slope_rows, z, z, z, cache4)


def ssm_matrices(lam_re, lam_im, log_dt, b_re, b_im, c_re, c_im, d_skip, length):
    hi = lax.Precision.HIGHEST
    g_n, n_n = lam_re.shape
    lam = lax.complex(lam_re.astype(F32), lam_im.astype(F32))
    lam_dt = lam * jnp.exp(log_dt.astype(F32))[:, None]
    lam_bar = jnp.exp(lam_dt)
    b_bar = ((lam_bar - 1.0) / lam)[..., None] * lax.complex(b_re.astype(F32), b_im.astype(F32))
    cc = lax.complex(c_re.astype(F32), c_im.astype(F32))
    steps = jnp.arange(length + 1, dtype=F32)
    pw = jnp.exp(lam_dt[None] * steps[:, None, None])
    cb = cc[None] * pw[:length, :, None, :]
    k_lag = (jnp.einsum('jgpn,gnq->jgpq', cb.real, b_bar.real, precision=hi)
             - jnp.einsum('jgpn,gnq->jgpq', cb.imag, b_bar.imag, precision=hi))
    t = jnp.arange(length)
    lag = t[None, :] - t[:, None]
    k_st = jnp.where((lag >= 0)[:, :, None, None, None], k_lag[jnp.clip(lag, 0)], 0.0)
    mt = k_st.transpose(2, 0, 4, 1, 3)
    eye_t = jnp.eye(length, dtype=F32)
    eye_p = jnp.eye(P_B, dtype=F32)
    mt = mt + (eye_t[None, :, None, :, None] * eye_p[None, None, :, None, :]
               * d_skip.astype(F32).reshape(g_n, P_B)[:, None, None, None, :])
    mt = mt.reshape(g_n, length * P_B, length * P_B)
    sin_c = (pw[length - 1 - t][:, :, :, None] * b_bar[None]).transpose(1, 0, 3, 2)
    sin_t = jnp.concatenate([sin_c.real, sin_c.imag], -1).reshape(g_n, length * P_B, 2 * n_n)
    g_c = (cc[None] * pw[1:length + 1][:, :, None, :]).transpose(1, 3, 0, 2)
    sout_t = jnp.concatenate([g_c.real, -g_c.imag], 1).reshape(g_n, 2 * n_n, length * P_B)
    lam_l = pw[length]
    lam_a = jnp.concatenate([lam_l.real, lam_l.real], -1)[:, None, :]
    lam_b = jnp.concatenate([-lam_l.imag, lam_l.imag], -1)[:, None, :]
    return mt, sin_t, sout_t, lam_a, lam_b


def _ssm_prompt_kernel(u_ref, mt_ref, sin_ref, sout_ref, la_ref, lb_ref, y_ref, h_ref, e_s, sp_s,
                       *, nc, bp):
    u = u_ref[...]
    e_s[...] = _dot3(u, sin_ref[...]).reshape(nc, bp, 2 * N_B)
    la = la_ref[...]
    lb = lb_ref[...]

    def body(c, s):
        sp_s[c] = s
        return la * s + lb * pltpu.roll(s, N_B, 1) + e_s[c]

    s_last = lax.fori_loop(0, nc, body, jnp.zeros((bp, 2 * N_B), F32))
    h_ref[...] = s_last
    y_ref[...] = (_dot3(u, mt_ref[...])
                  + _dot3(sp_s[...].reshape(nc * bp, 2 * N_B), sout_ref[...]))


def ssm_prompt(u, mats, *, nc, bp):
    mt, sin_t, sout_t, lam_a, lam_b = mats
    g_n = u.shape[0]
    lp = SSM_CHUNK * P_B
    n2 = 2 * N_B

    def per_group(shape):
        return pl.BlockSpec((None,) + shape, lambda g: (g, 0, 0))

    return pl.pallas_call(
        functools.partial(_ssm_prompt_kernel, nc=nc, bp=bp),
        grid=(g_n,),
        in_specs=[per_group((nc * bp, lp)), per_group((lp, lp)), per_group((lp, n2)),
                  per_group((n2, lp)), per_group((1, n2)), per_group((1, n2))],
        out_specs=[per_group((nc * bp, lp)), per_group((bp, n2))],
        out_shape=[jax.ShapeDtypeStruct((g_n, nc * bp, lp), F32),
                   jax.ShapeDtypeStruct((g_n, bp, n2), F32)],
        scratch_shapes=[pltpu.VMEM((nc, bp, n2), F32), pltpu.VMEM((nc, bp, n2), F32)],
        compiler_params=_params(1), name="ssm_prompt",
    )(u, mt, sin_t, sout_t, lam_a, lam_b)


def _ssm_sample_kernel(u_ref, s0_ref, mt_ref, sin_ref, sout_ref, la_ref, lb_ref, y_ref, h_ref):
    u = u_ref[...]
    s0 = s0_ref[...]
    y_ref[...] = _dot3(u, mt_ref[...]) + _dot3(s0, sout_ref[...])
    h_ref[...] = (la_ref[...] * s0 + lb_ref[...] * pltpu.roll(s0, N_B, 1)
                  + _dot3(u, sin_ref[...]))


def ssm_sample(u, s0, mats):
    mt, sin_t, sout_t, lam_a, lam_b = mats
    g_n, n_dec, lp = u.shape
    n2 = 2 * N_B

    def per_group(shape):
        return pl.BlockSpec((None,) + shape, lambda g: (g, 0, 0))

    return pl.pallas_call(
        _ssm_sample_kernel, grid=(g_n,),
        in_specs=[per_group((n_dec, lp)), per_group((n_dec, n2)), per_group((lp, lp)),
                  per_group((lp, n2)), per_group((n2, lp)), per_group((1, n2)), per_group((1, n2))],
        out_specs=[per_group((n_dec, lp)), per_group((n_dec, n2))],
        out_shape=[jax.ShapeDtypeStruct((g_n, n_dec, lp), F32),
                   jax.ShapeDtypeStruct((g_n, n_dec, n2), F32)],
        compiler_params=_params(1), name="ssm_sample",
    )(u, s0, mt, sin_t, sout_t, lam_a, lam_b)


def _glu_kernel(y_ref, gate_ref, o_ref):
    y = y_ref[...]
    gelu = 0.5 * y * (1.0 + jnp.tanh(math.sqrt(2.0 / math.pi) * (y + 0.044715 * (y * y * y))))
    o_ref[...] = (gelu * jax.nn.sigmoid(gate_ref[...])).astype(o_ref.dtype)


def glu(y, z, *, gate_col0):
    m, w = y.shape
    tm = _pick(m, (512, 256, 128))
    tc = 512
    assert gate_col0 % tc == 0 and w % tc == 0
    return pl.pallas_call(
        _glu_kernel, grid=(m // tm, w // tc),
        in_specs=[pl.BlockSpec((tm, tc), lambda i, c: (i, c)),
                  pl.BlockSpec((tm, tc), lambda i, c: (i, gate_col0 // tc + c))],
        out_specs=pl.BlockSpec((tm, tc), lambda i, c: (i, c)),
        out_shape=jax.ShapeDtypeStruct((m, w), BF16),
        compiler_params=_params(2), name="glu",
    )(y, z)


def _window_sums(ext, width):
    n = ext.shape[0]
    acc = ext
    k = 1
    while k < width:
        acc = acc[:n - 2 * k + 1] + acc[k:n - k + 1]
        k *= 2
    first = POOL_MAX - (width - 1)
    return acc[first:first + n - POOL_MAX]


def _pool_mix(ext, cur, count_fn, pw_ref, sc_ref, o_ref):
    cg = cur.shape[1] // len(POOL_WINDOWS)
    for g, width in enumerate(POOL_WINDOWS):
        sl = slice(g * cg, (g + 1) * cg)
        pooled = _window_sums(ext[:, sl], width) / count_fn(width) - cur[:, sl]
        mixed = _dot(pooled.astype(BF16), pw_ref[g].astype(BF16))
        o_ref[:, sl] = (mixed * sc_ref[:, sl]).astype(o_ref.dtype)


def _pool_prompt_kernel(cur_ref, prev_ref, pw_ref, sc_ref, o_ref, *, padf):
    j = pl.program_id(1)
    base = j * BLOCK - padf
    pos = base + lax.broadcasted_iota(jnp.int32, (BLOCK, 1), 0)
    pos_prev = base - POOL_MAX + lax.broadcasted_iota(jnp.int32, (POOL_MAX, 1), 0)
    cur = jnp.where(pos >= 0, cur_ref[...], 0.0)
    prev = jnp.where(pos_prev >= 0, prev_ref[...], 0.0)
    ext = jnp.concatenate([prev, cur], 0)

    def count(width):
        return jnp.maximum(jnp.minimum(pos + 1, width), 1).astype(F32)

    _pool_mix(ext, cur, count, pw_ref, sc_ref, o_ref)


def pool_prompt(z, pool_w, pool_scale, *, n_batch, tp, padf, c_width):
    nb = tp // BLOCK
    per = BLOCK // POOL_MAX
    return pl.pallas_call(
        functools.partial(_pool_prompt_kernel, padf=padf),
        grid=(n_batch, nb),
        in_specs=[pl.BlockSpec((BLOCK, c_width), lambda b, j: (b * nb + j, 0)),
                  pl.BlockSpec((POOL_MAX, c_width),
                               lambda b, j: (jnp.maximum((b * nb + j) * per - 1, 0), 0)),
                  pl.BlockSpec(pool_w.shape, lambda b, j: (0, 0, 0)),
                  pl.BlockSpec((1, c_width), lambda b, j: (0, 0))],
        out_specs=pl.BlockSpec((BLOCK, c_width), lambda b, j: (b * nb + j, 0)),
        out_shape=jax.ShapeDtypeStruct((n_batch * tp, c_width), BF16),
        compiler_params=_params(2), name="pool_prompt",
    )(z, z, pool_w, pool_scale.reshape(1, c_width))


def _pool_sample_kernel(cur_ref, buf_ref, pw_ref, sc_ref, o_ref, *, bb, t_new):
    cur = cur_ref[...]
    cg = cur.shape[1] // len(POOL_WINDOWS)
    for g, width in enumerate(POOL_WINDOWS):
        sl = slice(g * cg, (g + 1) * cg)
        sums = [_window_sums(jnp.concatenate([buf_ref[b, :, sl], cur[b * t_new:(b + 1) * t_new, sl]], 0),
                             width) for b in range(bb)]
        pooled = jnp.concatenate(sums, 0) / float(width) - cur[:, sl]
        mixed = _dot(pooled.astype(BF16), pw_ref[g].astype(BF16))
        o_ref[:, sl] = (mixed * sc_ref[:, sl]).astype(o_ref.dtype)


def pool_sample(z, buf16, pool_w, pool_scale, *, row0, n_dec, t_new, c_width):
    bb = _pick(n_dec, (8, 4, 2, 1))
    rb0 = row0 // (bb * t_new)
    assert row0 % (bb * t_new) == 0
    return pl.pallas_call(
        functools.partial(_pool_sample_kernel, bb=bb, t_new=t_new),
        grid=(n_dec // bb,),
        in_specs=[pl.BlockSpec((bb * t_new, c_width), lambda i: (rb0 + i, 0)),
                  pl.BlockSpec((bb, POOL_MAX, c_width), lambda i: (i, 0, 0)),
                  pl.BlockSpec(pool_w.shape, lambda i: (0, 0, 0)),
                  pl.BlockSpec((1, c_width), lambda i: (0, 0))],
        out_specs=pl.BlockSpec((bb * t_new, c_width), lambda i: (i, 0)),
        out_shape=jax.ShapeDtypeStruct((n_dec * t_new, c_width), BF16),
        compiler_params=_params(1), name="pool_sample",
    )(z, buf16, pool_w, pool_scale.reshape(1, c_width))


def _rms(x, g):
    return x * lax.rsqrt(jnp.mean(x * x, -1, keepdims=True) + EPS) * g


def _mla_prep_kernel(cq_ref, ckv_ref, kr_ref, cs_ref, qg_ref, kg_ref, cqn_ref, rows_ref, ckvb_ref, krb_ref):
    cqn_ref[...] = _rms(cq_ref[...], qg_ref[...]).astype(BF16)
    ckvn = _rms(ckv_ref[...], kg_ref[...])
    kr2 = kr_ref[...]
    cs = cs_ref[...]
    krr = kr2[:, :ROPE_DIM] * cs[:, :ROPE_DIM] + kr2[:, ROPE_DIM:] * cs[:, ROPE_DIM:]
    rows_ref[:, :KV_LORA] = ckvn
    rows_ref[:, KV_LORA:] = krr
    ckvb_ref[...] = ckvn.astype(BF16)
    krb_ref[...] = krr.astype(BF16)


def mla_prep(z, cs_rows, q_norm, kv_norm, *, c_width, q_lora):
    m = z.shape[0]
    tm = _pick(m, (256, 128))
    assert c_width % q_lora == 0 and (c_width + q_lora) % KV_LORA == 0
    kr_col = (c_width + q_lora + KV_LORA) // (2 * ROPE_DIM)
    row = lambda w, cb: pl.BlockSpec((tm, w), lambda i: (i, cb))
    return pl.pallas_call(
        _mla_prep_kernel, grid=(m // tm,),
        in_specs=[row(q_lora, c_width // q_lora), row(KV_LORA, (c_width + q_lora) // KV_LORA),
                  row(2 * ROPE_DIM, kr_col), row(2 * ROPE_DIM, 0),
                  pl.BlockSpec((1, q_lora), lambda i: (0, 0)), pl.BlockSpec((1, KV_LORA), lambda i: (0, 0))],
        out_specs=[row(q_lora, 0), row(KV_LORA + ROPE_DIM, 0), row(KV_LORA, 0), row(ROPE_DIM, 0)],
        out_shape=[jax.ShapeDtypeStruct((m, q_lora), BF16),
                   jax.ShapeDtypeStruct((m, KV_LORA + ROPE_DIM), F32),
                   jax.ShapeDtypeStruct((m, KV_LORA), BF16),
                   jax.ShapeDtypeStruct((m, ROPE_DIM), BF16)],
        compiler_params=_params(1), name="mla_prep",
    )(z, z, z, cs_rows, q_norm.reshape(1, q_lora), kv_norm.reshape(1, KV_LORA))


def _mla_prompt_kernel(qn_ref, qr_ref, qs_ref, cos_ref, sin_ref, kn_ref, v_ref, kr_ref, o_ref,
                       *, padf, heads):
    j = pl.program_id(2)
    r = lax.broadcasted_iota(jnp.int32, (BLOCK, BLOCK), 0)
    c = lax.broadcasted_iota(jnp.int32, (BLOCK, BLOCK), 1)
    q_pos = j * BLOCK + r - padf
    q_rope = (qr_ref[...] * cos_ref[...] + qs_ref[...] * sin_ref[...]).astype(BF16)
    q_nope = qn_ref[...].astype(BF16)

    def body(kt, carry):
        row0 = pl.multiple_of(kt * BLOCK, BLOCK)
        k_pos = kt * BLOCK + c - padf
        valid = (k_pos <= q_pos) & (k_pos >= 0)
        kr_t = kr_ref[pl.ds(row0, BLOCK), :]
        out = []
        for h in range(heads):
            m_i, l_i, acc = carry[h]
            kn_t = kn_ref[pl.ds(row0, BLOCK), h * NOPE:(h + 1) * NOPE]
            v_t = v_ref[pl.ds(row0, BLOCK), h * V_DIM:(h + 1) * V_DIM]
            s = (_dot_nt(q_nope[:, h * NOPE:(h + 1) * NOPE], kn_t)
                 + _dot_nt(q_rope[:, h * ROPE_DIM:(h + 1) * ROPE_DIM], kr_t)) * MLA_SCALE
            s = jnp.where(valid, s, NEG)
            m_new = jnp.maximum(m_i, jnp.max(s, -1, keepdims=True))
            a = jnp.exp(m_i - m_new)
            p = jnp.exp(s - m_new)
            l_new = a * l_i + jnp.sum(p, -1, keepdims=True)
            acc_new = a * acc + _dot(p.astype(BF16), v_t)
            out.append((m_new, l_new, acc_new))
        return tuple(out)

    init = tuple((jnp.full((BLOCK, 1), NEG, F32), jnp.zeros((BLOCK, 1), F32),
                  jnp.zeros((BLOCK, V_DIM), F32)) for _ in range(heads))
    res = lax.fori_loop(0, j + 1, body, init)
    for h in range(heads):
        _, l_i, acc = res[h]
        o_ref[:, h * V_DIM:(h + 1) * V_DIM] = (acc / l_i).astype(o_ref.dtype)


def mla_prompt(qfull, cos_t, sin_t, kv, krb, *, n_batch, tp, padf, n_heads):
    heads = 2
    nb = tp // BLOCK
    hp_n = n_heads // heads
    nope_w = n_heads * NOPE
    rope_blk0 = nope_w // (heads * ROPE_DIM)
    return pl.pallas_call(
        functools.partial(_mla_prompt_kernel, padf=padf, heads=heads),
        grid=(n_batch, hp_n, nb),
        in_specs=[pl.BlockSpec((BLOCK, heads * NOPE), lambda b, hp, j: (b * nb + j, hp)),
                  pl.BlockSpec((BLOCK, heads * ROPE_DIM), lambda b, hp, j: (b * nb + j, rope_blk0 + hp)),
                  pl.BlockSpec((BLOCK, heads * ROPE_DIM),
                               lambda b, hp, j: (b * nb + j, rope_blk0 + hp_n + hp)),
                  pl.BlockSpec((BLOCK, heads * ROPE_DIM), lambda b, hp, j: (j, 0)),
                  pl.BlockSpec((BLOCK, heads * ROPE_DIM), lambda b, hp, j: (j, 0)),
                  pl.BlockSpec((tp, heads * NOPE), lambda b, hp, j: (b, hp)),
                  pl.BlockSpec((tp, heads * V_DIM), lambda b, hp, j: (b, hp_n + hp)),
                  pl.BlockSpec((tp, ROPE_DIM), lambda b, hp, j: (b, 0))],
        out_specs=pl.BlockSpec((BLOCK, heads * V_DIM), lambda b, hp, j: (b * nb + j, hp)),
        out_shape=jax.ShapeDtypeStruct((n_batch * tp, n_heads * V_DIM), BF16),
        compiler_params=_params(3), name="mla_prompt",
    )(qfull, qfull, qfull, cos_t, sin_t, kv, kv, krb)


def _head_proj_kernel(a_ref, w_ref, o_ref):
    o_ref[...] = _dot(a_ref[...].astype(BF16), w_ref[...].astype(BF16)).astype(o_ref.dtype)


def head_proj(a, w, *, row0, m_rows, k_head, n_head, a_col0=0):
    n_heads = w.shape[0]
    tm = _pick(math.gcd(row0, m_rows) if row0 else m_rows, (512, 256, 128))
    assert a_col0 % k_head == 0
    return pl.pallas_call(
        _head_proj_kernel, grid=(n_heads, m_rows // tm),
        in_specs=[pl.BlockSpec((tm, k_head), lambda h, i: (row0 // tm + i, a_col0 // k_head + h)),
                  pl.BlockSpec((None, k_head, n_head), lambda h, i: (h, 0, 0))],
        out_specs=pl.BlockSpec((tm, n_head), lambda h, i: (i, h)),
        out_shape=jax.ShapeDtypeStruct((m_rows, n_heads * n_head), BF16),
        compiler_params=_params(2), name="head_proj",
    )(a, w)


def _mla_sample_kernel(pt_ref, ql_ref, qr_ref, qs_ref, cos_ref, sin_ref, new_ref, *rest,
                       n_pages, n_heads, n_steps):
    page_refs = rest[:n_pages]
    o_ref = rest[n_pages]
    q_s, m_s, l_s, acc_s = rest[n_pages + 1:]
    step = pl.program_id(1)
    rows = q_s.shape[0]

    @pl.when(step == 0)
    def _():
        q_s[:, :KV_LORA] = ql_ref[...]
        q_s[:, KV_LORA:] = (qr_ref[...] * cos_ref[...] + qs_ref[...] * sin_ref[...]).astype(BF16)
        new = new_ref[...]
        new = jnp.concatenate([new, jnp.zeros_like(new)], 0).astype(BF16)
        t_pad = new.shape[0]
        s = _dot_nt(q_s[...], new) * MLA_SCALE
        t_q = lax.broadcasted_iota(jnp.int32, (rows, t_pad), 0) // n_heads
        t_k = lax.broadcasted_iota(jnp.int32, (rows, t_pad), 1)
        s = jnp.where(t_k <= t_q, s, NEG)
        mx = jnp.max(s, -1, keepdims=True)
        p = jnp.exp(s - mx)
        m_s[...] = mx
        l_s[...] = jnp.sum(p, -1, keepdims=True)
        acc_s[...] = _dot(p.astype(BF16), new[:, :KV_LORA])

    keys = jnp.concatenate([ref[...].astype(BF16) for ref in page_refs], 0)
    s = _dot_nt(q_s[...], keys) * MLA_SCALE
    m_old = m_s[...]
    m_new = jnp.maximum(m_old, jnp.max(s, -1, keepdims=True))
    a = jnp.exp(m_old - m_new)
    p = jnp.exp(s - m_new)
    m_s[...] = m_new
    l_s[...] = a * l_s[...] + jnp.sum(p, -1, keepdims=True)
    acc_s[...] = a * acc_s[...] + _dot(p.astype(BF16), keys[:, :KV_LORA])

    @pl.when(step == n_steps - 1)
    def _():
        o_ref[...] = (acc_s[...] / l_s[...]).astype(o_ref.dtype)


def mla_sample(page_table, q_lat, q_r, q_s, cos_t, sin_t, rows, cache, *, row0, n_dec, t_new, n_heads):
    n_pg = page_table.shape[1]
    pps = _pick(n_pg, (PAGES_PER_STEP, 4, 2, 1))
    n_steps = n_pg // pps
    qrows = t_new * n_heads
    assert row0 % t_new == 0
    row_spec = lambda w: pl.BlockSpec((qrows, w), lambda b, s, pt: (b, 0))
    tab_spec = pl.BlockSpec((qrows, ROPE_DIM), lambda b, s, pt: (0, 0))

    def page_spec(k):
        return pl.BlockSpec((None, PAGE_SIZE, KV_LORA + ROPE_DIM),
                            lambda b, s, pt, k=k: (pt[b * n_pg + s * pps + k], 0, 0))

    grid_spec = pltpu.PrefetchScalarGridSpec(
        num_scalar_prefetch=1, grid=(n_dec, n_steps),
        in_specs=[row_spec(KV_LORA), row_spec(ROPE_DIM), row_spec(ROPE_DIM), tab_spec, tab_spec,
                  pl.BlockSpec((t_new, KV_LORA + ROPE_DIM), lambda b, s, pt: (row0 // t_new + b, 0))]
                 + [page_spec(k) for k in range(pps)],
        out_specs=row_spec(KV_LORA),
        scratch_shapes=[pltpu.VMEM((qrows, KV_LORA + ROPE_DIM), BF16), pltpu.VMEM((qrows, 1), F32),
                        pltpu.VMEM((qrows, 1), F32), pltpu.VMEM((qrows, KV_LORA), F32)])
    return pl.pallas_call(
        functools.partial(_mla_sample_kernel, n_pages=pps, n_heads=n_heads, n_steps=n_steps),
        grid_spec=grid_spec,
        out_shape=jax.ShapeDtypeStruct((n_dec * qrows, KV_LORA), BF16),
        compiler_params=_params(2), name="mla_sample",
    )(page_table.reshape(-1), q_lat, q_r, q_s, cos_t, sin_t, rows, *([cache] * pps))


def _router_kernel(x_ref, w_ref, b_ref, o_ref):
    o_ref[...] = _dot3(x_ref[...], w_ref[...]) + b_ref[...]


def router_logits(x, w, b):
    m, d = x.shape
    tm = _pick(m, (256, 128))
    n = w.shape[1]
    return pl.pallas_call(
        _router_kernel, grid=(m // tm,),
        in_specs=[pl.BlockSpec((tm, d), lambda i: (i, 0)), pl.BlockSpec((d, n), lambda i: (0, 0)),
                  pl.BlockSpec((1, n), lambda i: (0, 0))],
        out_specs=pl.BlockSpec((tm, n), lambda i: (i, 0)),
        out_shape=jax.ShapeDtypeStruct((m, n), F32),
        compiler_params=_params(1), name="router",
    )(x, w, b)


def _gmm_kernel(be_ref, nu_ref, x_ref, *rest, n_w, gated):
    w_refs = rest[:n_w]
    o_ref = rest[n_w]
    wb_refs = rest[n_w + 1:]
    j = pl.program_id(1)
    active = j < nu_ref[0]
    changed = (j == 0) | (be_ref[j] != be_ref[jnp.maximum(j - 1, 0)])

    @pl.when(active & changed)
    def _():
        for w_ref, wb_ref in zip(w_refs, wb_refs):
            wb_ref[...] = w_ref[...].astype(BF16)

    @pl.when(active)
    def _():
        x = x_ref[...]
        if gated:
            gate = _dot(x, wb_refs[0][...])
            up = _dot(x, wb_refs[1][...])
            o_ref[...] = (gate * jax.nn.sigmoid(gate) * up).astype(o_ref.dtype)
        else:
            o_ref[...] = _dot(x, wb_refs[0][...]).astype(o_ref.dtype)

    @pl.when(jnp.logical_not(active))
    def _():
        o_ref[...] = jnp.zeros(o_ref.shape, o_ref.dtype)


def grouped_matmul(x, weights, block_expert, n_used, *, tn, out_dtype, gated):
    n_slots, k = x.shape
    n = weights[0].shape[2]
    n_w = len(weights)
    grid_spec = pltpu.PrefetchScalarGridSpec(
        num_scalar_prefetch=2, grid=(n // tn, n_slots // MOE_BLOCK),
        in_specs=[pl.BlockSpec((MOE_BLOCK, k), lambda c, j, be, nu: (j, 0))]
                 + [pl.BlockSpec((None, k, tn), lambda c, j, be, nu: (be[j], 0, c)) for _ in range(n_w)],
        out_specs=pl.BlockSpec((MOE_BLOCK, tn), lambda c, j, be, nu: (j, c)),
        scratch_shapes=[pltpu.VMEM((k, tn), BF16) for _ in range(n_w)])
    return pl.pallas_call(
        functools.partial(_gmm_kernel, n_w=n_w, gated=gated),
        grid_spec=grid_spec,
        out_shape=jax.ShapeDtypeStruct((n_slots, n), out_dtype),
        compiler_params=_params(2), name="grouped_matmul",
    )(block_expert, n_used, x, *weights)


def moe_layer(x, xb, valid, rg_w, rg_b, re_w, re_b, w_gate, w_up, w_down, ln_g, ln_b, alpha):
    m, d = x.shape
    n_route = N_GROUPS_E + N_EXPERTS
    w_r = jnp.zeros((d, BLOCK), F32).at[:, :N_GROUPS_E].set(rg_w.astype(F32))
    w_r = w_r.at[:, N_GROUPS_E:n_route].set(re_w.astype(F32))
    b_r = jnp.zeros((1, BLOCK), F32).at[0, :N_GROUPS_E].set(rg_b.astype(F32))
    b_r = b_r.at[0, N_GROUPS_E:n_route].set(re_b.astype(F32))
    logits = router_logits(x, w_r, b_r)
    rows = jnp.arange(m)
    g_logits = logits[:, :N_GROUPS_E]
    g_idx = jnp.argmax(g_logits, -1).astype(jnp.int32)
    g_gate = jax.nn.softmax(g_logits, -1)[rows, g_idx][:, None]
    e_logits = logits[:, N_GROUPS_E:n_route].reshape(m, N_GROUPS_E, EPG)[rows, g_idx]
    top_logit, top_j = lax.top_k(e_logits, TOP_K)
    gates = g_gate * jax.nn.softmax(top_logit, -1)
    experts = g_idx[:, None] * EPG + top_j.astype(jnp.int32)
    experts = jnp.where(valid[:, None], experts, N_EXPERTS)

    n_assign = m * TOP_K
    e_flat = experts.reshape(n_assign)
    order = jnp.argsort(e_flat)
    e_sorted = e_flat[order]
    counts = jnp.bincount(e_flat, length=N_EXPERTS + 1)[:N_EXPERTS]
    padded = (counts + MOE_BLOCK - 1) // MOE_BLOCK * MOE_BLOCK
    pad_end = jnp.cumsum(padded)
    pad_start = pad_end - padded
    start = jnp.cumsum(counts) - counts
    n_blocks = (n_assign + N_EXPERTS * (MOE_BLOCK - 1) + MOE_BLOCK - 1) // MOE_BLOCK
    n_slots = n_blocks * MOE_BLOCK
    e_clip = jnp.minimum(e_sorted, N_EXPERTS - 1)
    dest = jnp.where(e_sorted < N_EXPERTS,
                     pad_start[e_clip] + jnp.arange(n_assign, dtype=jnp.int32) - start[e_clip],
                     n_slots).astype(jnp.int32)
    slot_tok = jnp.zeros((n_slots,), jnp.int32).at[dest].set((order // TOP_K).astype(jnp.int32), mode='drop')
    block_expert = jnp.minimum(
        jnp.searchsorted(pad_end, jnp.arange(n_blocks, dtype=jnp.int32) * MOE_BLOCK, side='right'),
        N_EXPERTS - 1).astype(jnp.int32)
    n_used = (pad_end[-1] // MOE_BLOCK).astype(jnp.int32).reshape(1)
    slot_of = jnp.zeros((n_assign,), jnp.int32).at[order].set(jnp.minimum(dest, n_slots - 1))

    xs = xb[slot_tok]
    hidden = grouped_matmul(xs, [w_gate, w_up], block_expert, n_used, tn=512, out_dtype=BF16, gated=True)
    y_slots = grouped_matmul(hidden, [w_down], block_expert, n_used, tn=2048, out_dtype=F32, gated=False)
    slot_of = slot_of.reshape(m, TOP_K)
    y0 = y_slots[slot_of[:, 0]]
    y1 = y_slots[slot_of[:, 1]]
    return moe_add_layer_norm(x, y0, y1, gates, ln_g, ln_b, alpha)


def _rope_tables(pos):
    half = ROPE_DIM // 2
    inv = ROPE_THETA ** (-jnp.arange(half, dtype=F32) / half)
    ang = pos.astype(F32)[:, None] * inv
    cos, sin = jnp.cos(ang), jnp.sin(ang)
    return jnp.concatenate([cos, cos], -1), jnp.concatenate([sin, sin], -1)


def _rotate_half_cols(w):
    half = ROPE_DIM // 2
    return jnp.concatenate([-w[..., half:], w[..., :half]], -1)


def even_layer(x, xb, dims, kv_cache, h0, w_in, sinks, ssm_p, w_out):
    n_batch, tp, padf, t_real, n_dec, t_new, past_len = dims
    m, d = x.shape
    mp = n_batch * tp
    ms = n_dec * t_new
    w_b = d // 2
    w_a = w_b
    kv_w = (w_in.shape[1] - w_a - 2 * w_b) // 2
    g_b = w_b // P_B
    z = matmul([(xb, 0)], w_in, kp=d, m_rows=m, tn=512, out_dtype=F32)

    attn_p = swa_prompt(z, sinks.astype(F32), n_batch=n_batch, tp=tp, padf=padf, w_a=w_a, kv_w=kv_w)
    attn_s, kv_s = swa_sample(z, kv_cache, sinks, row0=mp, n_dec=n_dec, t_new=t_new,
                              past_len=past_len, w_a=w_a, kv_w=kv_w)
    zp = z[:mp].reshape(n_batch, tp, -1)
    meta = slice(padf, padf + N_META)
    last = slice(tp - WINDOW, tp)
    kv_p = jnp.stack(
        [jnp.concatenate([zp[:, meta, w_a + o * kv_w:w_a + (o + 1) * kv_w],
                          zp[:, last, w_a + o * kv_w:w_a + (o + 1) * kv_w]], 1) for o in range(2)],
        axis=1).reshape(n_batch, 2, N_META + WINDOW, kv_w // DH_A, DH_A)
    kv_s = kv_s.reshape(n_dec, 2, N_META + WINDOW, kv_w // DH_A, DH_A)

    u0 = w_a + 2 * kv_w
    nc = t_real // SSM_CHUNK
    bp = -(-n_batch // SUBLANES) * SUBLANES
    u_p = zp[:, padf:, u0:u0 + w_b].reshape(n_batch, nc, SSM_CHUNK, g_b, P_B).transpose(3, 1, 0, 2, 4)
    u_p = jnp.pad(u_p, ((0, 0), (0, 0), (0, bp - n_batch), (0, 0), (0, 0))).reshape(g_b, nc * bp, SSM_CHUNK * P_B)
    y_p, h_p = ssm_prompt(u_p, ssm_matrices(*ssm_p, SSM_CHUNK), nc=nc, bp=bp)
    y_p = y_p.reshape(g_b, nc, bp, SSM_CHUNK, P_B)[:, :, :n_batch].transpose(2, 1, 3, 0, 4)
    y_p = jnp.pad(y_p.reshape(n_batch, t_real, w_b), ((0, 0), (padf, 0), (0, 0))).reshape(mp, w_b)
    h_p = h_p[:, :n_batch].reshape(g_b, n_batch, 2, N_B).transpose(1, 0, 3, 2)

    u_s = z[mp:, u0:u0 + w_b].reshape(n_dec, t_new, g_b, P_B).transpose(2, 0, 1, 3).reshape(g_b, n_dec, t_new * P_B)
    s0 = h0.astype(F32).transpose(1, 0, 3, 2).reshape(g_b, n_dec, 2 * N_B)
    y_s, h_s = ssm_sample(u_s, s0, ssm_matrices(*ssm_p, t_new))
    y_s = y_s.reshape(g_b, n_dec, t_new, P_B).transpose(1, 2, 0, 3).reshape(ms, w_b)
    h_s = h_s.reshape(g_b, n_dec, 2, N_B).transpose(1, 0, 3, 2)

    y_glu = glu(jnp.concatenate([y_p, y_s], 0), z, gate_col0=u0 + w_b)
    attn = jnp.concatenate([attn_p, attn_s], 0)
    mix = matmul([(attn, 0), (y_glu, 0)], w_out, kp=w_a, m_rows=m, tn=512, out_dtype=F32)
    return mix, kv_p, kv_s, h_p, h_s


def odd_layer(x, xb, dims, tables, pool_buf, cache, page_table, w_in, pool_w, pool_scale,
              q_norm, w_uq, kv_norm, w_uk, w_uv, w_out):
    n_batch, tp, padf, t_real, n_dec, t_new, past_len = dims
    cs_rows, cos_p, sin_p, cos_s, sin_s = tables
    m, d = x.shape
    mp = n_batch * tp
    ms = n_dec * t_new
    c_width = pool_scale.shape[0]
    q_lora = q_norm.shape[0]
    n_heads = w_uq.shape[1]
    kr0 = c_width + q_lora + KV_LORA
    w_in_x = jnp.concatenate([w_in, _rotate_half_cols(w_in[:, kr0:])], 1)
    z = matmul([(xb, 0)], w_in_x, kp=d, m_rows=m, tn=w_in_x.shape[1] // 3, out_dtype=F32)

    y_pool_p = pool_prompt(z, pool_w, pool_scale, n_batch=n_batch, tp=tp, padf=padf, c_width=c_width)
    buf16 = jnp.pad(pool_buf.astype(F32), ((0, 0), (1, 0), (0, 0)))
    y_pool_s = pool_sample(z, buf16, pool_w, pool_scale, row0=mp, n_dec=n_dec, t_new=t_new, c_width=c_width)
    zp = z[:mp].reshape(n_batch, tp, -1)
    new_pool_p = zp[:, tp - (POOL_MAX - 1):, :c_width]
    u_s = z[mp:, :c_width].reshape(n_dec, t_new, c_width)
    new_pool_s = jnp.concatenate([pool_buf.astype(F32), u_s], 1)[:, -(POOL_MAX - 1):]

    cqn, rows, ckvb, krb = mla_prep(z, cs_rows, q_norm, kv_norm, c_width=c_width, q_lora=q_lora)
    w_q = jnp.concatenate([w_uq[..., :NOPE].reshape(q_lora, n_heads * NOPE),
                           w_uq[..., NOPE:].reshape(q_lora, n_heads * ROPE_DIM),
                           _rotate_half_cols(w_uq[..., NOPE:]).reshape(q_lora, n_heads * ROPE_DIM)], 1)
    qfull = matmul([(cqn, 0)], w_q, kp=q_lora, m_rows=m, tn=1024, out_dtype=F32)
    w_kv = jnp.concatenate([w_uk.reshape(KV_LORA, n_heads * NOPE), w_uv.reshape(KV_LORA, n_heads * V_DIM)], 1)
    kv = matmul([(ckvb, 0)], w_kv, kp=KV_LORA, m_rows=mp, tn=1024, out_dtype=BF16)
    y_mla_p = mla_prompt(qfull, cos_p, sin_p, kv, krb, n_batch=n_batch, tp=tp, padf=padf, n_heads=n_heads)

    q_lat = head_proj(qfull, w_uk.transpose(1, 2, 0), row0=mp, m_rows=ms, k_head=NOPE, n_head=KV_LORA)
    nope_w = n_heads * NOPE
    rope_w = n_heads * ROPE_DIM
    q_r = qfull[mp:, nope_w:nope_w + rope_w].reshape(ms * n_heads, ROPE_DIM)
    q_s = qfull[mp:, nope_w + rope_w:].reshape(ms * n_heads, ROPE_DIM)
    o_lat = mla_sample(page_table, q_lat.reshape(ms * n_heads, KV_LORA), q_r, q_s, cos_s, sin_s, rows, cache,
                       row0=mp, n_dec=n_dec, t_new=t_new, n_heads=n_heads)
    y_mla_s = head_proj(o_lat.reshape(ms, n_heads * KV_LORA), w_uv.transpose(1, 0, 2), row0=0, m_rows=ms,
                        k_head=KV_LORA, n_head=V_DIM)

    y_pool = jnp.concatenate([y_pool_p, y_pool_s], 0)
    y_mla = jnp.concatenate([y_mla_p, y_mla_s], 0)
    pieces = [(y_pool, 0)] + [(y_mla, cb) for cb in range(y_mla.shape[1] // c_width)]
    mix = matmul(pieces, w_out, kp=c_width, m_rows=m, tn=512, out_dtype=F32)
    rows_p = rows[:mp].reshape(n_batch, tp, -1)[:, padf:]
    rows_s = rows[mp:].reshape(n_dec, t_new, -1)
    return mix, new_pool_p, new_pool_s, rows_p, rows_s


def kernel(x_prompt, x_sample, cache_swa_kv, state_ssm, state_pool, cache_mla, page_table, meta_tokens, w_in_ab, attn_sinks, ssm_lambda_re, ssm_lambda_im, ssm_log_dt, ssm_b_re, ssm_b_im, ssm_c_re, ssm_c_im, ssm_d, w_out_ab, w_in_cd, pool_w, pool_scale, mla_q_norm, mla_w_uq, mla_kv_norm, mla_w_uk, mla_w_uv, w_out_cd, ln_mix_g, ln_mix_b, ln_ffn_g, ln_ffn_b, router_group_w, router_group_b, router_expert_w, router_expert_b, expert_w_gate, expert_w_up, expert_w_down):
    n_batch, seq, d = x_prompt.shape
    n_dec, t_new, _ = x_sample.shape
    depth = ln_mix_g.shape[0]
    past_len = page_table.shape[1] * PAGE_SIZE
    alpha = (2 * depth) ** 0.25
    t_real = N_META + seq
    padf = (-t_real) % BLOCK
    tp = t_real + padf
    assert t_real % SSM_CHUNK == 0 and padf % N_META == 0
    mp = n_batch * tp
    ms = n_dec * t_new
    dims = (n_batch, tp, padf, t_real, n_dec, t_new, past_len)

    meta = jnp.broadcast_to(meta_tokens.astype(F32)[None], (n_batch, N_META, d))
    xp = jnp.concatenate([jnp.zeros((n_batch, padf, d), F32), meta, x_prompt.astype(F32)], 1)
    x = jnp.concatenate([xp.reshape(mp, d), x_sample.astype(F32).reshape(ms, d)], 0)
    xb = x.astype(BF16)
    pos_p = jnp.arange(tp, dtype=jnp.int32) - padf
    valid = jnp.concatenate([jnp.tile(pos_p >= 0, n_batch), jnp.ones((ms,), bool)])

    pos_s = past_len + jnp.arange(t_new, dtype=jnp.int32)
    cos_p1, sin_p1 = _rope_tables(pos_p)
    cos_s1, sin_s1 = _rope_tables(pos_s)
    cs_rows = jnp.concatenate([jnp.tile(jnp.concatenate([cos_p1, sin_p1], 1), (n_batch, 1)),
                               jnp.tile(jnp.concatenate([cos_s1, sin_s1], 1), (n_dec, 1))], 0)
    n_heads_d = mla_w_uq.shape[2]
    tables = (cs_rows, jnp.tile(cos_p1, (1, 2)), jnp.tile(sin_p1, (1, 2)),
              jnp.repeat(cos_s1, n_heads_d, axis=0), jnp.repeat(sin_s1, n_heads_d, axis=0))

    swa_p, swa_s, ssm_p, ssm_s, pool_p, pool_s, mla_p, mla_s = [], [], [], [], [], [], [], []
    for l in range(depth):
        i = l // 2
        if l % 2 == 0:
            ssm_params = (ssm_lambda_re[i], ssm_lambda_im[i], ssm_log_dt[i], ssm_b_re[i], ssm_b_im[i],
                          ssm_c_re[i], ssm_c_im[i], ssm_d[i])
            mix, kv_p, kv_s, h_p, h_s = even_layer(x, xb, dims, cache_swa_kv[i], state_ssm[i], w_in_ab[i],
                                                   attn_sinks[i], ssm_params, w_out_ab[i])
            swa_p.append(kv_p)
            swa_s.append(kv_s)
            ssm_p.append(h_p)
            ssm_s.append(h_s)
        else:
            mix, np_p, np_s, rows_p, rows_s = odd_layer(
                x, xb, dims, tables, state_pool[i], cache_mla[i], page_table, w_in_cd[i], pool_w[i],
                pool_scale[i], mla_q_norm[i], mla_w_uq[i], mla_kv_norm[i], mla_w_uk[i], mla_w_uv[i],
                w_out_cd[i])
            pool_p.append(np_p)
            pool_s.append(np_s)
            mla_p.append(rows_p)
            mla_s.append(rows_s)
        x, xb = add_layer_norm(x, mix, ln_mix_g[l], ln_mix_b[l], alpha)
        x, xb = moe_layer(x, xb, valid, router_group_w[l], router_group_b[l], router_expert_w[l],
                          router_expert_b[l], expert_w_gate[l], expert_w_up[l], expert_w_down[l],
                          ln_ffn_g[l], ln_ffn_b[l], alpha)
    y_p = x[:mp].reshape(n_batch, tp, d)[:, padf + N_META:]
    y_s = x[mp:].reshape(n_dec, t_new, d)
    return (y_p, y_s, jnp.stack(swa_p), jnp.stack(swa_s), jnp.stack(ssm_p), jnp.stack(ssm_s),
            jnp.stack(pool_p), jnp.stack(pool_s), jnp.stack(mla_p), jnp.stack(mla_s))
```

```python
import functools
import math

import jax
import jax.numpy as jnp
from jax import lax
from jax.experimental import pallas as pl
from jax.experimental.pallas import tpu as pltpu

F32 = jnp.float32
BF16 = jnp.bfloat16

N_META = 16
BLOCK = 128
SUBLANES = 8
DH_A = 64
WINDOW = 128
P_B = 16
N_B = 64
POOL_WINDOWS = (2, 4, 8, 16)
POOL_MAX = 16
KV_LORA = 512
NOPE = 128
ROPE_DIM = 64
V_DIM = 128
ROPE_THETA = 10000.0
MLA_SCALE = (NOPE + ROPE_DIM) ** -0.5
PAGE_SIZE = 128
N_GROUPS_E = 4
EPG = 4
N_EXPERTS = N_GROUPS_E * EPG
TOP_K = 2
NEG = -1e30
EPS = 1e-5
SSM_CHUNK = 16
MOE_BLOCK = 256
PAGES_PER_STEP = 32
MLA_SAMPLE_CHAINS = 1
VMEM_LIMIT = 56 * 1024 * 1024


def _params(n_axes):
    return pltpu.CompilerParams(dimension_semantics=("arbitrary",) * n_axes,
                                vmem_limit_bytes=VMEM_LIMIT)


def _pick(n, cands):
    for c in cands:
        if n % c == 0:
            return c
    raise ValueError(f"no tile in {cands} divides {n}")


def _dot(a, b):
    return jnp.dot(a, b, preferred_element_type=F32)


def _dot_nt(a, b):
    return lax.dot_general(a, b, (((1,), (1,)), ((), ())), preferred_element_type=F32)


def _split(a):
    hi = a.astype(BF16)
    lo = (a - hi.astype(F32)).astype(BF16)
    return hi, lo


def _dot3(a, b):
    ah, al = _split(a)
    bh, bl = _split(b)
    return _dot(ah, bh) + (_dot(ah, bl) + _dot(al, bh))


def _mm_kernel(*refs, n_pieces):
    a_refs = refs[:n_pieces]
    w_refs = refs[n_pieces:2 * n_pieces]
    o_ref = refs[2 * n_pieces]
    wb_refs = refs[2 * n_pieces + 1:]

    @pl.when(pl.program_id(1) == 0)
    def _():
        for w_ref, wb_ref in zip(w_refs, wb_refs):
            wb_ref[...] = w_ref[...].astype(BF16)

    acc = _dot(a_refs[0][...].astype(BF16), wb_refs[0][...])
    for a_ref, wb_ref in zip(a_refs[1:], wb_refs[1:]):
        acc = acc + _dot(a_ref[...].astype(BF16), wb_ref[...])
    o_ref[...] = acc.astype(o_ref.dtype)


def matmul(pieces, w, *, kp, m_rows, tn, out_dtype, layer=None):
    n = w.shape[-1]
    tm = _pick(m_rows, (512, 256, 128))
    assert n % tn == 0 and (w.ndim == 3) == (layer is not None)
    n_p = len(pieces)
    in_specs = []
    for _, cb in pieces:
        in_specs.append(pl.BlockSpec((tm, kp), lambda j, i, cb=cb: (i, cb)))
    for p in range(n_p):
        if layer is None:
            in_specs.append(pl.BlockSpec((kp, tn), lambda j, i, p=p: (p, j)))
        else:
            in_specs.append(pl.BlockSpec((None, kp, tn), lambda j, i, p=p: (layer, p, j)))
    return pl.pallas_call(
        functools.partial(_mm_kernel, n_pieces=n_p),
        grid=(n // tn, m_rows // tm),
        in_specs=in_specs,
        out_specs=pl.BlockSpec((tm, tn), lambda j, i: (i, j)),
        out_shape=jax.ShapeDtypeStruct((m_rows, n), out_dtype),
        scratch_shapes=[pltpu.VMEM((kp, tn), BF16) for _ in range(n_p)],
        compiler_params=_params(2),
        name="matmul",
    )(*[a for a, _ in pieces], *([w] * n_p))


def _ln_core(h, g_ref, b_ref, of_ref, ob_ref):
    mu = jnp.mean(h, -1, keepdims=True)
    hc = h - mu
    var = jnp.mean(hc * hc, -1, keepdims=True)
    y = hc * lax.rsqrt(var + EPS) * g_ref[...] + b_ref[...]
    of_ref[...] = y
    ob_ref[...] = y.astype(BF16)


def _ln_kernel(x_ref, m_ref, g_ref, b_ref, of_ref, ob_ref, *, alpha):
    _ln_core(alpha * x_ref[...] + m_ref[...].astype(F32), g_ref, b_ref, of_ref, ob_ref)


def _ln_moe_kernel(x_ref, y_ref, gt_ref, g_ref, b_ref, of_ref, ob_ref, *, alpha):
    gt = gt_ref[...]
    d = x_ref.shape[1]
    moe = gt[:, 0:1] * y_ref[:, :d].astype(F32) + gt[:, 1:2] * y_ref[:, d:].astype(F32)
    _ln_core(alpha * x_ref[...] + moe, g_ref, b_ref, of_ref, ob_ref)


def _ln_call(kern, row_inputs, small_inputs, m, d):
    tm = _pick(m, (256, 128))
    row_spec = pl.BlockSpec((tm, d), lambda i: (i, 0))
    in_specs = [pl.BlockSpec((tm, a.shape[1]), lambda i: (i, 0)) for a in row_inputs]
    in_specs += [pl.BlockSpec(a.shape, lambda i: (0, 0)) for a in small_inputs]
    return pl.pallas_call(
        kern, grid=(m // tm,), in_specs=in_specs,
        out_specs=[row_spec, row_spec],
        out_shape=[jax.ShapeDtypeStruct((m, d), F32), jax.ShapeDtypeStruct((m, d), BF16)],
        compiler_params=_params(1), name="add_layer_norm",
    )(*row_inputs, *small_inputs)


def add_layer_norm(x, mix, g, b, alpha):
    m, d = x.shape
    return _ln_call(functools.partial(_ln_kernel, alpha=alpha), [x, mix],
                    [g.reshape(1, d), b.reshape(1, d)], m, d)


def moe_add_layer_norm(x, y01, gates, g, b, alpha):
    m, d = x.shape
    return _ln_call(functools.partial(_ln_moe_kernel, alpha=alpha), [x, y01, gates],
                    [g.reshape(1, d), b.reshape(1, d)], m, d)


def _alibi_slope(h, n_heads):
    return 2.0 ** (-8.0 * (h + 1) / n_heads)


def _swa_prompt_kernel(sink_ref, q_ref, kc_ref, vc_ref, kp_ref, vp_ref, km_ref, vm_ref, o_ref,
                       *, padf, n_heads, kvh):
    j = pl.program_id(1)
    base = j * BLOCK - padf
    r = lax.broadcasted_iota(jnp.int32, (BLOCK, 2 * BLOCK), 0)
    c = lax.broadcasted_iota(jnp.int32, (BLOCK, 2 * BLOCK), 1)
    dist = r + BLOCK - c
    vis_band = (dist >= 0) & (dist <= WINDOW) & (base - BLOCK + c >= N_META)
    dist_f = dist.astype(F32)
    rm = lax.broadcasted_iota(jnp.int32, (BLOCK, N_META), 0)
    cm = lax.broadcasted_iota(jnp.int32, (BLOCK, N_META), 1)
    dist_m = base + rm - cm
    vis_meta = dist_m >= 0
    dist_mf = jnp.abs(dist_m).astype(F32)
    group = n_heads // kvh
    scale = DH_A ** -0.5
    for h in range(kvh):
        sl = slice(h * DH_A, (h + 1) * DH_A)
        k_band = jnp.concatenate([kp_ref[:, sl], kc_ref[:, sl]], 0).astype(BF16)
        v_band = jnp.concatenate([vp_ref[:, sl], vc_ref[:, sl]], 0).astype(BF16)
        k_meta = km_ref[:, sl].astype(BF16)
        v_meta = vm_ref[:, sl].astype(BF16)
        for g in range(group):
            hh = h * group + g
            hs = slice(hh * DH_A, (hh + 1) * DH_A)
            slope = _alibi_slope(hh, n_heads)
            sink = sink_ref[hh]
            q = q_ref[:, hs].astype(BF16)
            lb = jnp.where(vis_band, _dot_nt(q, k_band) * scale - slope * dist_f, NEG)
            lm = jnp.where(vis_meta, _dot_nt(q, k_meta) * scale - slope * dist_mf, NEG)
            mx = jnp.maximum(jnp.maximum(jnp.max(lb, -1, keepdims=True),
                                         jnp.max(lm, -1, keepdims=True)), sink)
            eb = jnp.exp(lb - mx)
            em = jnp.exp(lm - mx)
            den = (jnp.sum(eb, -1, keepdims=True) + jnp.sum(em, -1, keepdims=True)
                   + jnp.exp(sink - mx))
            inv = 1.0 / den
            o = _dot((eb * inv).astype(BF16), v_band) + _dot((em * inv).astype(BF16), v_meta)
            o_ref[:, hs] = o.astype(o_ref.dtype)


def swa_prompt(z, sinks, *, n_batch, tp, padf, w_a, kv_w):
    nb = tp // BLOCK
    kcol = w_a // kv_w
    meta_blk = padf // N_META
    rows_meta = tp // N_META

    def cur(col):
        return pl.BlockSpec((BLOCK, kv_w), lambda b, j: (b * nb + j, col))

    def prev(col):
        return pl.BlockSpec((BLOCK, kv_w), lambda b, j: (jnp.maximum(b * nb + j - 1, 0), col))

    def meta(col):
        return pl.BlockSpec((N_META, kv_w), lambda b, j: (b * rows_meta + meta_blk, col))

    n_heads = w_a // DH_A
    return pl.pallas_call(
        functools.partial(_swa_prompt_kernel, padf=padf, n_heads=n_heads, kvh=kv_w // DH_A),
        grid=(n_batch, nb),
        in_specs=[pl.BlockSpec(memory_space=pltpu.SMEM),
                  pl.BlockSpec((BLOCK, w_a), lambda b, j: (b * nb + j, 0)),
                  cur(kcol), cur(kcol + 1), prev(kcol), prev(kcol + 1), meta(kcol), meta(kcol + 1)],
        out_specs=pl.BlockSpec((BLOCK, w_a), lambda b, j: (b * nb + j, 0)),
        out_shape=jax.ShapeDtypeStruct((n_batch * tp, w_a), BF16),
        compiler_params=_params(2), name="swa_prompt",
    )(sinks, z, z, z, z, z, z, z)


def _swa_sample_kernel(sink_ref, slope_ref, q_ref, kn_ref, vn_ref, cache_ref, o_ref, kv_ref,
                       *, bb, t_new, past_len, n_heads, kvh):
    group = n_heads // kvh
    win = WINDOW
    n_keys = 2 * BLOCK
    n_fill = n_keys - (N_META + win + t_new)
    rows = group * t_new
    r = lax.broadcasted_iota(jnp.int32, (rows, n_keys), 0)
    c = lax.broadcasted_iota(jnp.int32, (rows, n_keys), 1)
    q_pos = past_len + r % t_new
    k_pos = jnp.where(c < N_META, c,
                      jnp.where(c < N_META + win, past_len - win + (c - N_META),
                                past_len + (c - N_META - win)))
    dist = q_pos - k_pos
    visible = (dist >= 0) & ((c < N_META) | ((k_pos >= N_META) & (dist <= WINDOW)))
    fill = jnp.zeros((n_fill, DH_A), F32)
    dist_f = jnp.abs(dist).astype(F32)
    scale = DH_A ** -0.5
    for b in range(bb):
        rs = slice(b * t_new, (b + 1) * t_new)
        for kv in range(2):
            new_ref = kn_ref if kv == 0 else vn_ref
            kv_ref[b, kv, 0:N_META, :] = cache_ref[b, kv, 0:N_META, :]
            kv_ref[b, kv, N_META:N_META + win - t_new, :] = cache_ref[b, kv, N_META + t_new:N_META + win, :]
            kv_ref[b, kv, N_META + win - t_new:N_META + win, :] = new_ref[rs, :]
        for h in range(kvh):
            sl = slice(h * DH_A, (h + 1) * DH_A)
            k_all = jnp.concatenate([cache_ref[b, 0, :, sl], kn_ref[rs, sl], fill], 0).astype(BF16)
            v_all = jnp.concatenate([cache_ref[b, 1, :, sl], vn_ref[rs, sl], fill], 0).astype(BF16)
            q = jnp.concatenate(
                [q_ref[rs, (h * group + g) * DH_A:(h * group + g + 1) * DH_A] for g in range(group)],
                0).astype(BF16)
            sink = sink_ref[h]
            logits = jnp.where(visible, _dot_nt(q, k_all) * scale - slope_ref[h] * dist_f, NEG)
            mx = jnp.maximum(jnp.max(logits, -1, keepdims=True), sink)
            e = jnp.exp(logits - mx)
            den = jnp.sum(e, -1, keepdims=True) + jnp.exp(sink - mx)
            o = _dot((e * (1.0 / den)).astype(BF16), v_all)
            for g in range(group):
                hh = h * group + g
                o_ref[rs, hh * DH_A:(hh + 1) * DH_A] = o[g * t_new:(g + 1) * t_new].astype(o_ref.dtype)


def swa_sample(z, cache_all, layer, sinks, *, row0, n_dec, t_new, past_len, w_a, kv_w):
    bb = _pick(n_dec, (8, 4, 2, 1))
    n_heads = w_a // DH_A
    kvh = kv_w // DH_A
    group = n_heads // kvh
    rb0 = row0 // (bb * t_new)
    assert row0 % (bb * t_new) == 0 and t_new == SUBLANES and cache_all.shape[3] == N_META + WINDOW
    kcol = w_a // kv_w
    sink_rows = jnp.repeat(sinks.astype(F32).reshape(kvh, group), t_new, axis=1)[..., None]
    slopes = jnp.asarray([_alibi_slope(h, n_heads) for h in range(n_heads)], F32)
    slope_rows = jnp.repeat(slopes.reshape(kvh, group), t_new, axis=1)[..., None]
    cache5 = cache_all.reshape(cache_all.shape[0], n_dec, 2, N_META + WINDOW, kv_w)
    small = pl.BlockSpec((kvh, group * t_new, 1), lambda i: (0, 0, 0))
    cache_spec = pl.BlockSpec((bb, 2, N_META + WINDOW, kv_w), lambda i: (i, 0, 0, 0))
    cache_in_spec = pl.BlockSpec((None, bb, 2, N_META + WINDOW, kv_w), lambda i: (layer, i, 0, 0, 0))
    return pl.pallas_call(
        functools.partial(_swa_sample_kernel, bb=bb, t_new=t_new, past_len=past_len,
                          n_heads=n_heads, kvh=kvh),
        grid=(n_dec // bb,),
        in_specs=[small, small,
                  pl.BlockSpec((bb * t_new, w_a), lambda i: (rb0 + i, 0)),
                  pl.BlockSpec((bb * t_new, kv_w), lambda i: (rb0 + i, kcol)),
                  pl.BlockSpec((bb * t_new, kv_w), lambda i: (rb0 + i, kcol + 1)),
                  cache_in_spec],
        out_specs=[pl.BlockSpec((bb * t_new, w_a), lambda i: (i, 0)), cache_spec],
        out_shape=[jax.ShapeDtypeStruct((n_dec * t_new, w_a), BF16),
                   jax.ShapeDtypeStruct(cache5.shape[1:], F32)],
        compiler_params=_params(1), name="swa_sample",
    )(sink_rows, slope_rows, z, z, z, cache5)


def ssm_matrices(lam_re, lam_im, log_dt, b_re, b_im, c_re, c_im, d_skip, length):
    hi = lax.Precision.HIGHEST
    g_n, n_n = lam_re.shape
    lam = lax.complex(lam_re.astype(F32), lam_im.astype(F32))
    lam_dt = lam * jnp.exp(log_dt.astype(F32))[:, None]
    lam_bar = jnp.exp(lam_dt)
    b_bar = ((lam_bar - 1.0) / lam)[..., None] * lax.complex(b_re.astype(F32), b_im.astype(F32))
    cc = lax.complex(c_re.astype(F32), c_im.astype(F32))
    steps = jnp.arange(length + 1, dtype=F32)
    pw = jnp.exp(lam_dt[None] * steps[:, None, None])
    cb = cc[None] * pw[:length, :, None, :]
    k_lag = (jnp.einsum('jgpn,gnq->jgpq', cb.real, b_bar.real, precision=hi)
             - jnp.einsum('jgpn,gnq->jgpq', cb.imag, b_bar.imag, precision=hi))
    t = jnp.arange(length)
    lag = t[None, :] - t[:, None]
    k_st = jnp.where((lag >= 0)[:, :, None, None, None], k_lag[jnp.clip(lag, 0)], 0.0)
    mt = k_st.transpose(2, 0, 4, 1, 3)
    eye_t = jnp.eye(length, dtype=F32)
    eye_p = jnp.eye(P_B, dtype=F32)
    mt = mt + (eye_t[None, :, None, :, None] * eye_p[None, None, :, None, :]
               * d_skip.astype(F32).reshape(g_n, P_B)[:, None, None, None, :])
    mt = mt.reshape(g_n, length * P_B, length * P_B)
    sin_c = (pw[length - 1 - t][:, :, :, None] * b_bar[None]).transpose(1, 0, 3, 2)
    sin_re = sin_c.real.reshape(g_n, length * P_B, n_n)
    sin_im = sin_c.imag.reshape(g_n, length * P_B, n_n)
    g_c = (cc[None] * pw[1:length + 1][:, :, None, :]).transpose(1, 3, 0, 2)
    sout_re = g_c.real.reshape(g_n, n_n, length * P_B)
    sout_im = (-g_c.imag).reshape(g_n, n_n, length * P_B)
    return (mt, sin_re, sin_im, sout_re, sout_im), pw


def ssm_sub_matrices(mats, pw, steps):
    mt, sin_re, sin_im, sout_re, sout_im = mats
    w = steps * P_B
    skip = mt.shape[1] - w
    sin_t = jnp.concatenate([sin_re[:, skip:], sin_im[:, skip:]], -1)
    sout_t = jnp.concatenate([sout_re[:, :, :w], sout_im[:, :, :w]], 1)
    lam_l = pw[steps]
    lam_a = jnp.concatenate([lam_l.real, lam_l.real], -1)[:, None, :]
    lam_b = jnp.concatenate([-lam_l.imag, lam_l.imag], -1)[:, None, :]
    return mt[:, :w, :w], sin_t, sout_t, lam_a, lam_b


def _ssm_prompt_kernel(u_ref, mt_ref, sre_ref, sim_ref, ore_ref, oim_ref, lre_ref, lim_ref, y_ref, h_ref,
                       er_s, ei_s, pr_s, pi_s, *, gs, nc, bp, unroll):
    for g in range(gs):
        u = u_ref[g]
        er_s[g] = _dot3(u, sre_ref[g]).reshape(nc, bp, N_B)
        ei_s[g] = _dot3(u, sim_ref[g]).reshape(nc, bp, N_B)
    lam = [(lre_ref[g], lim_ref[g]) for g in range(gs)]

    def body(c, carry):
        nxt = []
        for g in range(gs):
            sr, si = carry[g]
            lr, li = lam[g]
            pr_s[g, c] = sr
            pi_s[g, c] = si
            nxt.append((lr * sr - li * si + er_s[g, c], lr * si + li * sr + ei_s[g, c]))
        return tuple(nxt)

    zero = jnp.zeros((bp, N_B), F32)
    last = lax.fori_loop(0, nc, body, tuple((zero, zero) for _ in range(gs)), unroll=unroll)
    for g in range(gs):
        h_ref[g] = jnp.concatenate(last[g], 1)
        y_ref[g] = (_dot3(u_ref[g], mt_ref[g])
                    + _dot3(pr_s[g].reshape(nc * bp, N_B), ore_ref[g])
                    + _dot3(pi_s[g].reshape(nc * bp, N_B), oim_ref[g]))


def ssm_prompt(u, mats, lam_l, *, nc, bp):
    mt, sin_re, sin_im, sout_re, sout_im = mats
    g_n = u.shape[0]
    gs = _pick(g_n, (4, 2, 1))
    lp = SSM_CHUNK * P_B

    def per_group(shape):
        return pl.BlockSpec((gs,) + shape, lambda g: (g, 0, 0))

    return pl.pallas_call(
        functools.partial(_ssm_prompt_kernel, gs=gs, nc=nc, bp=bp, unroll=3 if nc % 3 == 0 else 1),
        grid=(g_n // gs,),
        in_specs=[per_group((nc * bp, lp)), per_group((lp, lp)), per_group((lp, N_B)), per_group((lp, N_B)),
                  per_group((N_B, lp)), per_group((N_B, lp)), per_group((1, N_B)), per_group((1, N_B))],
        out_specs=[per_group((nc * bp, lp)), per_group((bp, 2 * N_B))],
        out_shape=[jax.ShapeDtypeStruct((g_n, nc * bp, lp), F32),
                   jax.ShapeDtypeStruct((g_n, bp, 2 * N_B), F32)],
        scratch_shapes=[pltpu.VMEM((gs, nc, bp, N_B), F32) for _ in range(4)],
        compiler_params=_params(1), name="ssm_prompt",
    )(u, mt, sin_re, sin_im, sout_re, sout_im, lam_l.real[:, None, :], lam_l.imag[:, None, :])


def _ssm_sample_kernel(u_ref, s0_ref, mt_ref, sin_ref, sout_ref, la_ref, lb_ref, y_ref, h_ref):
    u = u_ref[...]
    s0 = s0_ref[...]
    y_ref[...] = _dot3(u, mt_ref[...]) + _dot3(s0, sout_ref[...])
    h_ref[...] = (la_ref[...] * s0 + lb_ref[...] * pltpu.roll(s0, N_B, 1)
                  + _dot3(u, sin_ref[...]))


def ssm_sample(u, s0, mats):
    mt, sin_t, sout_t, lam_a, lam_b = mats
    g_n, n_dec, lp = u.shape
    n2 = 2 * N_B

    def per_group(shape):
        return pl.BlockSpec((None,) + shape, lambda g: (g, 0, 0))

    return pl.pallas_call(
        _ssm_sample_kernel, grid=(g_n,),
        in_specs=[per_group((n_dec, lp)), per_group((n_dec, n2)), per_group((lp, lp)),
                  per_group((lp, n2)), per_group((n2, lp)), per_group((1, n2)), per_group((1, n2))],
        out_specs=[per_group((n_dec, lp)), per_group((n_dec, n2))],
        out_shape=[jax.ShapeDtypeStruct((g_n, n_dec, lp), F32),
                   jax.ShapeDtypeStruct((g_n, n_dec, n2), F32)],
        compiler_params=_params(1), name="ssm_sample",
    )(u, s0, mt, sin_t, sout_t, lam_a, lam_b)


def _glu_kernel(y_ref, gate_ref, o_ref):
    y = y_ref[...]
    gelu = 0.5 * y * (1.0 + jnp.tanh(math.sqrt(2.0 / math.pi) * (y + 0.044715 * (y * y * y))))
    o_ref[...] = (gelu * jax.nn.sigmoid(gate_ref[...])).astype(o_ref.dtype)


def glu(y, z, *, gate_col0):
    m, w = y.shape
    tm = _pick(m, (512, 256, 128))
    tc = 512
    assert gate_col0 % tc == 0 and w % tc == 0
    return pl.pallas_call(
        _glu_kernel, grid=(m // tm, w // tc),
        in_specs=[pl.BlockSpec((tm, tc), lambda i, c: (i, c)),
                  pl.BlockSpec((tm, tc), lambda i, c: (i, gate_col0 // tc + c))],
        out_specs=pl.BlockSpec((tm, tc), lambda i, c: (i, c)),
        out_shape=jax.ShapeDtypeStruct((m, w), BF16),
        compiler_params=_params(2), name="glu",
    )(y, z)


def _window_sums(ext, width):
    n = ext.shape[0]
    acc = ext
    k = 1
    while k < width:
        acc = acc[:n - 2 * k + 1] + acc[k:n - k + 1]
        k *= 2
    first = POOL_MAX - (width - 1)
    return acc[first:first + n - POOL_MAX]


def _pool_mix(ext, cur, count_fn, pw_ref, sc_ref, o_ref):
    cg = cur.shape[1] // len(POOL_WINDOWS)
    for g, width in enumerate(POOL_WINDOWS):
        sl = slice(g * cg, (g + 1) * cg)
        pooled = _window_sums(ext[:, sl], width) / count_fn(width) - cur[:, sl]
        mixed = _dot(pooled.astype(BF16), pw_ref[g].astype(BF16))
        o_ref[:, sl] = (mixed * sc_ref[:, sl]).astype(o_ref.dtype)


def _pool_prompt_kernel(cur_ref, prev_ref, pw_ref, sc_ref, o_ref, *, padf):
    j = pl.program_id(1)
    base = j * BLOCK - padf
    pos = base + lax.broadcasted_iota(jnp.int32, (BLOCK, 1), 0)
    pos_prev = base - POOL_MAX + lax.broadcasted_iota(jnp.int32, (POOL_MAX, 1), 0)
    cur = jnp.where(pos >= 0, cur_ref[...], 0.0)
    prev = jnp.where(pos_prev >= 0, prev_ref[...], 0.0)
    ext = jnp.concatenate([prev, cur], 0)

    def count(width):
        return jnp.maximum(jnp.minimum(pos + 1, width), 1).astype(F32)

    _pool_mix(ext, cur, count, pw_ref, sc_ref, o_ref)


def pool_prompt(z, pool_w, pool_scale, *, n_batch, tp, padf, c_width):
    nb = tp // BLOCK
    per = BLOCK // POOL_MAX
    return pl.pallas_call(
        functools.partial(_pool_prompt_kernel, padf=padf),
        grid=(n_batch, nb),
        in_specs=[pl.BlockSpec((BLOCK, c_width), lambda b, j: (b * nb + j, 0)),
                  pl.BlockSpec((POOL_MAX, c_width),
                               lambda b, j: (jnp.maximum((b * nb + j) * per - 1, 0), 0)),
                  pl.BlockSpec(pool_w.shape, lambda b, j: (0, 0, 0)),
                  pl.BlockSpec((1, c_width), lambda b, j: (0, 0))],
        out_specs=pl.BlockSpec((BLOCK, c_width), lambda b, j: (b * nb + j, 0)),
        out_shape=jax.ShapeDtypeStruct((n_batch * tp, c_width), BF16),
        compiler_params=_params(2), name="pool_prompt",
    )(z, z, pool_w, pool_scale.reshape(1, c_width))


def _pool_sample_kernel(cur_ref, buf_ref, pw_ref, sc_ref, o_ref, *, bb, t_new):
    cur = cur_ref[...]
    cg = cur.shape[1] // len(POOL_WINDOWS)
    for g, width in enumerate(POOL_WINDOWS):
        sl = slice(g * cg, (g + 1) * cg)
        sums = [_window_sums(jnp.concatenate([buf_ref[b, :, sl], cur[b * t_new:(b + 1) * t_new, sl]], 0),
                             width) for b in range(bb)]
        pooled = jnp.concatenate(sums, 0) / float(width) - cur[:, sl]
        mixed = _dot(pooled.astype(BF16), pw_ref[g].astype(BF16))
        o_ref[:, sl] = (mixed * sc_ref[:, sl]).astype(o_ref.dtype)


def pool_sample(z, buf16, pool_w, pool_scale, *, row0, n_dec, t_new, c_width):
    bb = _pick(n_dec, (8, 4, 2, 1))
    rb0 = row0 // (bb * t_new)
    assert row0 % (bb * t_new) == 0
    return pl.pallas_call(
        functools.partial(_pool_sample_kernel, bb=bb, t_new=t_new),
        grid=(n_dec // bb,),
        in_specs=[pl.BlockSpec((bb * t_new, c_width), lambda i: (rb0 + i, 0)),
                  pl.BlockSpec((bb, POOL_MAX, c_width), lambda i: (i, 0, 0)),
                  pl.BlockSpec(pool_w.shape, lambda i: (0, 0, 0)),
                  pl.BlockSpec((1, c_width), lambda i: (0, 0))],
        out_specs=pl.BlockSpec((bb * t_new, c_width), lambda i: (i, 0)),
        out_shape=jax.ShapeDtypeStruct((n_dec * t_new, c_width), BF16),
        compiler_params=_params(1), name="pool_sample",
    )(z, buf16, pool_w, pool_scale.reshape(1, c_width))


def _rms(x, g):
    return x * lax.rsqrt(jnp.mean(x * x, -1, keepdims=True) + EPS) * g


def _mla_prep_kernel(cq_ref, ckv_ref, kr_ref, cs_ref, qg_ref, kg_ref, cqn_ref, rows_ref, ckvb_ref, krb_ref):
    cqn_ref[...] = _rms(cq_ref[...], qg_ref[...]).astype(BF16)
    ckvn = _rms(ckv_ref[...], kg_ref[...])
    kr2 = kr_ref[...]
    cs = cs_ref[...]
    krr = kr2[:, :ROPE_DIM] * cs[:, :ROPE_DIM] + kr2[:, ROPE_DIM:] * cs[:, ROPE_DIM:]
    rows_ref[:, :KV_LORA] = ckvn
    rows_ref[:, KV_LORA:] = krr
    ckvb_ref[...] = ckvn.astype(BF16)
    krb_ref[...] = krr.astype(BF16)


def mla_prep(z, cs_rows, q_norm, kv_norm, *, c_width, q_lora):
    m = z.shape[0]
    tm = _pick(m, (256, 128))
    assert c_width % q_lora == 0 and (c_width + q_lora) % KV_LORA == 0
    kr_col = (c_width + q_lora + KV_LORA) // (2 * ROPE_DIM)
    row = lambda w, cb: pl.BlockSpec((tm, w), lambda i: (i, cb))
    return pl.pallas_call(
        _mla_prep_kernel, grid=(m // tm,),
        in_specs=[row(q_lora, c_width // q_lora), row(KV_LORA, (c_width + q_lora) // KV_LORA),
                  row(2 * ROPE_DIM, kr_col), row(2 * ROPE_DIM, 0),
                  pl.BlockSpec((1, q_lora), lambda i: (0, 0)), pl.BlockSpec((1, KV_LORA), lambda i: (0, 0))],
        out_specs=[row(q_lora, 0), row(KV_LORA + ROPE_DIM, 0), row(KV_LORA, 0), row(ROPE_DIM, 0)],
        out_shape=[jax.ShapeDtypeStruct((m, q_lora), BF16),
                   jax.ShapeDtypeStruct((m, KV_LORA + ROPE_DIM), F32),
                   jax.ShapeDtypeStruct((m, KV_LORA), BF16),
                   jax.ShapeDtypeStruct((m, ROPE_DIM), BF16)],
        compiler_params=_params(1), name="mla_prep",
    )(z, z, z, cs_rows, q_norm.reshape(1, q_lora), kv_norm.reshape(1, KV_LORA))


def _mla_prompt_kernel(qn_ref, qr_ref, qs_ref, cos_ref, sin_ref, kn_ref, v_ref, kr_ref, o_ref,
                       s_s, p_s, *, padf, heads, qb, exts):
    jq = pl.program_id(2)
    q_rope = (qr_ref[...] * cos_ref[...] + qs_ref[...] * sin_ref[...]).astype(BF16)
    q_nope = qn_ref[...].astype(BF16)
    rs_max = qb // 4
    rs_sum = qb // 2
    exp2_scale = MLA_SCALE * math.log2(math.e)

    def block(jv, ext):
        kr = kr_ref[0:ext, :]
        for h in range(heads):
            s_s[h, :, 0:ext] = (_dot_nt(q_nope[:, h * NOPE:(h + 1) * NOPE], kn_ref[0:ext, h * NOPE:(h + 1) * NOPE])
                                + _dot_nt(q_rope[:, h * ROPE_DIM:(h + 1) * ROPE_DIM], kr))
        for h in range(heads):
            m_rows = []
            for r0 in range(0, qb, rs_max):
                rows = slice(r0, r0 + rs_max)
                r = r0 + lax.broadcasted_iota(jnp.int32, (rs_max, BLOCK), 0)
                c = lax.broadcasted_iota(jnp.int32, (rs_max, BLOCK), 1)
                m_run = None
                for c0 in range(0, ext, BLOCK):
                    cols = slice(c0, c0 + BLOCK)
                    sc = s_s[h, rows, cols]
                    if c0 < padf or c0 + BLOCK - 1 > jv * qb + r0:
                        sc = jnp.where((c0 + c >= padf) & (c0 + c <= jv * qb + r), sc, NEG)
                        s_s[h, rows, cols] = sc
                    m_run = sc if m_run is None else jnp.maximum(m_run, sc)
                m_rows.append(jnp.max(m_run, -1, keepdims=True))
            m = jnp.concatenate(m_rows, 0)
            l_rows = []
            for r0 in range(0, qb, rs_sum):
                rows = slice(r0, r0 + rs_sum)
                l_run = None
                for c0 in range(0, ext, BLOCK):
                    cols = slice(c0, c0 + BLOCK)
                    p = jnp.exp2((s_s[h, rows, cols] - m[rows]) * exp2_scale)
                    l_run = p if l_run is None else l_run + p
                    p_s[h, rows, cols] = p.astype(BF16)
                l_rows.append(jnp.sum(l_run, -1, keepdims=True))
            o = _dot(p_s[h, :, 0:ext], v_ref[0:ext, h * V_DIM:(h + 1) * V_DIM])
            o_ref[:, h * V_DIM:(h + 1) * V_DIM] = (o / jnp.concatenate(l_rows, 0)).astype(o_ref.dtype)

    for jv, ext in enumerate(exts):
        pl.when(jq == jv)(functools.partial(block, jv, ext))


def mla_prompt(qfull, cos_t, sin_t, kv, krb, *, n_batch, tp, padf, n_heads):
    heads = 2
    n_qb = 4
    qb = tp // n_qb
    assert tp % n_qb == 0 and qb % 32 == 0
    exts = tuple(min(tp, -(-((jv + 1) * qb) // BLOCK) * BLOCK) for jv in range(n_qb))
    hp_n = n_heads // heads
    nope_w = n_heads * NOPE
    rope_blk0 = nope_w // (heads * ROPE_DIM)
    return pl.pallas_call(
        functools.partial(_mla_prompt_kernel, padf=padf, heads=heads, qb=qb, exts=exts),
        grid=(n_batch, hp_n, n_qb),
        in_specs=[pl.BlockSpec((qb, heads * NOPE), lambda b, hp, j: (b * n_qb + j, hp)),
                  pl.BlockSpec((qb, heads * ROPE_DIM), lambda b, hp, j: (b * n_qb + j, rope_blk0 + hp)),
                  pl.BlockSpec((qb, heads * ROPE_DIM),
                               lambda b, hp, j: (b * n_qb + j, rope_blk0 + hp_n + hp)),
                  pl.BlockSpec((qb, heads * ROPE_DIM), lambda b, hp, j: (j, 0)),
                  pl.BlockSpec((qb, heads * ROPE_DIM), lambda b, hp, j: (j, 0)),
                  pl.BlockSpec((tp, heads * NOPE), lambda b, hp, j: (b, hp)),
                  pl.BlockSpec((tp, heads * V_DIM), lambda b, hp, j: (b, hp_n + hp)),
                  pl.BlockSpec((tp, ROPE_DIM), lambda b, hp, j: (b, 0))],
        out_specs=pl.BlockSpec((qb, heads * V_DIM), lambda b, hp, j: (b * n_qb + j, hp)),
        out_shape=jax.ShapeDtypeStruct((n_batch * tp, n_heads * V_DIM), BF16),
        scratch_shapes=[pltpu.VMEM((heads, qb, tp), F32), pltpu.VMEM((heads, qb, tp), BF16)],
        compiler_params=_params(3), name="mla_prompt",
    )(qfull, qfull, qfull, cos_t, sin_t, kv, kv, krb)


def _head_proj_kernel(a_ref, w_ref, o_ref):
    o_ref[...] = _dot(a_ref[...].astype(BF16), w_ref[...].astype(BF16)).astype(o_ref.dtype)


def head_proj(a, w, *, row0, m_rows, k_head, n_head, a_col0=0):
    n_heads = w.shape[0]
    tm = _pick(math.gcd(row0, m_rows) if row0 else m_rows, (512, 256, 128))
    assert a_col0 % k_head == 0
    return pl.pallas_call(
        _head_proj_kernel, grid=(n_heads, m_rows // tm),
        in_specs=[pl.BlockSpec((tm, k_head), lambda h, i: (row0 // tm + i, a_col0 // k_head + h)),
                  pl.BlockSpec((None, k_head, n_head), lambda h, i: (h, 0, 0))],
        out_specs=pl.BlockSpec((tm, n_head), lambda h, i: (i, h)),
        out_shape=jax.ShapeDtypeStruct((m_rows, n_heads * n_head), BF16),
        compiler_params=_params(2), name="head_proj",
    )(a, w)


def _mla_sample_kernel(pt_ref, ql_ref, qr_ref, qs_ref, cos_ref, sin_ref, new_ref, *rest,
                       n_pages, n_chains, n_heads, n_steps):
    page_refs = rest[:n_pages]
    o_ref = rest[n_pages]
    q_s, m_s, l_s, acc_s = rest[n_pages + 1:]
    step = pl.program_id(1)
    rows = q_s.shape[0]
    exp2_scale = MLA_SCALE * math.log2(math.e)

    @pl.when(step == 0)
    def _():
        q_s[:, :KV_LORA] = ql_ref[...]
        q_s[:, KV_LORA:] = (qr_ref[...] * cos_ref[...] + qs_ref[...] * sin_ref[...]).astype(BF16)
        new = new_ref[...]
        new = jnp.concatenate([new, jnp.zeros_like(new)], 0).astype(BF16)
        t_pad = new.shape[0]
        s = _dot_nt(q_s[...], new)
        t_q = lax.broadcasted_iota(jnp.int32, (rows, t_pad), 0) // n_heads
        t_k = lax.broadcasted_iota(jnp.int32, (rows, t_pad), 1)
        s = jnp.where(t_k <= t_q, s, NEG)
        mx = jnp.max(s, -1, keepdims=True)
        p = jnp.exp2((s - mx) * exp2_scale)
        m_s[0] = mx
        l_s[0] = jnp.sum(p, -1, keepdims=True)
        acc_s[0] = _dot(p.astype(BF16), new[:, :KV_LORA])
        for ch in range(1, n_chains):
            m_s[ch] = jnp.full((rows, 1), NEG, F32)
            l_s[ch] = jnp.zeros((rows, 1), F32)
            acc_s[ch] = jnp.zeros((rows, KV_LORA), F32)

    q = q_s[...]
    per = n_pages // n_chains
    for ch in range(n_chains):
        keys = jnp.concatenate([ref[...].astype(BF16) for ref in page_refs[ch * per:(ch + 1) * per]], 0)
        s = _dot_nt(q, keys)
        m_old = m_s[ch]
        m_new = jnp.maximum(m_old, jnp.max(s, -1, keepdims=True))
        a = jnp.exp2((m_old - m_new) * exp2_scale)
        p = jnp.exp2((s - m_new) * exp2_scale)
        m_s[ch] = m_new
        l_s[ch] = a * l_s[ch] + jnp.sum(p, -1, keepdims=True)
        acc_s[ch] = a * acc_s[ch] + _dot(p.astype(BF16), keys[:, :KV_LORA])

    @pl.when(step == n_steps - 1)
    def _():
        m = m_s[0]
        for ch in range(1, n_chains):
            m = jnp.maximum(m, m_s[ch])
        l = jnp.zeros((rows, 1), F32)
        acc = jnp.zeros((rows, KV_LORA), F32)
        for ch in range(n_chains):
            w = jnp.exp2((m_s[ch] - m) * exp2_scale)
            l = l + w * l_s[ch]
            acc = acc + w * acc_s[ch]
        o_ref[...] = (acc / l).astype(o_ref.dtype)


def mla_sample(page_table, q_lat, q_r, q_s, cos_t, sin_t, rows, cache_all, layer,
               *, row0, n_dec, t_new, n_heads):
    n_pg = page_table.shape[1]
    pps = _pick(n_pg, (PAGES_PER_STEP, 16, 8, 4, 2, 1))
    n_chains = _pick(pps, (MLA_SAMPLE_CHAINS, 1))
    n_steps = n_pg // pps
    qrows = t_new * n_heads
    assert row0 % t_new == 0
    row_spec = lambda w: pl.BlockSpec((qrows, w), lambda b, s, pt: (b, 0))
    tab_spec = pl.BlockSpec((qrows, ROPE_DIM), lambda b, s, pt: (0, 0))

    def page_spec(k):
        return pl.BlockSpec((None, None, PAGE_SIZE, KV_LORA + ROPE_DIM),
                            lambda b, s, pt, k=k: (layer, pt[b * n_pg + s * pps + k], 0, 0))

    grid_spec = pltpu.PrefetchScalarGridSpec(
        num_scalar_prefetch=1, grid=(n_dec, n_steps),
        in_specs=[row_spec(KV_LORA), row_spec(ROPE_DIM), row_spec(ROPE_DIM), tab_spec, tab_spec,
                  pl.BlockSpec((t_new, KV_LORA + ROPE_DIM), lambda b, s, pt: (row0 // t_new + b, 0))]
                 + [page_spec(k) for k in range(pps)],
        out_specs=row_spec(KV_LORA),
        scratch_shapes=[pltpu.VMEM((qrows, KV_LORA + ROPE_DIM), BF16), pltpu.VMEM((n_chains, qrows, 1), F32),
                        pltpu.VMEM((n_chains, qrows, 1), F32), pltpu.VMEM((n_chains, qrows, KV_LORA), F32)])
    return pl.pallas_call(
        functools.partial(_mla_sample_kernel, n_pages=pps, n_chains=n_chains, n_heads=n_heads,
                          n_steps=n_steps),
        grid_spec=grid_spec,
        out_shape=jax.ShapeDtypeStruct((n_dec * qrows, KV_LORA), BF16),
        compiler_params=_params(2), name="mla_sample",
    )(page_table.reshape(-1), q_lat, q_r, q_s, cos_t, sin_t, rows, *([cache_all] * pps))


def _router_kernel(x_ref, w_ref, b_ref, o_ref):
    o_ref[...] = _dot3(x_ref[...], w_ref[...]) + b_ref[...]


def router_logits(x, w, b):
    m, d = x.shape
    tm = _pick(m, (256, 128))
    n = w.shape[1]
    return pl.pallas_call(
        _router_kernel, grid=(m // tm,),
        in_specs=[pl.BlockSpec((tm, d), lambda i: (i, 0)), pl.BlockSpec((d, n), lambda i: (0, 0)),
                  pl.BlockSpec((1, n), lambda i: (0, 0))],
        out_specs=pl.BlockSpec((tm, n), lambda i: (i, 0)),
        out_shape=jax.ShapeDtypeStruct((m, n), F32),
        compiler_params=_params(1), name="router",
    )(x, w, b)


def _gmm_kernel(be_ref, nu_ref, x_ref, *rest, n_w, gated):
    w_refs = rest[:n_w]
    o_ref = rest[n_w]
    wb_refs = rest[n_w + 1:]
    j = pl.program_id(1)
    active = j < nu_ref[0]
    changed = (j == 0) | (be_ref[j] != be_ref[jnp.maximum(j - 1, 0)])

    @pl.when(active & changed)
    def _():
        for w_ref, wb_ref in zip(w_refs, wb_refs):
            wb_ref[...] = w_ref[...].astype(BF16)

    @pl.when(active)
    def _():
        x = x_ref[...]
        if gated:
            gate = _dot(x, wb_refs[0][...])
            up = _dot(x, wb_refs[1][...])
            o_ref[...] = (gate * jax.nn.sigmoid(gate) * up).astype(o_ref.dtype)
        else:
            o_ref[...] = _dot(x, wb_refs[0][...]).astype(o_ref.dtype)

    @pl.when(jnp.logical_not(active))
    def _():
        o_ref[...] = jnp.zeros(o_ref.shape, o_ref.dtype)


def grouped_matmul(x, weights, layer, block_expert, n_used, *, tn, out_dtype, gated):
    n_slots, k = x.shape
    n = weights[0].shape[3]
    n_w = len(weights)
    grid_spec = pltpu.PrefetchScalarGridSpec(
        num_scalar_prefetch=2, grid=(n // tn, n_slots // MOE_BLOCK),
        in_specs=[pl.BlockSpec((MOE_BLOCK, k), lambda c, j, be, nu: (j, 0))]
                 + [pl.BlockSpec((None, None, k, tn), lambda c, j, be, nu: (layer, be[j], 0, c))
                    for _ in range(n_w)],
        out_specs=pl.BlockSpec((MOE_BLOCK, tn), lambda c, j, be, nu: (j, c)),
        scratch_shapes=[pltpu.VMEM((k, tn), BF16) for _ in range(n_w)])
    return pl.pallas_call(
        functools.partial(_gmm_kernel, n_w=n_w, gated=gated),
        grid_spec=grid_spec,
        out_shape=jax.ShapeDtypeStruct((n_slots, n), out_dtype),
        compiler_params=_params(2), name="grouped_matmul",
    )(block_expert, n_used, x, *weights)


def moe_layer(x, xb, valid, rg_w, rg_b, re_w, re_b, w_gate, w_up, w_down, layer, ln_g, ln_b, alpha):
    m, d = x.shape
    n_route = N_GROUPS_E + N_EXPERTS
    w_r = jnp.zeros((d, BLOCK), F32).at[:, :N_GROUPS_E].set(rg_w.astype(F32))
    w_r = w_r.at[:, N_GROUPS_E:n_route].set(re_w.astype(F32))
    b_r = jnp.zeros((1, BLOCK), F32).at[0, :N_GROUPS_E].set(rg_b.astype(F32))
    b_r = b_r.at[0, N_GROUPS_E:n_route].set(re_b.astype(F32))
    logits = router_logits(x, w_r, b_r)
    rows = jnp.arange(m)
    g_logits = logits[:, :N_GROUPS_E]
    g_idx = jnp.argmax(g_logits, -1).astype(jnp.int32)
    g_gate = jax.nn.softmax(g_logits, -1)[rows, g_idx][:, None]
    e_logits = logits[:, N_GROUPS_E:n_route].reshape(m, N_GROUPS_E, EPG)[rows, g_idx]
    top_logit, top_j = lax.top_k(e_logits, TOP_K)
    gates = g_gate * jax.nn.softmax(top_logit, -1)
    experts = g_idx[:, None] * EPG + top_j.astype(jnp.int32)
    experts = jnp.where(valid[:, None], experts, N_EXPERTS)

    n_assign = m * TOP_K
    e_flat = experts.reshape(n_assign)
    order = jnp.argsort(e_flat)
    e_sorted = e_flat[order]
    counts = jnp.bincount(e_flat, length=N_EXPERTS + 1)[:N_EXPERTS]
    padded = (counts + MOE_BLOCK - 1) // MOE_BLOCK * MOE_BLOCK
    pad_end = jnp.cumsum(padded)
    pad_start = pad_end - padded
    start = jnp.cumsum(counts) - counts
    n_blocks = (n_assign + N_EXPERTS * (MOE_BLOCK - 1) + MOE_BLOCK - 1) // MOE_BLOCK
    n_slots = n_blocks * MOE_BLOCK
    e_clip = jnp.minimum(e_sorted, N_EXPERTS - 1)
    dest = jnp.where(e_sorted < N_EXPERTS,
                     pad_start[e_clip] + jnp.arange(n_assign, dtype=jnp.int32) - start[e_clip],
                     n_slots).astype(jnp.int32)
    slot_tok = jnp.zeros((n_slots,), jnp.int32).at[dest].set((order // TOP_K).astype(jnp.int32), mode='drop')
    block_expert = jnp.minimum(
        jnp.searchsorted(pad_end, jnp.arange(n_blocks, dtype=jnp.int32) * MOE_BLOCK, side='right'),
        N_EXPERTS - 1).astype(jnp.int32)
    n_used = (pad_end[-1] // MOE_BLOCK).astype(jnp.int32).reshape(1)
    slot_of = jnp.zeros((n_assign,), jnp.int32).at[order].set(jnp.minimum(dest, n_slots - 1))

    xs = xb[slot_tok]
    hidden = grouped_matmul(xs, [w_gate, w_up], layer, block_expert, n_used, tn=512, out_dtype=BF16,
                            gated=True)
    y_slots = grouped_matmul(hidden, [w_down], layer, block_expert, n_used, tn=w_down.shape[3],
                             out_dtype=BF16, gated=False)
    y01 = y_slots[slot_of].reshape(m, TOP_K * d)
    return moe_add_layer_norm(x, y01, gates, ln_g, ln_b, alpha)


def _rope_tables(pos):
    half = ROPE_DIM // 2
    inv = ROPE_THETA ** (-jnp.arange(half, dtype=F32) / half)
    ang = pos.astype(F32)[:, None] * inv
    cos, sin = jnp.cos(ang), jnp.sin(ang)
    return jnp.concatenate([cos, cos], -1), jnp.concatenate([sin, sin], -1)


def _rotate_half_cols(w):
    half = ROPE_DIM // 2
    return jnp.concatenate([-w[..., half:], w[..., :half]], -1)


def even_layer(x, xb, dims, layer, kv_cache_all, h0, w_in_all, sinks, ssm_p, w_out_all):
    n_batch, tp, padf, t_real, n_dec, t_new, past_len = dims
    m, d = x.shape
    mp = n_batch * tp
    ms = n_dec * t_new
    w_b = d // 2
    w_a = w_b
    kv_w = (w_in_all.shape[2] - w_a - 2 * w_b) // 2
    g_b = w_b // P_B
    z = matmul([(xb, 0)], w_in_all, kp=d, m_rows=m, tn=512, out_dtype=F32, layer=layer)

    attn_p = swa_prompt(z, sinks.astype(F32), n_batch=n_batch, tp=tp, padf=padf, w_a=w_a, kv_w=kv_w)
    attn_s, kv_s = swa_sample(z, kv_cache_all, layer, sinks, row0=mp, n_dec=n_dec, t_new=t_new,
                              past_len=past_len, w_a=w_a, kv_w=kv_w)
    zp = z[:mp].reshape(n_batch, tp, -1)
    meta = slice(padf, padf + N_META)
    last = slice(tp - WINDOW, tp)
    kv_p = jnp.stack(
        [jnp.concatenate([zp[:, meta, w_a + o * kv_w:w_a + (o + 1) * kv_w],
                          zp[:, last, w_a + o * kv_w:w_a + (o + 1) * kv_w]], 1) for o in range(2)],
        axis=1).reshape(n_batch, 2, N_META + WINDOW, kv_w // DH_A, DH_A)
    kv_s = kv_s.reshape(n_dec, 2, N_META + WINDOW, kv_w // DH_A, DH_A)

    u0 = w_a + 2 * kv_w
    nc = t_real // SSM_CHUNK
    bp = -(-n_batch // SUBLANES) * SUBLANES
    u_p = zp[:, padf:, u0:u0 + w_b].reshape(n_batch, nc, SSM_CHUNK, g_b, P_B).transpose(3, 1, 0, 2, 4)
    u_p = jnp.pad(u_p, ((0, 0), (0, 0), (0, bp - n_batch), (0, 0), (0, 0))).reshape(g_b, nc * bp, SSM_CHUNK * P_B)
    mats, pw = ssm_matrices(*ssm_p, SSM_CHUNK)
    y_p, h_p = ssm_prompt(u_p, mats, pw[SSM_CHUNK], nc=nc, bp=bp)
    y_p = y_p.reshape(g_b, nc, bp, SSM_CHUNK, P_B)[:, :, :n_batch].transpose(2, 1, 3, 0, 4)
    y_p = jnp.pad(y_p.reshape(n_batch, t_real, w_b), ((0, 0), (padf, 0), (0, 0))).reshape(mp, w_b)
    h_p = h_p[:, :n_batch].reshape(g_b, n_batch, 2, N_B).transpose(1, 0, 3, 2)

    u_s = z[mp:, u0:u0 + w_b].reshape(n_dec, t_new, g_b, P_B).transpose(2, 0, 1, 3).reshape(g_b, n_dec, t_new * P_B)
    s0 = h0.astype(F32).transpose(1, 0, 3, 2).reshape(g_b, n_dec, 2 * N_B)
    y_s, h_s = ssm_sample(u_s, s0, ssm_sub_matrices(mats, pw, t_new))
    y_s = y_s.reshape(g_b, n_dec, t_new, P_B).transpose(1, 2, 0, 3).reshape(ms, w_b)
    h_s = h_s.reshape(g_b, n_dec, 2, N_B).transpose(1, 0, 3, 2)

    y_glu = glu(jnp.concatenate([y_p, y_s], 0), z, gate_col0=u0 + w_b)
    attn = jnp.concatenate([attn_p, attn_s], 0)
    mix = matmul([(attn, 0), (y_glu, 0)], w_out_all, kp=w_a, m_rows=m, tn=512, out_dtype=F32, layer=layer)
    return mix, kv_p, kv_s, h_p, h_s


def odd_layer(x, xb, dims, tables, layer, pool_buf, cache_all, page_table, w_in, pool_w, pool_scale,
              q_norm, w_uq, kv_norm, w_uk, w_uv, w_out_all):
    n_batch, tp, padf, t_real, n_dec, t_new, past_len = dims
    cs_rows, cos_p, sin_p, cos_s, sin_s = tables
    m, d = x.shape
    mp = n_batch * tp
    ms = n_dec * t_new
    c_width = pool_scale.shape[0]
    q_lora = q_norm.shape[0]
    n_heads = w_uq.shape[1]
    kr0 = c_width + q_lora + KV_LORA
    w_in_x = jnp.concatenate([w_in, _rotate_half_cols(w_in[:, kr0:])], 1)
    z = matmul([(xb, 0)], w_in_x, kp=d, m_rows=m, tn=w_in_x.shape[1] // 3, out_dtype=F32)

    y_pool_p = pool_prompt(z, pool_w, pool_scale, n_batch=n_batch, tp=tp, padf=padf, c_width=c_width)
    buf16 = jnp.pad(pool_buf.astype(F32), ((0, 0), (1, 0), (0, 0)))
    y_pool_s = pool_sample(z, buf16, pool_w, pool_scale, row0=mp, n_dec=n_dec, t_new=t_new, c_width=c_width)
    zp = z[:mp].reshape(n_batch, tp, -1)
    new_pool_p = zp[:, tp - (POOL_MAX - 1):, :c_width]
    u_s = z[mp:, :c_width].reshape(n_dec, t_new, c_width)
    new_pool_s = jnp.concatenate([pool_buf.astype(F32), u_s], 1)[:, -(POOL_MAX - 1):]

    cqn, rows, ckvb, krb = mla_prep(z, cs_rows, q_norm, kv_norm, c_width=c_width, q_lora=q_lora)
    w_q = jnp.concatenate([w_uq[..., :NOPE].reshape(q_lora, n_heads * NOPE),
                           w_uq[..., NOPE:].reshape(q_lora, n_heads * ROPE_DIM),
                           _rotate_half_cols(w_uq[..., NOPE:]).reshape(q_lora, n_heads * ROPE_DIM)], 1)
    qfull = matmul([(cqn, 0)], w_q, kp=q_lora, m_rows=m, tn=1024, out_dtype=F32)
    w_kv = jnp.concatenate([w_uk.reshape(KV_LORA, n_heads * NOPE), w_uv.reshape(KV_LORA, n_heads * V_DIM)], 1)
    kv = matmul([(ckvb, 0)], w_kv, kp=KV_LORA, m_rows=mp, tn=1024, out_dtype=BF16)
    y_mla_p = mla_prompt(qfull, cos_p, sin_p, kv, krb, n_batch=n_batch, tp=tp, padf=padf, n_heads=n_heads)

    q_lat = head_proj(qfull, w_uk.transpose(1, 2, 0), row0=mp, m_rows=ms, k_head=NOPE, n_head=KV_LORA)
    nope_w = n_heads * NOPE
    rope_w = n_heads * ROPE_DIM
    q_r = qfull[mp:, nope_w:nope_w + rope_w].reshape(ms * n_heads, ROPE_DIM)
    q_s = qfull[mp:, nope_w + rope_w:].reshape(ms * n_heads, ROPE_DIM)
    o_lat = mla_sample(page_table, q_lat.reshape(ms * n_heads, KV_LORA), q_r, q_s, cos_s, sin_s, rows,
                       cache_all, layer, row0=mp, n_dec=n_dec, t_new=t_new, n_heads=n_heads)
    y_mla_s = head_proj(o_lat.reshape(ms, n_heads * KV_LORA), w_uv.transpose(1, 0, 2), row0=0, m_rows=ms,
                        k_head=KV_LORA, n_head=V_DIM)

    y_pool = jnp.concatenate([y_pool_p, y_pool_s], 0)
    y_mla = jnp.concatenate([y_mla_p, y_mla_s], 0)
    pieces = [(y_pool, 0)] + [(y_mla, cb) for cb in range(y_mla.shape[1] // c_width)]
    mix = matmul(pieces, w_out_all, kp=c_width, m_rows=m, tn=512, out_dtype=F32, layer=layer)
    rows_p = rows[:mp].reshape(n_batch, tp, -1)[:, padf:]
    rows_s = rows[mp:].reshape(n_dec, t_new, -1)
    return mix, new_pool_p, new_pool_s, rows_p, rows_s


def kernel(x_prompt, x_sample, cache_swa_kv, state_ssm, state_pool, cache_mla, page_table, meta_tokens, w_in_ab, attn_sinks, ssm_lambda_re, ssm_lambda_im, ssm_log_dt, ssm_b_re, ssm_b_im, ssm_c_re, ssm_c_im, ssm_d, w_out_ab, w_in_cd, pool_w, pool_scale, mla_q_norm, mla_w_uq, mla_kv_norm, mla_w_uk, mla_w_uv, w_out_cd, ln_mix_g, ln_mix_b, ln_ffn_g, ln_ffn_b, router_group_w, router_group_b, router_expert_w, router_expert_b, expert_w_gate, expert_w_up, expert_w_down):
    n_batch, seq, d = x_prompt.shape
    n_dec, t_new, _ = x_sample.shape
    depth = ln_mix_g.shape[0]
    past_len = page_table.shape[1] * PAGE_SIZE
    alpha = (2 * depth) ** 0.25
    t_real = N_META + seq
    padf = (-t_real) % BLOCK
    tp = t_real + padf
    assert t_real % SSM_CHUNK == 0 and padf % N_META == 0
    mp = n_batch * tp
    ms = n_dec * t_new
    dims = (n_batch, tp, padf, t_real, n_dec, t_new, past_len)

    meta = jnp.broadcast_to(meta_tokens.astype(F32)[None], (n_batch, N_META, d))
    xp = jnp.concatenate([jnp.zeros((n_batch, padf, d), F32), meta, x_prompt.astype(F32)], 1)
    x = jnp.concatenate([xp.reshape(mp, d), x_sample.astype(F32).reshape(ms, d)], 0)
    xb = x.astype(BF16)
    pos_p = jnp.arange(tp, dtype=jnp.int32) - padf
    valid = jnp.concatenate([jnp.tile(pos_p >= 0, n_batch), jnp.ones((ms,), bool)])

    pos_s = past_len + jnp.arange(t_new, dtype=jnp.int32)
    cos_p1, sin_p1 = _rope_tables(pos_p)
    cos_s1, sin_s1 = _rope_tables(pos_s)
    cs_rows = jnp.concatenate([jnp.tile(jnp.concatenate([cos_p1, sin_p1], 1), (n_batch, 1)),
                               jnp.tile(jnp.concatenate([cos_s1, sin_s1], 1), (n_dec, 1))], 0)
    n_heads_d = mla_w_uq.shape[2]
    tables = (cs_rows, jnp.tile(cos_p1, (1, 2)), jnp.tile(sin_p1, (1, 2)),
              jnp.repeat(cos_s1, n_heads_d, axis=0), jnp.repeat(sin_s1, n_heads_d, axis=0))

    swa_p, swa_s, ssm_p, ssm_s, pool_p, pool_s, mla_p, mla_s = [], [], [], [], [], [], [], []
    for l in range(depth):
        i = l // 2
        if l % 2 == 0:
            ssm_params = (ssm_lambda_re[i], ssm_lambda_im[i], ssm_log_dt[i], ssm_b_re[i], ssm_b_im[i],
                          ssm_c_re[i], ssm_c_im[i], ssm_d[i])
            mix, kv_p, kv_s, h_p, h_s = even_layer(x, xb, dims, i, cache_swa_kv, state_ssm[i], w_in_ab,
                                                   attn_sinks[i], ssm_params, w_out_ab)
            swa_p.append(kv_p)
            swa_s.append(kv_s)
            ssm_p.append(h_p)
            ssm_s.append(h_s)
        else:
            mix, np_p, np_s, rows_p, rows_s = odd_layer(
                x, xb, dims, tables, i, state_pool[i], cache_mla, page_table, w_in_cd[i], pool_w[i],
                pool_scale[i], mla_q_norm[i], mla_w_uq[i], mla_kv_norm[i], mla_w_uk[i], mla_w_uv[i],
                w_out_cd)
            pool_p.append(np_p)
            pool_s.append(np_s)
            mla_p.append(rows_p)
            mla_s.append(rows_s)
        x, xb = add_layer_norm(x, mix, ln_mix_g[l], ln_mix_b[l], alpha)
        x, xb = moe_layer(x, xb, valid, router_group_w[l], router_group_b[l], router_expert_w[l],
                          router_expert_b[l], expert_w_gate, expert_w_up, expert_w_down, l,
                          ln_ffn_g[l], ln_ffn_b[l], alpha)
    y_p = x[:mp].reshape(n_batch, tp, d)[:, padf + N_META:]
    y_s = x[mp:].reshape(n_dec, t_new, d)
    return (y_p, y_s, jnp.stack(swa_p), jnp.stack(swa_s), jnp.stack(ssm_p), jnp.stack(ssm_s),
            jnp.stack(pool_p), jnp.stack(pool_s), jnp.stack(mla_p), jnp.stack(mla_s))
```

```python
import functools
import math

import jax
import jax.numpy as jnp
from jax import lax
from jax.experimental import pallas as pl
from jax.experimental.pallas import tpu as pltpu

F32 = jnp.float32
BF16 = jnp.bfloat16

N_META = 16
BLOCK = 128
SUBLANES = 8
DH_A = 64
WINDOW = 128
P_B = 16
N_B = 64
POOL_WINDOWS = (2, 4, 8, 16)
POOL_MAX = 16
KV_LORA = 512
NOPE = 128
ROPE_DIM = 64
V_DIM = 128
ROPE_THETA = 10000.0
MLA_SCALE = (NOPE + ROPE_DIM) ** -0.5
PAGE_SIZE = 128
N_GROUPS_E = 4
EPG = 4
N_EXPERTS = N_GROUPS_E * EPG
TOP_K = 2
NEG = -1e30
EPS = 1e-5
SSM_CHUNK = 16
MOE_BLOCK = 256
PAGES_PER_STEP = 32
MLA_SAMPLE_CHAINS = 1
VMEM_LIMIT = 56 * 1024 * 1024


def _params(n_axes):
    return pltpu.CompilerParams(dimension_semantics=("arbitrary",) * n_axes,
                                vmem_limit_bytes=VMEM_LIMIT)


def _pick(n, cands):
    for c in cands:
        if n % c == 0:
            return c
    raise ValueError(f"no tile in {cands} divides {n}")


def _dot(a, b):
    return jnp.dot(a, b, preferred_element_type=F32)


def _dot_nt(a, b):
    return lax.dot_general(a, b, (((1,), (1,)), ((), ())), preferred_element_type=F32)


def _split(a):
    hi = a.astype(BF16)
    lo = (a - hi.astype(F32)).astype(BF16)
    return hi, lo


def _dot3(a, b):
    ah, al = _split(a)
    bh, bl = _split(b)
    return _dot(ah, bh) + (_dot(ah, bl) + _dot(al, bh))


def _mm_kernel(*refs, n_pieces):
    a_refs = refs[:n_pieces]
    w_refs = refs[n_pieces:2 * n_pieces]
    o_ref = refs[2 * n_pieces]
    wb_refs = refs[2 * n_pieces + 1:]

    @pl.when(pl.program_id(1) == 0)
    def _():
        for w_ref, wb_ref in zip(w_refs, wb_refs):
            wb_ref[...] = w_ref[...].astype(BF16)

    acc = _dot(a_refs[0][...].astype(BF16), wb_refs[0][...])
    for a_ref, wb_ref in zip(a_refs[1:], wb_refs[1:]):
        acc = acc + _dot(a_ref[...].astype(BF16), wb_ref[...])
    o_ref[...] = acc.astype(o_ref.dtype)


def matmul(pieces, w, *, kp, m_rows, tn, out_dtype, layer=None):
    n = w.shape[-1]
    tm = _pick(m_rows, (512, 256, 128))
    assert n % tn == 0 and (w.ndim == 3) == (layer is not None)
    n_p = len(pieces)
    in_specs = []
    for _, cb in pieces:
        in_specs.append(pl.BlockSpec((tm, kp), lambda j, i, cb=cb: (i, cb)))
    for p in range(n_p):
        if layer is None:
            in_specs.append(pl.BlockSpec((kp, tn), lambda j, i, p=p: (p, j)))
        else:
            in_specs.append(pl.BlockSpec((None, kp, tn), lambda j, i, p=p: (layer, p, j)))
    return pl.pallas_call(
        functools.partial(_mm_kernel, n_pieces=n_p),
        grid=(n // tn, m_rows // tm),
        in_specs=in_specs,
        out_specs=pl.BlockSpec((tm, tn), lambda j, i: (i, j)),
        out_shape=jax.ShapeDtypeStruct((m_rows, n), out_dtype),
        scratch_shapes=[pltpu.VMEM((kp, tn), BF16) for _ in range(n_p)],
        compiler_params=_params(2),
        name="matmul",
    )(*[a for a, _ in pieces], *([w] * n_p))


def _ln_core(h, g_ref, b_ref, of_ref, ob_ref):
    mu = jnp.mean(h, -1, keepdims=True)
    hc = h - mu
    var = jnp.mean(hc * hc, -1, keepdims=True)
    y = hc * lax.rsqrt(var + EPS) * g_ref[...] + b_ref[...]
    of_ref[...] = y
    ob_ref[...] = y.astype(BF16)


def _ln_kernel(x_ref, m_ref, g_ref, b_ref, of_ref, ob_ref, *, alpha):
    _ln_core(alpha * x_ref[...] + m_ref[...].astype(F32), g_ref, b_ref, of_ref, ob_ref)


def _ln_moe_kernel(x_ref, y0_ref, y1_ref, gt_ref, g_ref, b_ref, of_ref, ob_ref, *, alpha):
    gt = gt_ref[...]
    moe = gt[:, 0:1] * y0_ref[...].astype(F32) + gt[:, 1:2] * y1_ref[...].astype(F32)
    _ln_core(alpha * x_ref[...] + moe, g_ref, b_ref, of_ref, ob_ref)


def _ln_call(kern, row_inputs, small_inputs, m, d):
    tm = _pick(m, (256, 128))
    row_spec = pl.BlockSpec((tm, d), lambda i: (i, 0))
    in_specs = [pl.BlockSpec((tm, a.shape[1]), lambda i, r0=r0: (r0 // tm + i, 0)) for a, r0 in row_inputs]
    in_specs += [pl.BlockSpec(a.shape, lambda i: (0, 0)) for a in small_inputs]
    assert all(r0 % tm == 0 for _, r0 in row_inputs)
    return pl.pallas_call(
        kern, grid=(m // tm,), in_specs=in_specs,
        out_specs=[row_spec, row_spec],
        out_shape=[jax.ShapeDtypeStruct((m, d), F32), jax.ShapeDtypeStruct((m, d), BF16)],
        compiler_params=_params(1), name="add_layer_norm",
    )(*[a for a, _ in row_inputs], *small_inputs)


def add_layer_norm(x, mix, g, b, alpha):
    m, d = x.shape
    return _ln_call(functools.partial(_ln_kernel, alpha=alpha), [(x, 0), (mix, 0)],
                    [g.reshape(1, d), b.reshape(1, d)], m, d)


def moe_add_layer_norm(x, y01, gates, g, b, alpha):
    m, d = x.shape
    return _ln_call(functools.partial(_ln_moe_kernel, alpha=alpha), [(x, 0), (y01, 0), (y01, m), (gates, 0)],
                    [g.reshape(1, d), b.reshape(1, d)], m, d)


def _alibi_slope(h, n_heads):
    return 2.0 ** (-8.0 * (h + 1) / n_heads)


def _swa_prompt_kernel(sink_ref, q_ref, kc_ref, vc_ref, kp_ref, vp_ref, km_ref, vm_ref, o_ref,
                       *, padf, n_heads, kvh):
    j = pl.program_id(1)
    base = j * BLOCK - padf
    r = lax.broadcasted_iota(jnp.int32, (BLOCK, 2 * BLOCK), 0)
    c = lax.broadcasted_iota(jnp.int32, (BLOCK, 2 * BLOCK), 1)
    dist = r + BLOCK - c
    vis_band = (dist >= 0) & (dist <= WINDOW) & (base - BLOCK + c >= N_META)
    dist_f = dist.astype(F32)
    rm = lax.broadcasted_iota(jnp.int32, (BLOCK, N_META), 0)
    cm = lax.broadcasted_iota(jnp.int32, (BLOCK, N_META), 1)
    dist_m = base + rm - cm
    vis_meta = dist_m >= 0
    dist_mf = jnp.abs(dist_m).astype(F32)
    group = n_heads // kvh
    scale = DH_A ** -0.5
    for h in range(kvh):
        sl = slice(h * DH_A, (h + 1) * DH_A)
        k_band = jnp.concatenate([kp_ref[:, sl], kc_ref[:, sl]], 0).astype(BF16)
        v_band = jnp.concatenate([vp_ref[:, sl], vc_ref[:, sl]], 0).astype(BF16)
        k_meta = km_ref[:, sl].astype(BF16)
        v_meta = vm_ref[:, sl].astype(BF16)
        for g in range(group):
            hh = h * group + g
            hs = slice(hh * DH_A, (hh + 1) * DH_A)
            slope = _alibi_slope(hh, n_heads)
            sink = sink_ref[hh]
            q = q_ref[:, hs].astype(BF16)
            lb = jnp.where(vis_band, _dot_nt(q, k_band) * scale - slope * dist_f, NEG)
            lm = jnp.where(vis_meta, _dot_nt(q, k_meta) * scale - slope * dist_mf, NEG)
            mx = jnp.maximum(jnp.maximum(jnp.max(lb, -1, keepdims=True),
                                         jnp.max(lm, -1, keepdims=True)), sink)
            eb = jnp.exp(lb - mx)
            em = jnp.exp(lm - mx)
            den = (jnp.sum(eb, -1, keepdims=True) + jnp.sum(em, -1, keepdims=True)
                   + jnp.exp(sink - mx))
            inv = 1.0 / den
            o = _dot((eb * inv).astype(BF16), v_band) + _dot((em * inv).astype(BF16), v_meta)
            o_ref[:, hs] = o.astype(o_ref.dtype)


def swa_prompt(z, sinks, *, n_batch, tp, padf, w_a, kv_w):
    nb = tp // BLOCK
    kcol = w_a // kv_w
    meta_blk = padf // N_META
    rows_meta = tp // N_META

    def cur(col):
        return pl.BlockSpec((BLOCK, kv_w), lambda b, j: (b * nb + j, col))

    def prev(col):
        return pl.BlockSpec((BLOCK, kv_w), lambda b, j: (jnp.maximum(b * nb + j - 1, 0), col))

    def meta(col):
        return pl.BlockSpec((N_META, kv_w), lambda b, j: (b * rows_meta + meta_blk, col))

    n_heads = w_a // DH_A
    return pl.pallas_call(
        functools.partial(_swa_prompt_kernel, padf=padf, n_heads=n_heads, kvh=kv_w // DH_A),
        grid=(n_batch, nb),
        in_specs=[pl.BlockSpec(memory_space=pltpu.SMEM),
                  pl.BlockSpec((BLOCK, w_a), lambda b, j: (b * nb + j, 0)),
                  cur(kcol), cur(kcol + 1), prev(kcol), prev(kcol + 1), meta(kcol), meta(kcol + 1)],
        out_specs=pl.BlockSpec((BLOCK, w_a), lambda b, j: (b * nb + j, 0)),
        out_shape=jax.ShapeDtypeStruct((n_batch * tp, w_a), BF16),
        compiler_params=_params(2), name="swa_prompt",
    )(sinks, z, z, z, z, z, z, z)


def _swa_sample_kernel(sink_ref, slope_ref, q_ref, kn_ref, vn_ref, cache_ref, o_ref, kv_ref,
                       *, bb, t_new, past_len, n_heads, kvh):
    group = n_heads // kvh
    win = WINDOW
    n_keys = 2 * BLOCK
    n_fill = n_keys - (N_META + win + t_new)
    rows = group * t_new
    r = lax.broadcasted_iota(jnp.int32, (rows, n_keys), 0)
    c = lax.broadcasted_iota(jnp.int32, (rows, n_keys), 1)
    q_pos = past_len + r % t_new
    k_pos = jnp.where(c < N_META, c,
                      jnp.where(c < N_META + win, past_len - win + (c - N_META),
                                past_len + (c - N_META - win)))
    dist = q_pos - k_pos
    visible = (dist >= 0) & ((c < N_META) | ((k_pos >= N_META) & (dist <= WINDOW)))
    fill = jnp.zeros((n_fill, DH_A), F32)
    dist_f = jnp.abs(dist).astype(F32)
    scale = DH_A ** -0.5
    for b in range(bb):
        rs = slice(b * t_new, (b + 1) * t_new)
        for kv in range(2):
            new_ref = kn_ref if kv == 0 else vn_ref
            kv_ref[b, kv, 0:N_META, :] = cache_ref[b, kv, 0:N_META, :]
            kv_ref[b, kv, N_META:N_META + win - t_new, :] = cache_ref[b, kv, N_META + t_new:N_META + win, :]
            kv_ref[b, kv, N_META + win - t_new:N_META + win, :] = new_ref[rs, :]
        for h in range(kvh):
            sl = slice(h * DH_A, (h + 1) * DH_A)
            k_all = jnp.concatenate([cache_ref[b, 0, :, sl], kn_ref[rs, sl], fill], 0).astype(BF16)
            v_all = jnp.concatenate([cache_ref[b, 1, :, sl], vn_ref[rs, sl], fill], 0).astype(BF16)
            q = jnp.concatenate(
                [q_ref[rs, (h * group + g) * DH_A:(h * group + g + 1) * DH_A] for g in range(group)],
                0).astype(BF16)
            sink = sink_ref[h]
            logits = jnp.where(visible, _dot_nt(q, k_all) * scale - slope_ref[h] * dist_f, NEG)
            mx = jnp.maximum(jnp.max(logits, -1, keepdims=True), sink)
            e = jnp.exp(logits - mx)
            den = jnp.sum(e, -1, keepdims=True) + jnp.exp(sink - mx)
            o = _dot((e * (1.0 / den)).astype(BF16), v_all)
            for g in range(group):
                hh = h * group + g
                o_ref[rs, hh * DH_A:(hh + 1) * DH_A] = o[g * t_new:(g + 1) * t_new].astype(o_ref.dtype)


def swa_sample(z, cache_all, layer, sinks, *, row0, n_dec, t_new, past_len, w_a, kv_w):
    bb = _pick(n_dec, (8, 4, 2, 1))
    n_heads = w_a // DH_A
    kvh = kv_w // DH_A
    group = n_heads // kvh
    rb0 = row0 // (bb * t_new)
    assert row0 % (bb * t_new) == 0 and t_new == SUBLANES and cache_all.shape[3] == N_META + WINDOW
    kcol = w_a // kv_w
    sink_rows = jnp.repeat(sinks.astype(F32).reshape(kvh, group), t_new, axis=1)[..., None]
    slopes = jnp.asarray([_alibi_slope(h, n_heads) for h in range(n_heads)], F32)
    slope_rows = jnp.repeat(slopes.reshape(kvh, group), t_new, axis=1)[..., None]
    cache5 = cache_all.reshape(cache_all.shape[0], n_dec, 2, N_META + WINDOW, kv_w)
    small = pl.BlockSpec((kvh, group * t_new, 1), lambda i: (0, 0, 0))
    cache_spec = pl.BlockSpec((bb, 2, N_META + WINDOW, kv_w), lambda i: (i, 0, 0, 0))
    cache_in_spec = pl.BlockSpec((None, bb, 2, N_META + WINDOW, kv_w), lambda i: (layer, i, 0, 0, 0))
    return pl.pallas_call(
        functools.partial(_swa_sample_kernel, bb=bb, t_new=t_new, past_len=past_len,
                          n_heads=n_heads, kvh=kvh),
        grid=(n_dec // bb,),
        in_specs=[small, small,
                  pl.BlockSpec((bb * t_new, w_a), lambda i: (rb0 + i, 0)),
                  pl.BlockSpec((bb * t_new, kv_w), lambda i: (rb0 + i, kcol)),
                  pl.BlockSpec((bb * t_new, kv_w), lambda i: (rb0 + i, kcol + 1)),
                  cache_in_spec],
        out_specs=[pl.BlockSpec((bb * t_new, w_a), lambda i: (i, 0)), cache_spec],
        out_shape=[jax.ShapeDtypeStruct((n_dec * t_new, w_a), BF16),
                   jax.ShapeDtypeStruct(cache5.shape[1:], F32)],
        compiler_params=_params(1), name="swa_sample",
    )(sink_rows, slope_rows, z, z, z, cache5)


def ssm_terms(lam_re, lam_im, log_dt, b_re, b_im, c_re, c_im, length):
    hi = lax.Precision.HIGHEST
    g_n, n_n = lam_re.shape
    lam = lax.complex(lam_re.astype(F32), lam_im.astype(F32))
    lam_dt = lam * jnp.exp(log_dt.astype(F32))[:, None]
    lam_bar = jnp.exp(lam_dt)
    b_bar = ((lam_bar - 1.0) / lam)[..., None] * lax.complex(b_re.astype(F32), b_im.astype(F32))
    cc = lax.complex(c_re.astype(F32), c_im.astype(F32))
    steps = jnp.arange(length + 1, dtype=F32)
    pw = jnp.exp(lam_dt[None] * steps[:, None, None])
    cb = cc[None] * pw[:length, :, None, :]
    k_lag = (jnp.einsum('jgpn,gnq->jgpq', cb.real, b_bar.real, precision=hi)
             - jnp.einsum('jgpn,gnq->jgpq', cb.imag, b_bar.imag, precision=hi))
    t = jnp.arange(length)
    sin_c = (pw[length - 1 - t][:, :, :, None] * b_bar[None]).transpose(1, 0, 3, 2)
    g_c = (cc[None] * pw[1:length + 1][:, :, None, :]).transpose(1, 3, 0, 2)
    return k_lag, sin_c, g_c, pw


def ssm_sub_matrices(terms, d_skip, steps):
    k_lag, sin_c, g_c, pw = terms
    g_n, n_n = pw.shape[1:]
    length = k_lag.shape[0]
    w = steps * P_B
    t = jnp.arange(steps)
    lag = t[None, :] - t[:, None]
    k_st = jnp.where((lag >= 0)[:, :, None, None, None], k_lag[jnp.clip(lag, 0)], 0.0)
    mt = k_st.transpose(2, 0, 4, 1, 3)
    eye_t = jnp.eye(steps, dtype=F32)
    eye_p = jnp.eye(P_B, dtype=F32)
    mt = mt + (eye_t[None, :, None, :, None] * eye_p[None, None, :, None, :]
               * d_skip.astype(F32).reshape(g_n, P_B)[:, None, None, None, :])
    mt = mt.reshape(g_n, w, w)
    sin_s = sin_c[:, length - steps:].reshape(g_n, w, n_n)
    sin_t = jnp.concatenate([sin_s.real, sin_s.imag], -1)
    g_s = g_c[:, :, :steps].reshape(g_n, n_n, w)
    sout_t = jnp.concatenate([g_s.real, -g_s.imag], 1)
    lam_l = pw[steps]
    lam_a = jnp.concatenate([lam_l.real, lam_l.real], -1)[:, None, :]
    lam_b = jnp.concatenate([-lam_l.imag, lam_l.imag], -1)[:, None, :]
    return mt, sin_t, sout_t, lam_a, lam_b


def ssm_block_diag(terms, d_skip, gl):
    k_lag, sin_c, g_c, pw = terms
    length, g_n = k_lag.shape[:2]
    n_n = pw.shape[2]
    sg = g_n // gl
    eye = jnp.eye(gl, dtype=F32)
    lanes = gl * P_B
    bdk = jnp.einsum('jaxpq,xy->ajxqyp', k_lag.reshape(length, sg, gl, P_B, P_B), eye)
    bdk = bdk.reshape(sg, length, lanes, lanes).astype(BF16)

    def expand_in(m):
        m = jnp.einsum('axsqn,xy->asxqyn', m.reshape(sg, gl, length, P_B, n_n), eye)
        return m.reshape(sg, length, lanes, gl * n_n)

    def expand_out(m):
        m = jnp.einsum('axntp,xy->atxnyp', m.reshape(sg, gl, n_n, length, P_B), eye)
        return m.reshape(sg, length, gl * n_n, lanes)

    bd_in = jnp.concatenate([expand_in(sin_c.real), expand_in(sin_c.imag)], -1).astype(BF16)
    bd_out = jnp.concatenate([expand_out(g_c.real), expand_out(-g_c.imag)], 2).astype(BF16)
    lam_l = pw[length].reshape(sg, 1, gl * n_n)
    return bdk, bd_in, bd_out, d_skip.astype(F32).reshape(sg, 1, lanes), lam_l.real, lam_l.imag


def _gelu_tanh(y):
    return 0.5 * y * (1.0 + jnp.tanh(math.sqrt(2.0 / math.pi) * (y + 0.044715 * (y * y * y))))


def _ssm_prompt_kernel(u_ref, gate_ref, bdk_ref, bdin_ref, bdout_ref, d_ref, lre_ref, lim_ref, o_ref, h_ref,
                       x_s, e_s, sp_s, y_s, *, n_seq, cps, c_first):
    length = u_ref.shape[1]
    half = lre_ref.shape[1]
    for s in range(length):
        x_s[s] = u_ref[:, s, :].astype(BF16)
    e = _dot(x_s[0], bdin_ref[0])
    for s in range(1, length):
        e = e + _dot(x_s[s], bdin_ref[s])
    e_s[...] = e
    sp_s[...] = jnp.zeros(sp_s.shape, F32)
    lre = lre_ref[...]
    lim = lim_ref[...]
    for b in range(n_seq):
        sr = jnp.zeros((1, half), F32)
        si = jnp.zeros((1, half), F32)
        for c in range(c_first, cps):
            row = slice(b * cps + c, b * cps + c + 1)
            sp_s[row, :] = jnp.concatenate([sr, si], 1)
            ec = e_s[row, :]
            sr, si = lre * sr - lim * si + ec[:, :half], lre * si + lim * sr + ec[:, half:]
        h_ref[b:b + 1, :] = jnp.concatenate([sr, si], 1)
    sp = sp_s[...].astype(BF16)
    d_skip = d_ref[...]
    for t in range(length):
        acc = _dot(sp, bdout_ref[t])
        for j in range(t + 1):
            acc = acc + _dot(x_s[t - j], bdk_ref[j])
        y_s[:, t, :] = acc + d_skip * u_ref[:, t, :]
    o_ref[...] = (_gelu_tanh(y_s[...]) * jax.nn.sigmoid(gate_ref[...])).astype(o_ref.dtype)


def ssm_glu_prompt(z, terms, d_skip, *, n_batch, tp, padf, u_col0, w_b):
    gl = BLOCK // P_B
    bdk, bd_in, bd_out, d_l, lam_re, lam_im = ssm_block_diag(terms, d_skip, gl)
    sg = bdk.shape[0]
    m = z.shape[0]
    cps = tp // SSM_CHUNK
    n_seq = _pick(n_batch, (2, 1))
    rows = n_seq * cps
    states = bd_in.shape[3]
    assert m % SSM_CHUNK == 0 and u_col0 % BLOCK == 0 and w_b == sg * BLOCK and padf % SSM_CHUNK == 0
    z3 = z.reshape(m // SSM_CHUNK, SSM_CHUNK, z.shape[1])
    ucb, gcb = u_col0 // BLOCK, (u_col0 + w_b) // BLOCK

    def rows_spec(cb0):
        return pl.BlockSpec((rows, SSM_CHUNK, BLOCK), lambda a, i: (i, 0, cb0 + a))

    def per_sg(shape):
        return pl.BlockSpec((None,) + shape, lambda a, i: (a,) + (0,) * len(shape))

    return pl.pallas_call(
        functools.partial(_ssm_prompt_kernel, n_seq=n_seq, cps=cps, c_first=padf // SSM_CHUNK),
        grid=(sg, n_batch // n_seq),
        in_specs=[rows_spec(ucb), rows_spec(gcb), per_sg((SSM_CHUNK, BLOCK, BLOCK)),
                  per_sg((SSM_CHUNK, BLOCK, states)), per_sg((SSM_CHUNK, states, BLOCK)),
                  per_sg((1, BLOCK)), per_sg((1, states // 2)), per_sg((1, states // 2))],
        out_specs=[rows_spec(0), pl.BlockSpec((None, None, n_seq, states), lambda a, i: (a, i, 0, 0))],
        out_shape=[jax.ShapeDtypeStruct((n_batch * cps, SSM_CHUNK, w_b), BF16),
                   jax.ShapeDtypeStruct((sg, n_batch // n_seq, n_seq, states), F32)],
        scratch_shapes=[pltpu.VMEM((SSM_CHUNK, rows, BLOCK), BF16), pltpu.VMEM((rows, states), F32),
                        pltpu.VMEM((rows, states), F32), pltpu.VMEM((rows, SSM_CHUNK, BLOCK), F32)],
        compiler_params=_params(2), name="ssm_glu_prompt",
    )(z3, z3, bdk, bd_in, bd_out, d_l, lam_re, lam_im)


def _ssm_sample_kernel(u_ref, s0_ref, mt_ref, sin_ref, sout_ref, la_ref, lb_ref, y_ref, h_ref):
    u = u_ref[...]
    s0 = s0_ref[...]
    y_ref[...] = _dot3(u, mt_ref[...]) + _dot3(s0, sout_ref[...])
    h_ref[...] = (la_ref[...] * s0 + lb_ref[...] * pltpu.roll(s0, N_B, 1)
                  + _dot3(u, sin_ref[...]))


def ssm_sample(u, s0, mats):
    mt, sin_t, sout_t, lam_a, lam_b = mats
    g_n, n_dec, lp = u.shape
    n2 = 2 * N_B

    def per_group(shape):
        return pl.BlockSpec((None,) + shape, lambda g: (g, 0, 0))

    return pl.pallas_call(
        _ssm_sample_kernel, grid=(g_n,),
        in_specs=[per_group((n_dec, lp)), per_group((n_dec, n2)), per_group((lp, lp)),
                  per_group((lp, n2)), per_group((n2, lp)), per_group((1, n2)), per_group((1, n2))],
        out_specs=[per_group((n_dec, lp)), per_group((n_dec, n2))],
        out_shape=[jax.ShapeDtypeStruct((g_n, n_dec, lp), F32),
                   jax.ShapeDtypeStruct((g_n, n_dec, n2), F32)],
        compiler_params=_params(1), name="ssm_sample",
    )(u, s0, mt, sin_t, sout_t, lam_a, lam_b)


def _glu_kernel(y_ref, gate_ref, o_ref):
    o_ref[...] = (_gelu_tanh(y_ref[...]) * jax.nn.sigmoid(gate_ref[...])).astype(o_ref.dtype)


def glu(y, z, *, row0, gate_col0):
    m, w = y.shape
    tm = _pick(math.gcd(m, row0) if row0 else m, (512, 256, 128))
    tc = 512
    assert gate_col0 % tc == 0 and w % tc == 0
    return pl.pallas_call(
        _glu_kernel, grid=(m // tm, w // tc),
        in_specs=[pl.BlockSpec((tm, tc), lambda i, c: (i, c)),
                  pl.BlockSpec((tm, tc), lambda i, c: (row0 // tm + i, gate_col0 // tc + c))],
        out_specs=pl.BlockSpec((tm, tc), lambda i, c: (i, c)),
        out_shape=jax.ShapeDtypeStruct((m, w), BF16),
        compiler_params=_params(2), name="glu",
    )(y, z)


def _window_sums(ext, width):
    n = ext.shape[0]
    acc = ext
    k = 1
    while k < width:
        acc = acc[:n - 2 * k + 1] + acc[k:n - k + 1]
        k *= 2
    first = POOL_MAX - (width - 1)
    return acc[first:first + n - POOL_MAX]


def _pool_mix(ext, cur, count_fn, pw_ref, sc_ref, o_ref):
    cg = cur.shape[1] // len(POOL_WINDOWS)
    for g, width in enumerate(POOL_WINDOWS):
        sl = slice(g * cg, (g + 1) * cg)
        pooled = _window_sums(ext[:, sl], width) / count_fn(width) - cur[:, sl]
        mixed = _dot(pooled.astype(BF16), pw_ref[g].astype(BF16))
        o_ref[:, sl] = (mixed * sc_ref[:, sl]).astype(o_ref.dtype)


def _pool_prompt_kernel(cur_ref, prev_ref, pw_ref, sc_ref, o_ref, *, padf):
    j = pl.program_id(1)
    base = j * BLOCK - padf
    pos = base + lax.broadcasted_iota(jnp.int32, (BLOCK, 1), 0)
    pos_prev = base - POOL_MAX + lax.broadcasted_iota(jnp.int32, (POOL_MAX, 1), 0)
    cur = jnp.where(pos >= 0, cur_ref[...], 0.0)
    prev = jnp.where(pos_prev >= 0, prev_ref[...], 0.0)
    ext = jnp.concatenate([prev, cur], 0)

    def count(width):
        return jnp.maximum(jnp.minimum(pos + 1, width), 1).astype(F32)

    _pool_mix(ext, cur, count, pw_ref, sc_ref, o_ref)


def pool_prompt(z, pool_w, pool_scale, *, n_batch, tp, padf, c_width):
    nb = tp // BLOCK
    per = BLOCK // POOL_MAX
    return pl.pallas_call(
        functools.partial(_pool_prompt_kernel, padf=padf),
        grid=(n_batch, nb),
        in_specs=[pl.BlockSpec((BLOCK, c_width), lambda b, j: (b * nb + j, 0)),
                  pl.BlockSpec((POOL_MAX, c_width),
                               lambda b, j: (jnp.maximum((b * nb + j) * per - 1, 0), 0)),
                  pl.BlockSpec(pool_w.shape, lambda b, j: (0, 0, 0)),
                  pl.BlockSpec((1, c_width), lambda b, j: (0, 0))],
        out_specs=pl.BlockSpec((BLOCK, c_width), lambda b, j: (b * nb + j, 0)),
        out_shape=jax.ShapeDtypeStruct((n_batch * tp, c_width), BF16),
        compiler_params=_params(2), name="pool_prompt",
    )(z, z, pool_w, pool_scale.reshape(1, c_width))


def _pool_sample_kernel(cur_ref, buf_ref, pw_ref, sc_ref, o_ref, *, bb, t_new):
    cur = cur_ref[...]
    cg = cur.shape[1] // len(POOL_WINDOWS)
    for g, width in enumerate(POOL_WINDOWS):
        sl = slice(g * cg, (g + 1) * cg)
        sums = [_window_sums(jnp.concatenate([buf_ref[b, :, sl], cur[b * t_new:(b + 1) * t_new, sl]], 0),
                             width) for b in range(bb)]
        pooled = jnp.concatenate(sums, 0) / float(width) - cur[:, sl]
        mixed = _dot(pooled.astype(BF16), pw_ref[g].astype(BF16))
        o_ref[:, sl] = (mixed * sc_ref[:, sl]).astype(o_ref.dtype)


def pool_sample(z, buf16, pool_w, pool_scale, *, row0, n_dec, t_new, c_width):
    bb = _pick(n_dec, (8, 4, 2, 1))
    rb0 = row0 // (bb * t_new)
    assert row0 % (bb * t_new) == 0
    return pl.pallas_call(
        functools.partial(_pool_sample_kernel, bb=bb, t_new=t_new),
        grid=(n_dec // bb,),
        in_specs=[pl.BlockSpec((bb * t_new, c_width), lambda i: (rb0 + i, 0)),
                  pl.BlockSpec((bb, POOL_MAX, c_width), lambda i: (i, 0, 0)),
                  pl.BlockSpec(pool_w.shape, lambda i: (0, 0, 0)),
                  pl.BlockSpec((1, c_width), lambda i: (0, 0))],
        out_specs=pl.BlockSpec((bb * t_new, c_width), lambda i: (i, 0)),
        out_shape=jax.ShapeDtypeStruct((n_dec * t_new, c_width), BF16),
        compiler_params=_params(1), name="pool_sample",
    )(z, buf16, pool_w, pool_scale.reshape(1, c_width))


def _rms(x, g):
    return x * lax.rsqrt(jnp.mean(x * x, -1, keepdims=True) + EPS) * g


def _mla_prep_kernel(cq_ref, ckv_ref, kr_ref, cs_ref, qg_ref, kg_ref, cqn_ref, rows_ref, ckvb_ref, krb_ref):
    cqn_ref[...] = _rms(cq_ref[...], qg_ref[...]).astype(BF16)
    ckvn = _rms(ckv_ref[...], kg_ref[...])
    kr2 = kr_ref[...]
    cs = cs_ref[...]
    krr = kr2[:, :ROPE_DIM] * cs[:, :ROPE_DIM] + kr2[:, ROPE_DIM:] * cs[:, ROPE_DIM:]
    rows_ref[:, :KV_LORA] = ckvn
    rows_ref[:, KV_LORA:] = krr
    ckvb_ref[...] = ckvn.astype(BF16)
    krb_ref[...] = krr.astype(BF16)


def mla_prep(z, cs_rows, q_norm, kv_norm, *, c_width, q_lora):
    m = z.shape[0]
    tm = _pick(m, (256, 128))
    assert c_width % q_lora == 0 and (c_width + q_lora) % KV_LORA == 0
    kr_col = (c_width + q_lora + KV_LORA) // (2 * ROPE_DIM)
    row = lambda w, cb: pl.BlockSpec((tm, w), lambda i: (i, cb))
    return pl.pallas_call(
        _mla_prep_kernel, grid=(m // tm,),
        in_specs=[row(q_lora, c_width // q_lora), row(KV_LORA, (c_width + q_lora) // KV_LORA),
                  row(2 * ROPE_DIM, kr_col), row(2 * ROPE_DIM, 0),
                  pl.BlockSpec((1, q_lora), lambda i: (0, 0)), pl.BlockSpec((1, KV_LORA), lambda i: (0, 0))],
        out_specs=[row(q_lora, 0), row(KV_LORA + ROPE_DIM, 0), row(KV_LORA, 0), row(ROPE_DIM, 0)],
        out_shape=[jax.ShapeDtypeStruct((m, q_lora), BF16),
                   jax.ShapeDtypeStruct((m, KV_LORA + ROPE_DIM), F32),
                   jax.ShapeDtypeStruct((m, KV_LORA), BF16),
                   jax.ShapeDtypeStruct((m, ROPE_DIM), BF16)],
        compiler_params=_params(1), name="mla_prep",
    )(z, z, z, cs_rows, q_norm.reshape(1, q_lora), kv_norm.reshape(1, KV_LORA))


def _mla_prompt_kernel(qn_ref, qr_ref, qs_ref, cos_ref, sin_ref, kn_ref, v_ref, kr_ref, o_ref,
                       s_s, p_s, *, padf, heads, qb, exts):
    jq = pl.program_id(2)
    q_rope = (qr_ref[...] * cos_ref[...] + qs_ref[...] * sin_ref[...]).astype(BF16)
    q_nope = qn_ref[...].astype(BF16)
    rs_max = qb // 4
    rs_sum = qb // 2
    exp2_scale = MLA_SCALE * math.log2(math.e)

    def block(jv, ext):
        kr = kr_ref[0:ext, :]
        for h in range(heads):
            s_s[h, :, 0:ext] = (_dot_nt(q_nope[:, h * NOPE:(h + 1) * NOPE], kn_ref[0:ext, h * NOPE:(h + 1) * NOPE])
                                + _dot_nt(q_rope[:, h * ROPE_DIM:(h + 1) * ROPE_DIM], kr))
        for h in range(heads):
            m_rows = []
            for r0 in range(0, qb, rs_max):
                rows = slice(r0, r0 + rs_max)
                r = r0 + lax.broadcasted_iota(jnp.int32, (rs_max, BLOCK), 0)
                c = lax.broadcasted_iota(jnp.int32, (rs_max, BLOCK), 1)
                m_run = None
                for c0 in range(0, ext, BLOCK):
                    cols = slice(c0, c0 + BLOCK)
                    sc = s_s[h, rows, cols]
                    if c0 < padf or c0 + BLOCK - 1 > jv * qb + r0:
                        sc = jnp.where((c0 + c >= padf) & (c0 + c <= jv * qb + r), sc, NEG)
                        s_s[h, rows, cols] = sc
                    m_run = sc if m_run is None else jnp.maximum(m_run, sc)
                m_rows.append(jnp.max(m_run, -1, keepdims=True))
            m = jnp.concatenate(m_rows, 0)
            l_rows = []
            for r0 in range(0, qb, rs_sum):
                rows = slice(r0, r0 + rs_sum)
                l_run = None
                for c0 in range(0, ext, BLOCK):
                    cols = slice(c0, c0 + BLOCK)
                    p = jnp.exp2((s_s[h, rows, cols] - m[rows]) * exp2_scale)
                    l_run = p if l_run is None else l_run + p
                    p_s[h, rows, cols] = p.astype(BF16)
                l_rows.append(jnp.sum(l_run, -1, keepdims=True))
            o = _dot(p_s[h, :, 0:ext], v_ref[0:ext, h * V_DIM:(h + 1) * V_DIM])
            o_ref[:, h * V_DIM:(h + 1) * V_DIM] = (o / jnp.concatenate(l_rows, 0)).astype(o_ref.dtype)

    for jv, ext in enumerate(exts):
        pl.when(jq == jv)(functools.partial(block, jv, ext))


def mla_prompt(qfull, cos_t, sin_t, kv, krb, *, n_batch, tp, padf, n_heads):
    heads = 2
    n_qb = 4
    qb = tp // n_qb
    assert tp % n_qb == 0 and qb % 32 == 0
    exts = tuple(min(tp, -(-((jv + 1) * qb) // BLOCK) * BLOCK) for jv in range(n_qb))
    hp_n = n_heads // heads
    nope_w = n_heads * NOPE
    rope_blk0 = nope_w // (heads * ROPE_DIM)
    return pl.pallas_call(
        functools.partial(_mla_prompt_kernel, padf=padf, heads=heads, qb=qb, exts=exts),
        grid=(n_batch, hp_n, n_qb),
        in_specs=[pl.BlockSpec((qb, heads * NOPE), lambda b, hp, j: (b * n_qb + j, hp)),
                  pl.BlockSpec((qb, heads * ROPE_DIM), lambda b, hp, j: (b * n_qb + j, rope_blk0 + hp)),
                  pl.BlockSpec((qb, heads * ROPE_DIM),
                               lambda b, hp, j: (b * n_qb + j, rope_blk0 + hp_n + hp)),
                  pl.BlockSpec((qb, heads * ROPE_DIM), lambda b, hp, j: (j, 0)),
                  pl.BlockSpec((qb, heads * ROPE_DIM), lambda b, hp, j: (j, 0)),
                  pl.BlockSpec((tp, heads * NOPE), lambda b, hp, j: (b, hp)),
                  pl.BlockSpec((tp, heads * V_DIM), lambda b, hp, j: (b, hp_n + hp)),
                  pl.BlockSpec((tp, ROPE_DIM), lambda b, hp, j: (b, 0))],
        out_specs=pl.BlockSpec((qb, heads * V_DIM), lambda b, hp, j: (b * n_qb + j, hp)),
        out_shape=jax.ShapeDtypeStruct((n_batch * tp, n_heads * V_DIM), BF16),
        scratch_shapes=[pltpu.VMEM((heads, qb, tp), F32), pltpu.VMEM((heads, qb, tp), BF16)],
        compiler_params=_params(3), name="mla_prompt",
    )(qfull, qfull, qfull, cos_t, sin_t, kv, kv, krb)


def _head_proj_kernel(a_ref, w_ref, o_ref):
    o_ref[...] = _dot(a_ref[...].astype(BF16), w_ref[...].astype(BF16)).astype(o_ref.dtype)


def head_proj(a, w, *, row0, m_rows, k_head, n_head, a_col0=0):
    n_heads = w.shape[0]
    tm = _pick(math.gcd(row0, m_rows) if row0 else m_rows, (512, 256, 128))
    assert a_col0 % k_head == 0
    return pl.pallas_call(
        _head_proj_kernel, grid=(n_heads, m_rows // tm),
        in_specs=[pl.BlockSpec((tm, k_head), lambda h, i: (row0 // tm + i, a_col0 // k_head + h)),
                  pl.BlockSpec((None, k_head, n_head), lambda h, i: (h, 0, 0))],
        out_specs=pl.BlockSpec((tm, n_head), lambda h, i: (i, h)),
        out_shape=jax.ShapeDtypeStruct((m_rows, n_heads * n_head), BF16),
        compiler_params=_params(2), name="head_proj",
    )(a, w)


def _mla_sample_kernel(pt_ref, ql_ref, qr_ref, qs_ref, cos_ref, sin_ref, new_ref, *rest,
                       n_pages, n_chains, n_heads, n_steps):
    page_refs = rest[:n_pages]
    o_ref = rest[n_pages]
    q_s, m_s, l_s, acc_s = rest[n_pages + 1:]
    step = pl.program_id(1)
    rows = q_s.shape[0]
    exp2_scale = MLA_SCALE * math.log2(math.e)

    @pl.when(step == 0)
    def _():
        q_s[:, :KV_LORA] = ql_ref[...]
        q_s[:, KV_LORA:] = (qr_ref[...] * cos_ref[...] + qs_ref[...] * sin_ref[...]).astype(BF16)
        new = new_ref[...]
        new = jnp.concatenate([new, jnp.zeros_like(new)], 0).astype(BF16)
        t_pad = new.shape[0]
        s = _dot_nt(q_s[...], new)
        t_q = lax.broadcasted_iota(jnp.int32, (rows, t_pad), 0) // n_heads
        t_k = lax.broadcasted_iota(jnp.int32, (rows, t_pad), 1)
        s = jnp.where(t_k <= t_q, s, NEG)
        mx = jnp.max(s, -1, keepdims=True)
        p = jnp.exp2((s - mx) * exp2_scale)
        m_s[0] = mx
        l_s[0] = jnp.sum(p, -1, keepdims=True)
        acc_s[0] = _dot(p.astype(BF16), new[:, :KV_LORA])
        for ch in range(1, n_chains):
            m_s[ch] = jnp.full((rows, 1), NEG, F32)
            l_s[ch] = jnp.zeros((rows, 1), F32)
            acc_s[ch] = jnp.zeros((rows, KV_LORA), F32)

    q = q_s[...]
    per = n_pages // n_chains
    for ch in range(n_chains):
        keys_t = jnp.concatenate([ref[...].astype(BF16) for ref in page_refs[ch * per:(ch + 1) * per]], 1)
        s = _dot(q, keys_t)
        m_old = m_s[ch]
        m_new = jnp.maximum(m_old, jnp.max(s, -1, keepdims=True))
        a = jnp.exp2((m_old - m_new) * exp2_scale)
        p = jnp.exp2((s - m_new) * exp2_scale)
        m_s[ch] = m_new
        l_s[ch] = a * l_s[ch] + jnp.sum(p, -1, keepdims=True)
        acc_s[ch] = a * acc_s[ch] + _dot_nt(p.astype(BF16), keys_t[:KV_LORA])

    @pl.when(step == n_steps - 1)
    def _():
        m = m_s[0]
        for ch in range(1, n_chains):
            m = jnp.maximum(m, m_s[ch])
        l = jnp.zeros((rows, 1), F32)
        acc = jnp.zeros((rows, KV_LORA), F32)
        for ch in range(n_chains):
            w = jnp.exp2((m_s[ch] - m) * exp2_scale)
            l = l + w * l_s[ch]
            acc = acc + w * acc_s[ch]
        o_ref[...] = (acc / l).astype(o_ref.dtype)


def mla_sample(page_table, q_lat, q_r, q_s, cos_t, sin_t, rows, cache_all, layer,
               *, row0, n_dec, t_new, n_heads):
    n_pg = page_table.shape[1]
    pps = _pick(n_pg, (PAGES_PER_STEP, 16, 8, 4, 2, 1))
    n_chains = _pick(pps, (MLA_SAMPLE_CHAINS, 1))
    n_steps = n_pg // pps
    qrows = t_new * n_heads
    assert row0 % t_new == 0
    row_spec = lambda w: pl.BlockSpec((qrows, w), lambda b, s, pt: (b, 0))
    tab_spec = pl.BlockSpec((qrows, ROPE_DIM), lambda b, s, pt: (0, 0))

    def page_spec(k):
        return pl.BlockSpec((None, None, KV_LORA + ROPE_DIM, PAGE_SIZE),
                            lambda b, s, pt, k=k: (layer, pt[b * n_pg + s * pps + k], 0, 0))

    grid_spec = pltpu.PrefetchScalarGridSpec(
        num_scalar_prefetch=1, grid=(n_dec, n_steps),
        in_specs=[row_spec(KV_LORA), row_spec(ROPE_DIM), row_spec(ROPE_DIM), tab_spec, tab_spec,
                  pl.BlockSpec((t_new, KV_LORA + ROPE_DIM), lambda b, s, pt: (row0 // t_new + b, 0))]
                 + [page_spec(k) for k in range(pps)],
        out_specs=row_spec(KV_LORA),
        scratch_shapes=[pltpu.VMEM((qrows, KV_LORA + ROPE_DIM), BF16), pltpu.VMEM((n_chains, qrows, 1), F32),
                        pltpu.VMEM((n_chains, qrows, 1), F32), pltpu.VMEM((n_chains, qrows, KV_LORA), F32)])
    return pl.pallas_call(
        functools.partial(_mla_sample_kernel, n_pages=pps, n_chains=n_chains, n_heads=n_heads,
                          n_steps=n_steps),
        grid_spec=grid_spec,
        out_shape=jax.ShapeDtypeStruct((n_dec * qrows, KV_LORA), BF16),
        compiler_params=_params(2), name="mla_sample",
    )(page_table.reshape(-1), q_lat, q_r, q_s, cos_t, sin_t, rows, *([cache_all] * pps))


def _router_kernel(x_ref, w_ref, b_ref, o_ref):
    o_ref[...] = _dot3(x_ref[...], w_ref[...]) + b_ref[...]


def router_logits(x, w, b):
    m, d = x.shape
    tm = _pick(m, (256, 128))
    n = w.shape[1]
    return pl.pallas_call(
        _router_kernel, grid=(m // tm,),
        in_specs=[pl.BlockSpec((tm, d), lambda i: (i, 0)), pl.BlockSpec((d, n), lambda i: (0, 0)),
                  pl.BlockSpec((1, n), lambda i: (0, 0))],
        out_specs=pl.BlockSpec((tm, n), lambda i: (i, 0)),
        out_shape=jax.ShapeDtypeStruct((m, n), F32),
        compiler_params=_params(1), name="router",
    )(x, w, b)


def _gmm_kernel(be_ref, nu_ref, x_ref, *rest, n_w, gated):
    w_refs = rest[:n_w]
    o_ref = rest[n_w]
    wb_refs = rest[n_w + 1:]
    j = pl.program_id(1)
    active = j < nu_ref[0]
    changed = (j == 0) | (be_ref[j] != be_ref[jnp.maximum(j - 1, 0)])

    @pl.when(active & changed)
    def _():
        for w_ref, wb_ref in zip(w_refs, wb_refs):
            wb_ref[...] = w_ref[...].astype(BF16)

    @pl.when(active)
    def _():
        x = x_ref[...]
        if gated:
            gate = _dot(x, wb_refs[0][...])
            up = _dot(x, wb_refs[1][...])
            o_ref[...] = (gate * jax.nn.sigmoid(gate) * up).astype(o_ref.dtype)
        else:
            o_ref[...] = _dot(x, wb_refs[0][...]).astype(o_ref.dtype)

    @pl.when(jnp.logical_not(active))
    def _():
        o_ref[...] = jnp.zeros(o_ref.shape, o_ref.dtype)


def grouped_matmul(x, weights, layer, block_expert, n_used, *, tn, out_dtype, gated):
    n_slots, k = x.shape
    n = weights[0].shape[3]
    n_w = len(weights)
    grid_spec = pltpu.PrefetchScalarGridSpec(
        num_scalar_prefetch=2, grid=(n // tn, n_slots // MOE_BLOCK),
        in_specs=[pl.BlockSpec((MOE_BLOCK, k), lambda c, j, be, nu: (j, 0))]
                 + [pl.BlockSpec((None, None, k, tn), lambda c, j, be, nu: (layer, be[j], 0, c))
                    for _ in range(n_w)],
        out_specs=pl.BlockSpec((MOE_BLOCK, tn), lambda c, j, be, nu: (j, c)),
        scratch_shapes=[pltpu.VMEM((k, tn), BF16) for _ in range(n_w)])
    return pl.pallas_call(
        functools.partial(_gmm_kernel, n_w=n_w, gated=gated),
        grid_spec=grid_spec,
        out_shape=jax.ShapeDtypeStruct((n_slots, n), out_dtype),
        compiler_params=_params(2), name="grouped_matmul",
    )(block_expert, n_used, x, *weights)


def moe_layer(x, xb, valid, rg_w, rg_b, re_w, re_b, w_gate, w_up, w_down, layer, ln_g, ln_b, alpha):
    m, d = x.shape
    n_route = N_GROUPS_E + N_EXPERTS
    w_r = jnp.zeros((d, BLOCK), F32).at[:, :N_GROUPS_E].set(rg_w.astype(F32))
    w_r = w_r.at[:, N_GROUPS_E:n_route].set(re_w.astype(F32))
    b_r = jnp.zeros((1, BLOCK), F32).at[0, :N_GROUPS_E].set(rg_b.astype(F32))
    b_r = b_r.at[0, N_GROUPS_E:n_route].set(re_b.astype(F32))
    logits = router_logits(x, w_r, b_r)
    rows = jnp.arange(m)
    g_logits = logits[:, :N_GROUPS_E]
    g_idx = jnp.argmax(g_logits, -1).astype(jnp.int32)
    g_gate = jax.nn.softmax(g_logits, -1)[rows, g_idx][:, None]
    e_logits = logits[:, N_GROUPS_E:n_route].reshape(m, N_GROUPS_E, EPG)[rows, g_idx]
    top_logit, top_j = lax.top_k(e_logits, TOP_K)
    gates = g_gate * jax.nn.softmax(top_logit, -1)
    experts = g_idx[:, None] * EPG + top_j.astype(jnp.int32)
    experts = jnp.where(valid[:, None], experts, N_EXPERTS)

    n_assign = m * TOP_K
    e_flat = experts.reshape(n_assign)
    order = jnp.argsort(e_flat)
    e_sorted = e_flat[order]
    counts = jnp.bincount(e_flat, length=N_EXPERTS + 1)[:N_EXPERTS]
    padded = (counts + MOE_BLOCK - 1) // MOE_BLOCK * MOE_BLOCK
    pad_end = jnp.cumsum(padded)
    pad_start = pad_end - padded
    start = jnp.cumsum(counts) - counts
    n_blocks = (n_assign + N_EXPERTS * (MOE_BLOCK - 1) + MOE_BLOCK - 1) // MOE_BLOCK
    n_slots = n_blocks * MOE_BLOCK
    e_clip = jnp.minimum(e_sorted, N_EXPERTS - 1)
    dest = jnp.where(e_sorted < N_EXPERTS,
                     pad_start[e_clip] + jnp.arange(n_assign, dtype=jnp.int32) - start[e_clip],
                     n_slots).astype(jnp.int32)
    slot_tok = jnp.zeros((n_slots,), jnp.int32).at[dest].set((order // TOP_K).astype(jnp.int32), mode='drop')
    block_expert = jnp.minimum(
        jnp.searchsorted(pad_end, jnp.arange(n_blocks, dtype=jnp.int32) * MOE_BLOCK, side='right'),
        N_EXPERTS - 1).astype(jnp.int32)
    n_used = (pad_end[-1] // MOE_BLOCK).astype(jnp.int32).reshape(1)
    slot_of = jnp.zeros((n_assign,), jnp.int32).at[order].set(jnp.minimum(dest, n_slots - 1))

    xs = xb[slot_tok]
    hidden = grouped_matmul(xs, [w_gate, w_up], layer, block_expert, n_used, tn=512, out_dtype=BF16,
                            gated=True)
    y_slots = grouped_matmul(hidden, [w_down], layer, block_expert, n_used, tn=w_down.shape[3],
                             out_dtype=BF16, gated=False)
    y01 = y_slots[slot_of.reshape(m, TOP_K).T.reshape(-1)]
    return moe_add_layer_norm(x, y01, gates, ln_g, ln_b, alpha)


def _rope_tables(pos):
    half = ROPE_DIM // 2
    inv = ROPE_THETA ** (-jnp.arange(half, dtype=F32) / half)
    ang = pos.astype(F32)[:, None] * inv
    cos, sin = jnp.cos(ang), jnp.sin(ang)
    return jnp.concatenate([cos, cos], -1), jnp.concatenate([sin, sin], -1)


def _rotate_half_cols(w):
    half = ROPE_DIM // 2
    return jnp.concatenate([-w[..., half:], w[..., :half]], -1)


def even_layer(x, xb, dims, layer, kv_cache_all, h0, w_in_all, sinks, ssm_p, w_out_all):
    n_batch, tp, padf, t_real, n_dec, t_new, past_len = dims
    m, d = x.shape
    mp = n_batch * tp
    ms = n_dec * t_new
    w_b = d // 2
    w_a = w_b
    kv_w = (w_in_all.shape[2] - w_a - 2 * w_b) // 2
    g_b = w_b // P_B
    z = matmul([(xb, 0)], w_in_all, kp=d, m_rows=m, tn=512, out_dtype=F32, layer=layer)

    attn_p = swa_prompt(z, sinks.astype(F32), n_batch=n_batch, tp=tp, padf=padf, w_a=w_a, kv_w=kv_w)
    attn_s, kv_s = swa_sample(z, kv_cache_all, layer, sinks, row0=mp, n_dec=n_dec, t_new=t_new,
                              past_len=past_len, w_a=w_a, kv_w=kv_w)
    zp = z[:mp].reshape(n_batch, tp, -1)
    meta = slice(padf, padf + N_META)
    last = slice(tp - WINDOW, tp)
    kv_p = jnp.stack(
        [jnp.concatenate([zp[:, meta, w_a + o * kv_w:w_a + (o + 1) * kv_w],
                          zp[:, last, w_a + o * kv_w:w_a + (o + 1) * kv_w]], 1) for o in range(2)],
        axis=1).reshape(n_batch, 2, N_META + WINDOW, kv_w // DH_A, DH_A)
    kv_s = kv_s.reshape(n_dec, 2, N_META + WINDOW, kv_w // DH_A, DH_A)

    u0 = w_a + 2 * kv_w
    *ssm_abc, d_skip = ssm_p
    terms = ssm_terms(*ssm_abc, SSM_CHUNK)
    y_glu_p, h_p = ssm_glu_prompt(z, terms, d_skip, n_batch=n_batch, tp=tp, padf=padf, u_col0=u0, w_b=w_b)
    y_glu_p = y_glu_p.reshape(mp, w_b)
    gl = BLOCK // P_B
    h_p = h_p.reshape(g_b // gl, n_batch, 2, gl, N_B).transpose(1, 0, 3, 4, 2).reshape(n_batch, g_b, N_B, 2)

    u_s = z[mp:, u0:u0 + w_b].reshape(n_dec, t_new, g_b, P_B).transpose(2, 0, 1, 3).reshape(g_b, n_dec, t_new * P_B)
    s0 = h0.astype(F32).transpose(1, 0, 3, 2).reshape(g_b, n_dec, 2 * N_B)
    y_s, h_s = ssm_sample(u_s, s0, ssm_sub_matrices(terms, d_skip, t_new))
    y_s = y_s.reshape(g_b, n_dec, t_new, P_B).transpose(1, 2, 0, 3).reshape(ms, w_b)
    h_s = h_s.reshape(g_b, n_dec, 2, N_B).transpose(1, 0, 3, 2)

    y_glu = jnp.concatenate([y_glu_p, glu(y_s, z, row0=mp, gate_col0=u0 + w_b)], 0)
    attn = jnp.concatenate([attn_p, attn_s], 0)
    mix = matmul([(attn, 0), (y_glu, 0)], w_out_all, kp=w_a, m_rows=m, tn=512, out_dtype=F32, layer=layer)
    return mix, kv_p, kv_s, h_p, h_s


def odd_layer(x, xb, dims, tables, layer, pool_buf, cache_all, page_table, w_in, pool_w, pool_scale,
              q_norm, w_uq, kv_norm, w_uk, w_uv, w_out_all):
    n_batch, tp, padf, t_real, n_dec, t_new, past_len = dims
    cs_rows, cos_p, sin_p, cos_s, sin_s = tables
    m, d = x.shape
    mp = n_batch * tp
    ms = n_dec * t_new
    c_width = pool_scale.shape[0]
    q_lora = q_norm.shape[0]
    n_heads = w_uq.shape[1]
    kr0 = c_width + q_lora + KV_LORA
    w_in_x = jnp.concatenate([w_in, _rotate_half_cols(w_in[:, kr0:])], 1)
    z = matmul([(xb, 0)], w_in_x, kp=d, m_rows=m, tn=w_in_x.shape[1] // 3, out_dtype=F32)

    y_pool_p = pool_prompt(z, pool_w, pool_scale, n_batch=n_batch, tp=tp, padf=padf, c_width=c_width)
    buf16 = jnp.pad(pool_buf.astype(F32), ((0, 0), (1, 0), (0, 0)))
    y_pool_s = pool_sample(z, buf16, pool_w, pool_scale, row0=mp, n_dec=n_dec, t_new=t_new, c_width=c_width)
    zp = z[:mp].reshape(n_batch, tp, -1)
    new_pool_p = zp[:, tp - (POOL_MAX - 1):, :c_width]
    u_s = z[mp:, :c_width].reshape(n_dec, t_new, c_width)
    new_pool_s = jnp.concatenate([pool_buf.astype(F32), u_s], 1)[:, -(POOL_MAX - 1):]

    cqn, rows, ckvb, krb = mla_prep(z, cs_rows, q_norm, kv_norm, c_width=c_width, q_lora=q_lora)
    w_q = jnp.concatenate([w_uq[..., :NOPE].reshape(q_lora, n_heads * NOPE),
                           w_uq[..., NOPE:].reshape(q_lora, n_heads * ROPE_DIM),
                           _rotate_half_cols(w_uq[..., NOPE:]).reshape(q_lora, n_heads * ROPE_DIM)], 1)
    qfull = matmul([(cqn, 0)], w_q, kp=q_lora, m_rows=m, tn=1024, out_dtype=F32)
    w_kv = jnp.concatenate([w_uk.reshape(KV_LORA, n_heads * NOPE), w_uv.reshape(KV_LORA, n_heads * V_DIM)], 1)
    kv = matmul([(ckvb, 0)], w_kv, kp=KV_LORA, m_rows=mp, tn=1024, out_dtype=BF16)
    y_mla_p = mla_prompt(qfull, cos_p, sin_p, kv, krb, n_batch=n_batch, tp=tp, padf=padf, n_heads=n_heads)

    q_lat = head_proj(qfull, w_uk.transpose(1, 2, 0), row0=mp, m_rows=ms, k_head=NOPE, n_head=KV_LORA)
    nope_w = n_heads * NOPE
    rope_w = n_heads * ROPE_DIM
    q_r = qfull[mp:, nope_w:nope_w + rope_w].reshape(ms * n_heads, ROPE_DIM)
    q_s = qfull[mp:, nope_w + rope_w:].reshape(ms * n_heads, ROPE_DIM)
    o_lat = mla_sample(page_table, q_lat.reshape(ms * n_heads, KV_LORA), q_r, q_s, cos_s, sin_s, rows,
                       cache_all, layer, row0=mp, n_dec=n_dec, t_new=t_new, n_heads=n_heads)
    y_mla_s = head_proj(o_lat.reshape(ms, n_heads * KV_LORA), w_uv.transpose(1, 0, 2), row0=0, m_rows=ms,
                        k_head=KV_LORA, n_head=V_DIM)

    y_pool = jnp.concatenate([y_pool_p, y_pool_s], 0)
    y_mla = jnp.concatenate([y_mla_p, y_mla_s], 0)
    pieces = [(y_pool, 0)] + [(y_mla, cb) for cb in range(y_mla.shape[1] // c_width)]
    mix = matmul(pieces, w_out_all, kp=c_width, m_rows=m, tn=512, out_dtype=F32, layer=layer)
    rows_p = rows[:mp].reshape(n_batch, tp, -1)[:, padf:]
    rows_s = rows[mp:].reshape(n_dec, t_new, -1)
    return mix, new_pool_p, new_pool_s, rows_p, rows_s


def kernel(x_prompt, x_sample, cache_swa_kv, state_ssm, state_pool, cache_mla, page_table, meta_tokens, w_in_ab, attn_sinks, ssm_lambda_re, ssm_lambda_im, ssm_log_dt, ssm_b_re, ssm_b_im, ssm_c_re, ssm_c_im, ssm_d, w_out_ab, w_in_cd, pool_w, pool_scale, mla_q_norm, mla_w_uq, mla_kv_norm, mla_w_uk, mla_w_uv, w_out_cd, ln_mix_g, ln_mix_b, ln_ffn_g, ln_ffn_b, router_group_w, router_group_b, router_expert_w, router_expert_b, expert_w_gate, expert_w_up, expert_w_down):
    n_batch, seq, d = x_prompt.shape
    n_dec, t_new, _ = x_sample.shape
    depth = ln_mix_g.shape[0]
    past_len = page_table.shape[1] * PAGE_SIZE
    alpha = (2 * depth) ** 0.25
    t_real = N_META + seq
    padf = (-t_real) % BLOCK
    tp = t_real + padf
    assert t_real % SSM_CHUNK == 0 and padf % N_META == 0
    mp = n_batch * tp
    ms = n_dec * t_new
    dims = (n_batch, tp, padf, t_real, n_dec, t_new, past_len)

    meta = jnp.broadcast_to(meta_tokens.astype(F32)[None], (n_batch, N_META, d))
    xp = jnp.concatenate([jnp.zeros((n_batch, padf, d), F32), meta, x_prompt.astype(F32)], 1)
    x = jnp.concatenate([xp.reshape(mp, d), x_sample.astype(F32).reshape(ms, d)], 0)
    xb = x.astype(BF16)
    pos_p = jnp.arange(tp, dtype=jnp.int32) - padf
    valid = jnp.concatenate([jnp.tile(pos_p >= 0, n_batch), jnp.ones((ms,), bool)])

    pos_s = past_len + jnp.arange(t_new, dtype=jnp.int32)
    cos_p1, sin_p1 = _rope_tables(pos_p)
    cos_s1, sin_s1 = _rope_tables(pos_s)
    cs_rows = jnp.concatenate([jnp.tile(jnp.concatenate([cos_p1, sin_p1], 1), (n_batch, 1)),
                               jnp.tile(jnp.concatenate([cos_s1, sin_s1], 1), (n_dec, 1))], 0)
    n_heads_d = mla_w_uq.shape[2]
    tables = (cs_rows, jnp.tile(cos_p1, (1, 2)), jnp.tile(sin_p1, (1, 2)),
              jnp.repeat(cos_s1, n_heads_d, axis=0), jnp.repeat(sin_s1, n_heads_d, axis=0))

    cache_mla_t = jnp.swapaxes(cache_mla, 2, 3)

    swa_p, swa_s, ssm_p, ssm_s, pool_p, pool_s, mla_p, mla_s = [], [], [], [], [], [], [], []
    for l in range(depth):
        i = l // 2
        if l % 2 == 0:
            ssm_params = (ssm_lambda_re[i], ssm_lambda_im[i], ssm_log_dt[i], ssm_b_re[i], ssm_b_im[i],
                          ssm_c_re[i], ssm_c_im[i], ssm_d[i])
            mix, kv_p, kv_s, h_p, h_s = even_layer(x, xb, dims, i, cache_swa_kv, state_ssm[i], w_in_ab,
                                                   attn_sinks[i], ssm_params, w_out_ab)
            swa_p.append(kv_p)
            swa_s.append(kv_s)
            ssm_p.append(h_p)
            ssm_s.append(h_s)
        else:
            mix, np_p, np_s, rows_p, rows_s = odd_layer(
                x, xb, dims, tables, i, state_pool[i], cache_mla_t, page_table, w_in_cd[i], pool_w[i],
                pool_scale[i], mla_q_norm[i], mla_w_uq[i], mla_kv_norm[i], mla_w_uk[i], mla_w_uv[i],
                w_out_cd)
            pool_p.append(np_p)
            pool_s.append(np_s)
            mla_p.append(rows_p)
            mla_s.append(rows_s)
        x, xb = add_layer_norm(x, mix, ln_mix_g[l], ln_mix_b[l], alpha)
        x, xb = moe_layer(x, xb, valid, router_group_w[l], router_group_b[l], router_expert_w[l],
                          router_expert_b[l], expert_w_gate, expert_w_up, expert_w_down, l,
                          ln_ffn_g[l], ln_ffn_b[l], alpha)
    y_p = x[:mp].reshape(n_batch, tp, d)[:, padf + N_META:]
    y_s = x[mp:].reshape(n_dec, t_new, d)
    return (y_p, y_s, jnp.stack(swa_p), jnp.stack(swa_s), jnp.stack(ssm_p), jnp.stack(ssm_s),
            jnp.stack(pool_p), jnp.stack(pool_s), jnp.stack(mla_p), jnp.stack(mla_s))
```

```python
import functools
import math

import jax
import jax.numpy as jnp
from jax import lax
from jax.experimental import pallas as pl
from jax.experimental.pallas import tpu as pltpu

F32 = jnp.float32
BF16 = jnp.bfloat16

N_META = 16
BLOCK = 128
SUBLANES = 8
DH_A = 64
WINDOW = 128
P_B = 16
N_B = 64
POOL_WINDOWS = (2, 4, 8, 16)
POOL_MAX = 16
KV_LORA = 512
NOPE = 128
ROPE_DIM = 64
V_DIM = 128
ROPE_THETA = 10000.0
MLA_SCALE = (NOPE + ROPE_DIM) ** -0.5
PAGE_SIZE = 128
N_GROUPS_E = 4
EPG = 4
N_EXPERTS = N_GROUPS_E * EPG
TOP_K = 2
NEG = -1e30
EPS = 1e-5
SSM_CHUNK = 16
MOE_BLOCK = 256
PAGES_PER_STEP = 32
MLA_SAMPLE_CHAINS = 1
VMEM_LIMIT = 56 * 1024 * 1024
MATMUL_VMEM_BUDGET = 50 * 1024 * 1024


def _params(n_axes):
    return pltpu.CompilerParams(dimension_semantics=("arbitrary",) * n_axes,
                                vmem_limit_bytes=VMEM_LIMIT)


def _pick(n, cands):
    for c in cands:
        if n % c == 0:
            return c
    raise ValueError(f"no tile in {cands} divides {n}")


def _dot(a, b):
    return jnp.dot(a, b, preferred_element_type=F32)


def _dot_nt(a, b):
    return lax.dot_general(a, b, (((1,), (1,)), ((), ())), preferred_element_type=F32)


def _split(a):
    hi = a.astype(BF16)
    lo = (a - hi.astype(F32)).astype(BF16)
    return hi, lo


def _dot3(a, b):
    ah, al = _split(a)
    bh, bl = _split(b)
    return _dot(ah, bh) + (_dot(ah, bl) + _dot(al, bh))


def _mm_kernel(*refs, n_pieces):
    a_refs = refs[:n_pieces]
    w_refs = refs[n_pieces:2 * n_pieces]
    o_ref = refs[2 * n_pieces]
    wb_refs = refs[2 * n_pieces + 1:]

    @pl.when(pl.program_id(1) == 0)
    def _():
        for w_ref, wb_ref in zip(w_refs, wb_refs):
            wb_ref[...] = w_ref[...].astype(BF16)

    acc = _dot(a_refs[0][...].astype(BF16), wb_refs[0][...])
    for a_ref, wb_ref in zip(a_refs[1:], wb_refs[1:]):
        acc = acc + _dot(a_ref[...].astype(BF16), wb_ref[...])
    o_ref[...] = acc.astype(o_ref.dtype)


def matmul(pieces, w, *, kp, m_rows, tn, out_dtype, layer=None):
    n = w.shape[-1]
    n_p = len(pieces)
    a_bytes = jnp.dtype(pieces[0][0].dtype).itemsize
    o_bytes = jnp.dtype(out_dtype).itemsize

    def vmem_bytes(t):
        return n_p * kp * (2 * t * a_bytes + tn * (2 * 4 + 2)) + 2 * t * tn * o_bytes

    tm = _pick(m_rows, [t for t in (1216, 1024, 512, 256, 128) if vmem_bytes(t) <= MATMUL_VMEM_BUDGET or t == 128])
    assert n % tn == 0 and (w.ndim == 3) == (layer is not None)
    in_specs = []
    for _, cb in pieces:
        in_specs.append(pl.BlockSpec((tm, kp), lambda j, i, cb=cb: (i, cb)))
    for p in range(n_p):
        if layer is None:
            in_specs.append(pl.BlockSpec((kp, tn), lambda j, i, p=p: (p, j)))
        else:
            in_specs.append(pl.BlockSpec((None, kp, tn), lambda j, i, p=p: (layer, p, j)))
    return pl.pallas_call(
        functools.partial(_mm_kernel, n_pieces=n_p),
        grid=(n // tn, m_rows // tm),
        in_specs=in_specs,
        out_specs=pl.BlockSpec((tm, tn), lambda j, i: (i, j)),
        out_shape=jax.ShapeDtypeStruct((m_rows, n), out_dtype),
        scratch_shapes=[pltpu.VMEM((kp, tn), BF16) for _ in range(n_p)],
        compiler_params=_params(2),
        name="matmul",
    )(*[a for a, _ in pieces], *([w] * n_p))


def _ln_core(h, g_ref, b_ref, of_ref, ob_ref):
    mu = jnp.mean(h, -1, keepdims=True)
    hc = h - mu
    var = jnp.mean(hc * hc, -1, keepdims=True)
    y = hc * lax.rsqrt(var + EPS) * g_ref[...] + b_ref[...]
    of_ref[...] = y
    ob_ref[...] = y.astype(BF16)


def _ln_kernel(x_ref, m_ref, g_ref, b_ref, of_ref, ob_ref, *, alpha):
    _ln_core(alpha * x_ref[...] + m_ref[...].astype(F32), g_ref, b_ref, of_ref, ob_ref)


def _ln_moe_kernel(x_ref, y0_ref, y1_ref, gt_ref, g_ref, b_ref, of_ref, ob_ref, *, alpha):
    gt = gt_ref[...]
    moe = gt[:, 0:1] * y0_ref[...].astype(F32) + gt[:, 1:2] * y1_ref[...].astype(F32)
    _ln_core(alpha * x_ref[...] + moe, g_ref, b_ref, of_ref, ob_ref)


def _ln_call(kern, row_inputs, small_inputs, m, d):
    tm = _pick(m, (256, 128))
    row_spec = pl.BlockSpec((tm, d), lambda i: (i, 0))
    in_specs = [pl.BlockSpec((tm, a.shape[1]), lambda i, r0=r0: (r0 // tm + i, 0)) for a, r0 in row_inputs]
    in_specs += [pl.BlockSpec(a.shape, lambda i: (0, 0)) for a in small_inputs]
    assert all(r0 % tm == 0 for _, r0 in row_inputs)
    return pl.pallas_call(
        kern, grid=(m // tm,), in_specs=in_specs,
        out_specs=[row_spec, row_spec],
        out_shape=[jax.ShapeDtypeStruct((m, d), F32), jax.ShapeDtypeStruct((m, d), BF16)],
        compiler_params=_params(1), name="add_layer_norm",
    )(*[a for a, _ in row_inputs], *small_inputs)


def add_layer_norm(x, mix, g, b, alpha):
    m, d = x.shape
    return _ln_call(functools.partial(_ln_kernel, alpha=alpha), [(x, 0), (mix, 0)],
                    [g.reshape(1, d), b.reshape(1, d)], m, d)


def moe_add_layer_norm(x, y01, gates, g, b, alpha):
    m, d = x.shape
    return _ln_call(functools.partial(_ln_moe_kernel, alpha=alpha), [(x, 0), (y01, 0), (y01, m), (gates, 0)],
                    [g.reshape(1, d), b.reshape(1, d)], m, d)


def _alibi_slope(h, n_heads):
    return 2.0 ** (-8.0 * (h + 1) / n_heads)


def _swa_prompt_kernel(sink_ref, q_ref, kc_ref, vc_ref, kp_ref, vp_ref, km_ref, vm_ref, o_ref,
                       *, padf, n_heads, kvh):
    j = pl.program_id(1)
    base = j * BLOCK - padf
    kt = 3 * BLOCK
    r = lax.broadcasted_iota(jnp.int32, (BLOCK, kt), 0)
    c = lax.broadcasted_iota(jnp.int32, (BLOCK, kt), 1)
    is_band = c < 2 * BLOCK
    k_pos = jnp.where(is_band, base - BLOCK + c, c - 2 * BLOCK)
    dist = base + r - k_pos
    visible = (dist >= 0) & ((is_band & (k_pos >= N_META) & (dist <= WINDOW))
                             | (jnp.logical_not(is_band) & (k_pos < N_META)))
    dist_f = jnp.abs(dist).astype(F32)
    group = n_heads // kvh
    scale = DH_A ** -0.5
    fill = jnp.zeros((BLOCK - N_META, DH_A), F32)
    zeros = jnp.zeros((kt, DH_A), BF16)

    def pair_rows(ref_p, ref_c, ref_m, sl):
        one = jnp.concatenate([ref_p[:, sl], ref_c[:, sl], ref_m[:, sl], fill], 0).astype(BF16)
        return jnp.concatenate([jnp.concatenate([one, zeros], 1), jnp.concatenate([zeros, one], 1)], 0)

    for h in range(kvh):
        sl = slice(h * DH_A, (h + 1) * DH_A)
        k_pair = pair_rows(kp_ref, kc_ref, km_ref, sl)
        v_pair = pair_rows(vp_ref, vc_ref, vm_ref, sl)
        for g in range(0, group, 2):
            h_a = h * group + g
            hs = slice(h_a * DH_A, (h_a + 2) * DH_A)
            s = _dot_nt(q_ref[:, hs].astype(BF16), k_pair) * scale
            probs = []
            for i, hh in enumerate((h_a, h_a + 1)):
                sink = sink_ref[hh]
                logits = jnp.where(visible, s[:, i * kt:(i + 1) * kt] - _alibi_slope(hh, n_heads) * dist_f, NEG)
                mx = jnp.maximum(jnp.max(logits, -1, keepdims=True), sink)
                e = jnp.exp(logits - mx)
                den = jnp.sum(e, -1, keepdims=True) + jnp.exp(sink - mx)
                probs.append((e * (1.0 / den)).astype(BF16))
            o_ref[:, hs] = _dot(jnp.concatenate(probs, 1), v_pair).astype(o_ref.dtype)


def swa_prompt(z, sinks, *, n_batch, tp, padf, w_a, kv_w):
    nb = tp // BLOCK
    kcol = w_a // kv_w
    meta_blk = padf // N_META
    rows_meta = tp // N_META

    def cur(col):
        return pl.BlockSpec((BLOCK, kv_w), lambda b, j: (b * nb + j, col))

    def prev(col):
        return pl.BlockSpec((BLOCK, kv_w), lambda b, j: (jnp.maximum(b * nb + j - 1, 0), col))

    def meta(col):
        return pl.BlockSpec((N_META, kv_w), lambda b, j: (b * rows_meta + meta_blk, col))

    n_heads = w_a // DH_A
    return pl.pallas_call(
        functools.partial(_swa_prompt_kernel, padf=padf, n_heads=n_heads, kvh=kv_w // DH_A),
        grid=(n_batch, nb),
        in_specs=[pl.BlockSpec(memory_space=pltpu.SMEM),
                  pl.BlockSpec((BLOCK, w_a), lambda b, j: (b * nb + j, 0)),
                  cur(kcol), cur(kcol + 1), prev(kcol), prev(kcol + 1), meta(kcol), meta(kcol + 1)],
        out_specs=pl.BlockSpec((BLOCK, w_a), lambda b, j: (b * nb + j, 0)),
        out_shape=jax.ShapeDtypeStruct((z.shape[0], w_a), BF16),
        compiler_params=_params(2), name="swa_prompt",
    )(sinks, z, z, z, z, z, z, z)


def _swa_sample_kernel(sink_ref, slope_ref, q_ref, kn_ref, vn_ref, cache_ref, _into_ref, o_ref, kv_ref,
                       *, bb, t_new, past_len, n_heads, kvh):
    group = n_heads // kvh
    win = WINDOW
    n_keys = 2 * BLOCK
    n_fill = n_keys - (N_META + win + t_new)
    rows = group * t_new
    r = lax.broadcasted_iota(jnp.int32, (rows, n_keys), 0)
    c = lax.broadcasted_iota(jnp.int32, (rows, n_keys), 1)
    q_pos = past_len + r % t_new
    k_pos = jnp.where(c < N_META, c,
                      jnp.where(c < N_META + win, past_len - win + (c - N_META),
                                past_len + (c - N_META - win)))
    dist = q_pos - k_pos
    visible = (dist >= 0) & ((c < N_META) | ((k_pos >= N_META) & (dist <= WINDOW)))
    fill = jnp.zeros((n_fill, DH_A), F32)
    dist_f = jnp.abs(dist).astype(F32)
    scale = DH_A ** -0.5
    for b in range(bb):
        rs = slice(b * t_new, (b + 1) * t_new)
        for kv in range(2):
            new_ref = kn_ref if kv == 0 else vn_ref
            kv_ref[b, kv, 0:N_META, :] = cache_ref[b, kv, 0:N_META, :]
            kv_ref[b, kv, N_META:N_META + win - t_new, :] = cache_ref[b, kv, N_META + t_new:N_META + win, :]
            kv_ref[b, kv, N_META + win - t_new:N_META + win, :] = new_ref[rs, :]
        for h in range(kvh):
            sl = slice(h * DH_A, (h + 1) * DH_A)
            k_all = jnp.concatenate([cache_ref[b, 0, :, sl], kn_ref[rs, sl], fill], 0).astype(BF16)
            v_all = jnp.concatenate([cache_ref[b, 1, :, sl], vn_ref[rs, sl], fill], 0).astype(BF16)
            q = jnp.concatenate(
                [q_ref[rs, (h * group + g) * DH_A:(h * group + g + 1) * DH_A] for g in range(group)],
                0).astype(BF16)
            sink = sink_ref[h]
            logits = jnp.where(visible, _dot_nt(q, k_all) * scale - slope_ref[h] * dist_f, NEG)
            mx = jnp.maximum(jnp.max(logits, -1, keepdims=True), sink)
            e = jnp.exp(logits - mx)
            den = jnp.sum(e, -1, keepdims=True) + jnp.exp(sink - mx)
            o = _dot((e * (1.0 / den)).astype(BF16), v_all)
            for g in range(group):
                hh = h * group + g
                o_ref[rs, hh * DH_A:(hh + 1) * DH_A] = o[g * t_new:(g + 1) * t_new].astype(o_ref.dtype)


def swa_sample(z, cache_all, layer, sinks, into, *, row0, n_dec, t_new, past_len, w_a, kv_w):
    bb = _pick(n_dec, (8, 4, 2, 1))
    n_heads = w_a // DH_A
    kvh = kv_w // DH_A
    group = n_heads // kvh
    rb0 = row0 // (bb * t_new)
    assert row0 % (bb * t_new) == 0 and t_new == SUBLANES and cache_all.shape[3] == N_META + WINDOW
    kcol = w_a // kv_w
    sink_rows = jnp.repeat(sinks.astype(F32).reshape(kvh, group), t_new, axis=1)[..., None]
    slopes = jnp.asarray([_alibi_slope(h, n_heads) for h in range(n_heads)], F32)
    slope_rows = jnp.repeat(slopes.reshape(kvh, group), t_new, axis=1)[..., None]
    cache5 = cache_all.reshape(cache_all.shape[0], n_dec, 2, N_META + WINDOW, kv_w)
    small = pl.BlockSpec((kvh, group * t_new, 1), lambda i: (0, 0, 0))
    cache_spec = pl.BlockSpec((bb, 2, N_META + WINDOW, kv_w), lambda i: (i, 0, 0, 0))
    cache_in_spec = pl.BlockSpec((None, bb, 2, N_META + WINDOW, kv_w), lambda i: (layer, i, 0, 0, 0))
    return pl.pallas_call(
        functools.partial(_swa_sample_kernel, bb=bb, t_new=t_new, past_len=past_len,
                          n_heads=n_heads, kvh=kvh),
        grid=(n_dec // bb,),
        in_specs=[small, small,
                  pl.BlockSpec((bb * t_new, w_a), lambda i: (rb0 + i, 0)),
                  pl.BlockSpec((bb * t_new, kv_w), lambda i: (rb0 + i, kcol)),
                  pl.BlockSpec((bb * t_new, kv_w), lambda i: (rb0 + i, kcol + 1)),
                  cache_in_spec, pl.BlockSpec(memory_space=pl.ANY)],
        out_specs=[pl.BlockSpec((bb * t_new, w_a), lambda i: (rb0 + i, 0)), cache_spec],
        out_shape=[jax.ShapeDtypeStruct(into.shape, into.dtype),
                   jax.ShapeDtypeStruct(cache5.shape[1:], F32)],
        input_output_aliases={6: 0},
        compiler_params=_params(1), name="swa_sample",
    )(sink_rows, slope_rows, z, z, z, cache5, into)


def ssm_terms(lam_re, lam_im, log_dt, b_re, b_im, c_re, c_im, length):
    hi = lax.Precision.HIGHEST
    g_n, n_n = lam_re.shape
    lam = lax.complex(lam_re.astype(F32), lam_im.astype(F32))
    lam_dt = lam * jnp.exp(log_dt.astype(F32))[:, None]
    lam_bar = jnp.exp(lam_dt)
    b_bar = ((lam_bar - 1.0) / lam)[..., None] * lax.complex(b_re.astype(F32), b_im.astype(F32))
    cc = lax.complex(c_re.astype(F32), c_im.astype(F32))
    steps = jnp.arange(length + 1, dtype=F32)
    pw = jnp.exp(lam_dt[None] * steps[:, None, None])
    cb = cc[None] * pw[:length, :, None, :]
    k_lag = (jnp.einsum('jgpn,gnq->jgpq', cb.real, b_bar.real, precision=hi)
             - jnp.einsum('jgpn,gnq->jgpq', cb.imag, b_bar.imag, precision=hi))
    t = jnp.arange(length)
    sin_c = (pw[length - 1 - t][:, :, :, None] * b_bar[None]).transpose(1, 0, 3, 2)
    g_c = (cc[None] * pw[1:length + 1][:, :, None, :]).transpose(1, 3, 0, 2)
    return k_lag, sin_c, g_c, pw


def ssm_sub_matrices(terms, d_skip, steps):
    k_lag, sin_c, g_c, pw = terms
    g_n, n_n = pw.shape[1:]
    length = k_lag.shape[0]
    w = steps * P_B
    t = jnp.arange(steps)
    lag = t[None, :] - t[:, None]
    k_st = jnp.where((lag >= 0)[:, :, None, None, None], k_lag[jnp.clip(lag, 0)], 0.0)
    mt = k_st.transpose(2, 0, 4, 1, 3)
    eye_t = jnp.eye(steps, dtype=F32)
    eye_p = jnp.eye(P_B, dtype=F32)
    mt = mt + (eye_t[None, :, None, :, None] * eye_p[None, None, :, None, :]
               * d_skip.astype(F32).reshape(g_n, P_B)[:, None, None, None, :])
    mt = mt.reshape(g_n, w, w)
    sin_s = sin_c[:, length - steps:].reshape(g_n, w, n_n)
    sin_t = jnp.concatenate([sin_s.real, sin_s.imag], -1)
    g_s = g_c[:, :, :steps].reshape(g_n, n_n, w)
    sout_t = jnp.concatenate([g_s.real, -g_s.imag], 1)
    lam_l = pw[steps]
    lam_a = jnp.concatenate([lam_l.real, lam_l.real], -1)[:, None, :]
    lam_b = jnp.concatenate([-lam_l.imag, lam_l.imag], -1)[:, None, :]
    return mt, sin_t, sout_t, lam_a, lam_b


def ssm_block_diag(terms, d_skip, gl):
    k_lag, sin_c, g_c, pw = terms
    length, g_n = k_lag.shape[:2]
    n_n = pw.shape[2]
    sg = g_n // gl
    lanes = gl * P_B

    def block_diag(m, r, c):
        rows = lax.broadcasted_iota(jnp.int32, (gl * r, gl * c), 0) // r
        cols = lax.broadcasted_iota(jnp.int32, (gl * r, gl * c), 1) // c
        return jnp.where(rows == cols, jnp.tile(m.astype(BF16), (1, 1, 1, gl)), 0)

    bdk = block_diag(k_lag.reshape(length, sg, gl, P_B, P_B).transpose(1, 0, 2, 4, 3)
                     .reshape(sg, length, lanes, P_B), P_B, P_B)

    def expand_in(m):
        m = m.reshape(sg, gl, length, P_B, n_n).transpose(0, 2, 1, 3, 4).reshape(sg, length, lanes, n_n)
        return block_diag(m, P_B, n_n)

    def expand_out(m):
        m = m.reshape(sg, gl, n_n, length, P_B).transpose(0, 3, 1, 2, 4).reshape(sg, length, gl * n_n, P_B)
        return block_diag(m, n_n, P_B)

    bd_in = jnp.concatenate([expand_in(sin_c.real), expand_in(sin_c.imag)], -1)
    bd_out = jnp.concatenate([expand_out(g_c.real), expand_out(-g_c.imag)], 2)
    lam_l = pw[length].reshape(sg, 1, gl * n_n)
    return bdk, bd_in, bd_out, d_skip.astype(F32).reshape(sg, 1, lanes), lam_l.real, lam_l.imag


def _gelu_tanh(y):
    return 0.5 * y * (1.0 + jnp.tanh(math.sqrt(2.0 / math.pi) * (y + 0.044715 * (y * y * y))))


def _ssm_prompt_kernel(u_ref, gate_ref, bdk_ref, bdin_ref, bdout_ref, d_ref, lre_ref, lim_ref, o_ref, h_ref,
                       x_s, e_s, sp_s, y_s, *, n_seq, cps, c_first):
    length = u_ref.shape[1]
    half = lre_ref.shape[1]
    for s in range(length):
        x_s[s] = u_ref[:, s, :].astype(BF16)
    e = _dot(x_s[0], bdin_ref[0])
    for s in range(1, length):
        e = e + _dot(x_s[s], bdin_ref[s])
    e_s[...] = e
    sp_s[...] = jnp.zeros(sp_s.shape, F32)
    lre = lre_ref[...]
    lim = lim_ref[...]
    for b in range(n_seq):
        sr = jnp.zeros((1, half), F32)
        si = jnp.zeros((1, half), F32)
        for c in range(c_first, cps):
            row = slice(b * cps + c, b * cps + c + 1)
            sp_s[row, :] = jnp.concatenate([sr, si], 1)
            ec = e_s[row, :]
            sr, si = lre * sr - lim * si + ec[:, :half], lre * si + lim * sr + ec[:, half:]
        h_ref[b:b + 1, :] = jnp.concatenate([sr, si], 1)
    sp = sp_s[...].astype(BF16)
    d_skip = d_ref[...]
    for t in range(length):
        acc = _dot(sp, bdout_ref[t])
        for j in range(t + 1):
            acc = acc + _dot(x_s[t - j], bdk_ref[j])
        y_s[:, t, :] = acc + d_skip * u_ref[:, t, :]
    o_ref[...] = (_gelu_tanh(y_s[...]) * jax.nn.sigmoid(gate_ref[...])).astype(o_ref.dtype)


def ssm_glu_prompt(z, terms, d_skip, *, n_batch, tp, padf, u_col0, w_b):
    gl = BLOCK // P_B
    bdk, bd_in, bd_out, d_l, lam_re, lam_im = ssm_block_diag(terms, d_skip, gl)
    sg = bdk.shape[0]
    m = z.shape[0]
    cps = tp // SSM_CHUNK
    n_seq = _pick(n_batch, (2, 1))
    rows = n_seq * cps
    states = bd_in.shape[3]
    assert m % SSM_CHUNK == 0 and u_col0 % BLOCK == 0 and w_b == sg * BLOCK and padf % SSM_CHUNK == 0
    z3 = z.reshape(m // SSM_CHUNK, SSM_CHUNK, z.shape[1])
    ucb, gcb = u_col0 // BLOCK, (u_col0 + w_b) // BLOCK

    def rows_spec(cb0):
        return pl.BlockSpec((rows, SSM_CHUNK, BLOCK), lambda a, i: (i, 0, cb0 + a))

    def per_sg(shape):
        return pl.BlockSpec((None,) + shape, lambda a, i: (a,) + (0,) * len(shape))

    return pl.pallas_call(
        functools.partial(_ssm_prompt_kernel, n_seq=n_seq, cps=cps, c_first=padf // SSM_CHUNK),
        grid=(sg, n_batch // n_seq),
        in_specs=[rows_spec(ucb), rows_spec(gcb), per_sg((SSM_CHUNK, BLOCK, BLOCK)),
                  per_sg((SSM_CHUNK, BLOCK, states)), per_sg((SSM_CHUNK, states, BLOCK)),
                  per_sg((1, BLOCK)), per_sg((1, states // 2)), per_sg((1, states // 2))],
        out_specs=[rows_spec(0), pl.BlockSpec((None, None, n_seq, states), lambda a, i: (a, i, 0, 0))],
        out_shape=[jax.ShapeDtypeStruct((m // SSM_CHUNK, SSM_CHUNK, w_b), BF16),
                   jax.ShapeDtypeStruct((sg, n_batch // n_seq, n_seq, states), F32)],
        scratch_shapes=[pltpu.VMEM((SSM_CHUNK, rows, BLOCK), BF16), pltpu.VMEM((rows, states), F32),
                        pltpu.VMEM((rows, states), F32), pltpu.VMEM((rows, SSM_CHUNK, BLOCK), F32)],
        compiler_params=_params(2), name="ssm_glu_prompt",
    )(z3, z3, bdk, bd_in, bd_out, d_l, lam_re, lam_im)


def _ssm_sample_kernel(u_ref, s0_ref, mt_ref, sin_ref, sout_ref, la_ref, lb_ref, y_ref, h_ref):
    u = u_ref[...]
    s0 = s0_ref[...]
    y_ref[...] = _dot3(u, mt_ref[...]) + _dot3(s0, sout_ref[...])
    h_ref[...] = (la_ref[...] * s0 + lb_ref[...] * pltpu.roll(s0, N_B, 1)
                  + _dot3(u, sin_ref[...]))


def ssm_sample(u, s0, mats):
    mt, sin_t, sout_t, lam_a, lam_b = mats
    g_n, n_dec, lp = u.shape
    n2 = 2 * N_B

    def per_group(shape):
        return pl.BlockSpec((None,) + shape, lambda g: (g, 0, 0))

    return pl.pallas_call(
        _ssm_sample_kernel, grid=(g_n,),
        in_specs=[per_group((n_dec, lp)), per_group((n_dec, n2)), per_group((lp, lp)),
                  per_group((lp, n2)), per_group((n2, lp)), per_group((1, n2)), per_group((1, n2))],
        out_specs=[per_group((n_dec, lp)), per_group((n_dec, n2))],
        out_shape=[jax.ShapeDtypeStruct((g_n, n_dec, lp), F32),
                   jax.ShapeDtypeStruct((g_n, n_dec, n2), F32)],
        compiler_params=_params(1), name="ssm_sample",
    )(u, s0, mt, sin_t, sout_t, lam_a, lam_b)


def _glu_kernel(y_ref, gate_ref, _into_ref, o_ref):
    o_ref[...] = (_gelu_tanh(y_ref[...]) * jax.nn.sigmoid(gate_ref[...])).astype(o_ref.dtype)


def glu(y, z, into, *, row0, gate_col0):
    m, w = y.shape
    tm = _pick(math.gcd(m, row0) if row0 else m, (512, 256, 128))
    tc = 512
    assert gate_col0 % tc == 0 and w % tc == 0
    return pl.pallas_call(
        _glu_kernel, grid=(m // tm, w // tc),
        in_specs=[pl.BlockSpec((tm, tc), lambda i, c: (i, c)),
                  pl.BlockSpec((tm, tc), lambda i, c: (row0 // tm + i, gate_col0 // tc + c)),
                  pl.BlockSpec(memory_space=pl.ANY)],
        out_specs=pl.BlockSpec((tm, tc), lambda i, c: (row0 // tm + i, c)),
        out_shape=jax.ShapeDtypeStruct(into.shape, into.dtype),
        input_output_aliases={2: 0},
        compiler_params=_params(2), name="glu",
    )(y, z, into)


def _window_sums(ext, width):
    n = ext.shape[0]
    acc = ext
    k = 1
    while k < width:
        acc = acc[:n - 2 * k + 1] + acc[k:n - k + 1]
        k *= 2
    first = POOL_MAX - (width - 1)
    return acc[first:first + n - POOL_MAX]


def _pool_mix(ext, cur, count_fn, pw_ref, sc_ref, o_ref):
    cg = cur.shape[1] // len(POOL_WINDOWS)
    for g, width in enumerate(POOL_WINDOWS):
        sl = slice(g * cg, (g + 1) * cg)
        pooled = _window_sums(ext[:, sl], width) / count_fn(width) - cur[:, sl]
        mixed = _dot(pooled.astype(BF16), pw_ref[g].astype(BF16))
        o_ref[:, sl] = (mixed * sc_ref[:, sl]).astype(o_ref.dtype)


def _pool_prompt_kernel(cur_ref, prev_ref, pw_ref, sc_ref, o_ref, *, padf):
    j = pl.program_id(1)
    base = j * BLOCK - padf
    pos = base + lax.broadcasted_iota(jnp.int32, (BLOCK, 1), 0)
    pos_prev = base - POOL_MAX + lax.broadcasted_iota(jnp.int32, (POOL_MAX, 1), 0)
    cur = jnp.where(pos >= 0, cur_ref[...], 0.0)
    prev = jnp.where(pos_prev >= 0, prev_ref[...], 0.0)
    ext = jnp.concatenate([prev, cur], 0)

    def count(width):
        return jnp.maximum(jnp.minimum(pos + 1, width), 1).astype(F32)

    _pool_mix(ext, cur, count, pw_ref, sc_ref, o_ref)


def pool_prompt(z, pool_w, pool_scale, *, n_batch, tp, padf, c_width):
    nb = tp // BLOCK
    per = BLOCK // POOL_MAX
    return pl.pallas_call(
        functools.partial(_pool_prompt_kernel, padf=padf),
        grid=(n_batch, nb),
        in_specs=[pl.BlockSpec((BLOCK, c_width), lambda b, j: (b * nb + j, 0)),
                  pl.BlockSpec((POOL_MAX, c_width),
                               lambda b, j: (jnp.maximum((b * nb + j) * per - 1, 0), 0)),
                  pl.BlockSpec(pool_w.shape, lambda b, j: (0, 0, 0)),
                  pl.BlockSpec((1, c_width), lambda b, j: (0, 0))],
        out_specs=pl.BlockSpec((BLOCK, c_width), lambda b, j: (b * nb + j, 0)),
        out_shape=jax.ShapeDtypeStruct((z.shape[0], c_width), BF16),
        compiler_params=_params(2), name="pool_prompt",
    )(z, z, pool_w, pool_scale.reshape(1, c_width))


def _pool_sample_kernel(cur_ref, buf_ref, pw_ref, sc_ref, _into_ref, o_ref, *, bb, t_new):
    cur = cur_ref[...]
    cg = cur.shape[1] // len(POOL_WINDOWS)
    for g, width in enumerate(POOL_WINDOWS):
        sl = slice(g * cg, (g + 1) * cg)
        sums = [_window_sums(jnp.concatenate([buf_ref[b, :, sl], cur[b * t_new:(b + 1) * t_new, sl]], 0),
                             width) for b in range(bb)]
        pooled = jnp.concatenate(sums, 0) / float(width) - cur[:, sl]
        mixed = _dot(pooled.astype(BF16), pw_ref[g].astype(BF16))
        o_ref[:, sl] = (mixed * sc_ref[:, sl]).astype(o_ref.dtype)


def pool_sample(z, buf16, pool_w, pool_scale, into, *, row0, n_dec, t_new, c_width):
    bb = _pick(n_dec, (8, 4, 2, 1))
    rb0 = row0 // (bb * t_new)
    assert row0 % (bb * t_new) == 0
    return pl.pallas_call(
        functools.partial(_pool_sample_kernel, bb=bb, t_new=t_new),
        grid=(n_dec // bb,),
        in_specs=[pl.BlockSpec((bb * t_new, c_width), lambda i: (rb0 + i, 0)),
                  pl.BlockSpec((bb, POOL_MAX, c_width), lambda i: (i, 0, 0)),
                  pl.BlockSpec(pool_w.shape, lambda i: (0, 0, 0)),
                  pl.BlockSpec((1, c_width), lambda i: (0, 0)), pl.BlockSpec(memory_space=pl.ANY)],
        out_specs=pl.BlockSpec((bb * t_new, c_width), lambda i: (rb0 + i, 0)),
        out_shape=jax.ShapeDtypeStruct(into.shape, into.dtype),
        input_output_aliases={4: 0},
        compiler_params=_params(1), name="pool_sample",
    )(z, buf16, pool_w, pool_scale.reshape(1, c_width), into)


def _rms(x, g):
    return x * lax.rsqrt(jnp.mean(x * x, -1, keepdims=True) + EPS) * g


def _mla_prep_kernel(cq_ref, ckv_ref, kr_ref, cs_ref, qg_ref, kg_ref, cqn_ref, rows_ref, ckvb_ref, krb_ref):
    cqn_ref[...] = _rms(cq_ref[...], qg_ref[...]).astype(BF16)
    ckvn = _rms(ckv_ref[...], kg_ref[...])
    kr2 = kr_ref[...]
    cs = cs_ref[...]
    krr = kr2[:, :ROPE_DIM] * cs[:, :ROPE_DIM] + kr2[:, ROPE_DIM:] * cs[:, ROPE_DIM:]
    rows_ref[:, :KV_LORA] = ckvn
    rows_ref[:, KV_LORA:] = krr
    ckvb_ref[...] = ckvn.astype(BF16)
    krb_ref[...] = krr.astype(BF16)


def mla_prep(z, cs_rows, q_norm, kv_norm, *, c_width, q_lora):
    m = z.shape[0]
    tm = _pick(m, (256, 128))
    assert c_width % q_lora == 0 and (c_width + q_lora) % KV_LORA == 0
    kr_col = (c_width + q_lora + KV_LORA) // (2 * ROPE_DIM)
    row = lambda w, cb: pl.BlockSpec((tm, w), lambda i: (i, cb))
    return pl.pallas_call(
        _mla_prep_kernel, grid=(m // tm,),
        in_specs=[row(q_lora, c_width // q_lora), row(KV_LORA, (c_width + q_lora) // KV_LORA),
                  row(2 * ROPE_DIM, kr_col), row(2 * ROPE_DIM, 0),
                  pl.BlockSpec((1, q_lora), lambda i: (0, 0)), pl.BlockSpec((1, KV_LORA), lambda i: (0, 0))],
        out_specs=[row(q_lora, 0), row(KV_LORA + ROPE_DIM, 0), row(KV_LORA, 0), row(ROPE_DIM, 0)],
        out_shape=[jax.ShapeDtypeStruct((m, q_lora), BF16),
                   jax.ShapeDtypeStruct((m, KV_LORA + ROPE_DIM), F32),
                   jax.ShapeDtypeStruct((m, KV_LORA), BF16),
                   jax.ShapeDtypeStruct((m, ROPE_DIM), BF16)],
        compiler_params=_params(1), name="mla_prep",
    )(z, z, z, cs_rows, q_norm.reshape(1, q_lora), kv_norm.reshape(1, KV_LORA))


def _mla_prompt_kernel(qn_ref, qr_ref, qs_ref, cos_ref, sin_ref, kn_ref, v_ref, kr_ref, o_ref,
                       s_s, p_s, *, padf, heads, qb, exts):
    jq = pl.program_id(2)
    q_rope = (qr_ref[...] * cos_ref[...] + qs_ref[...] * sin_ref[...]).astype(BF16)
    q_nope = qn_ref[...].astype(BF16)
    rs_max = qb // 4
    rs_sum = qb // 2
    exp2_scale = MLA_SCALE * math.log2(math.e)

    def block(jv, ext):
        kr = kr_ref[0:ext, :]
        for h in range(heads):
            s_s[h, :, 0:ext] = (_dot_nt(q_nope[:, h * NOPE:(h + 1) * NOPE], kn_ref[0:ext, h * NOPE:(h + 1) * NOPE])
                                + _dot_nt(q_rope[:, h * ROPE_DIM:(h + 1) * ROPE_DIM], kr))
        for h in range(heads):
            m_rows = []
            for r0 in range(0, qb, rs_max):
                rows = slice(r0, r0 + rs_max)
                r = r0 + lax.broadcasted_iota(jnp.int32, (rs_max, BLOCK), 0)
                c = lax.broadcasted_iota(jnp.int32, (rs_max, BLOCK), 1)
                m_run = None
                for c0 in range(0, ext, BLOCK):
                    cols = slice(c0, c0 + BLOCK)
                    sc = s_s[h, rows, cols]
                    if c0 < padf or c0 + BLOCK - 1 > jv * qb + r0:
                        sc = jnp.where((c0 + c >= padf) & (c0 + c <= jv * qb + r), sc, NEG)
                        s_s[h, rows, cols] = sc
                    m_run = sc if m_run is None else jnp.maximum(m_run, sc)
                m_rows.append(jnp.max(m_run, -1, keepdims=True))
            m = jnp.concatenate(m_rows, 0)
            l_rows = []
            for r0 in range(0, qb, rs_sum):
                rows = slice(r0, r0 + rs_sum)
                l_run = None
                for c0 in range(0, ext, BLOCK):
                    cols = slice(c0, c0 + BLOCK)
                    p = jnp.exp2((s_s[h, rows, cols] - m[rows]) * exp2_scale)
                    l_run = p if l_run is None else l_run + p
                    p_s[h, rows, cols] = p.astype(BF16)
                l_rows.append(jnp.sum(l_run, -1, keepdims=True))
            o = _dot(p_s[h, :, 0:ext], v_ref[0:ext, h * V_DIM:(h + 1) * V_DIM])
            o_ref[:, h * V_DIM:(h + 1) * V_DIM] = (o / jnp.concatenate(l_rows, 0)).astype(o_ref.dtype)

    for jv, ext in enumerate(exts):
        pl.when(jq == jv)(functools.partial(block, jv, ext))


def mla_prompt(qfull, cos_t, sin_t, kv, krb, *, n_batch, tp, padf, n_heads):
    heads = 2
    n_qb = 4
    qb = tp // n_qb
    assert tp % n_qb == 0 and qb % 32 == 0
    exts = tuple(min(tp, -(-((jv + 1) * qb) // BLOCK) * BLOCK) for jv in range(n_qb))
    hp_n = n_heads // heads
    nope_w = n_heads * NOPE
    rope_blk0 = nope_w // (heads * ROPE_DIM)
    return pl.pallas_call(
        functools.partial(_mla_prompt_kernel, padf=padf, heads=heads, qb=qb, exts=exts),
        grid=(n_batch, hp_n, n_qb),
        in_specs=[pl.BlockSpec((qb, heads * NOPE), lambda b, hp, j: (b * n_qb + j, hp)),
                  pl.BlockSpec((qb, heads * ROPE_DIM), lambda b, hp, j: (b * n_qb + j, rope_blk0 + hp)),
                  pl.BlockSpec((qb, heads * ROPE_DIM),
                               lambda b, hp, j: (b * n_qb + j, rope_blk0 + hp_n + hp)),
                  pl.BlockSpec((qb, heads * ROPE_DIM), lambda b, hp, j: (j, 0)),
                  pl.BlockSpec((qb, heads * ROPE_DIM), lambda b, hp, j: (j, 0)),
                  pl.BlockSpec((tp, heads * NOPE), lambda b, hp, j: (b, hp)),
                  pl.BlockSpec((tp, heads * V_DIM), lambda b, hp, j: (b, hp_n + hp)),
                  pl.BlockSpec((tp, ROPE_DIM), lambda b, hp, j: (b, 0))],
        out_specs=pl.BlockSpec((qb, heads * V_DIM), lambda b, hp, j: (b * n_qb + j, hp)),
        out_shape=jax.ShapeDtypeStruct((qfull.shape[0], n_heads * V_DIM), BF16),
        scratch_shapes=[pltpu.VMEM((heads, qb, tp), F32), pltpu.VMEM((heads, qb, tp), BF16)],
        compiler_params=_params(3), name="mla_prompt",
    )(qfull, qfull, qfull, cos_t, sin_t, kv, kv, krb)


def _head_proj_kernel(a_ref, w_ref, *rest):
    o_ref = rest[-1]
    o_ref[...] = _dot(a_ref[...].astype(BF16), w_ref[...].astype(BF16)).astype(o_ref.dtype)


def head_proj(a, w, *, row0, m_rows, k_head, n_head, a_col0=0, into=None, into_row0=0):
    n_heads = w.shape[0]
    tm = _pick(math.gcd(math.gcd(row0, into_row0), m_rows), (512, 256, 128))
    assert a_col0 % k_head == 0
    aliased = into is not None
    out_shape = (jax.ShapeDtypeStruct(into.shape, into.dtype) if aliased
                 else jax.ShapeDtypeStruct((m_rows, n_heads * n_head), BF16))
    return pl.pallas_call(
        _head_proj_kernel, grid=(n_heads, m_rows // tm),
        in_specs=[pl.BlockSpec((tm, k_head), lambda h, i: (row0 // tm + i, a_col0 // k_head + h)),
                  pl.BlockSpec((None, k_head, n_head), lambda h, i: (h, 0, 0))]
                 + ([pl.BlockSpec(memory_space=pl.ANY)] if aliased else []),
        out_specs=pl.BlockSpec((tm, n_head), lambda h, i: (into_row0 // tm + i, h)),
        out_shape=out_shape,
        input_output_aliases={2: 0} if aliased else {},
        compiler_params=_params(2), name="head_proj",
    )(a, w, *([into] if aliased else []))


def _mla_sample_kernel(pt_ref, ql_ref, qr_ref, qs_ref, cos_ref, sin_ref, new_ref, *rest,
                       n_pages, n_chains, n_heads, n_steps):
    page_refs = rest[:n_pages]
    o_ref = rest[n_pages]
    q_s, m_s, l_s, acc_s = rest[n_pages + 1:]
    step = pl.program_id(1)
    rows = q_s.shape[0]
    exp2_scale = MLA_SCALE * math.log2(math.e)

    @pl.when(step == 0)
    def _():
        q_s[:, :KV_LORA] = ql_ref[...]
        q_s[:, KV_LORA:] = (qr_ref[...] * cos_ref[...] + qs_ref[...] * sin_ref[...]).astype(BF16)
        new = new_ref[...]
        new = jnp.concatenate([new, jnp.zeros_like(new)], 0).astype(BF16)
        t_pad = new.shape[0]
        s = _dot_nt(q_s[...], new)
        t_q = lax.broadcasted_iota(jnp.int32, (rows, t_pad), 0) // n_heads
        t_k = lax.broadcasted_iota(jnp.int32, (rows, t_pad), 1)
        s = jnp.where(t_k <= t_q, s, NEG)
        mx = jnp.max(s, -1, keepdims=True)
        p = jnp.exp2((s - mx) * exp2_scale)
        m_s[0] = mx
        l_s[0] = jnp.sum(p, -1, keepdims=True)
        acc_s[0] = _dot(p.astype(BF16), new[:, :KV_LORA])
        for ch in range(1, n_chains):
            m_s[ch] = jnp.full((rows, 1), NEG, F32)
            l_s[ch] = jnp.zeros((rows, 1), F32)
            acc_s[ch] = jnp.zeros((rows, KV_LORA), F32)

    q = q_s[...]
    per = n_pages // n_chains
    for ch in range(n_chains):
        keys_t = jnp.concatenate([ref[...].astype(BF16) for ref in page_refs[ch * per:(ch + 1) * per]], 1)
        s = _dot(q, keys_t)
        m_old = m_s[ch]
        m_new = jnp.maximum(m_old, jnp.max(s, -1, keepdims=True))
        a = jnp.exp2((m_old - m_new) * exp2_scale)
        p = jnp.exp2((s - m_new) * exp2_scale)
        m_s[ch] = m_new
        l_s[ch] = a * l_s[ch] + jnp.sum(p, -1, keepdims=True)
        acc_s[ch] = a * acc_s[ch] + _dot_nt(p.astype(BF16), keys_t[:KV_LORA])

    @pl.when(step == n_steps - 1)
    def _():
        m = m_s[0]
        for ch in range(1, n_chains):
            m = jnp.maximum(m, m_s[ch])
        l = jnp.zeros((rows, 1), F32)
        acc = jnp.zeros((rows, KV_LORA), F32)
        for ch in range(n_chains):
            w = jnp.exp2((m_s[ch] - m) * exp2_scale)
            l = l + w * l_s[ch]
            acc = acc + w * acc_s[ch]
        o_ref[...] = (acc / l).astype(o_ref.dtype)


def mla_sample(page_table, q_lat, q_r, q_s, cos_t, sin_t, rows, cache_all, layer,
               *, row0, n_dec, t_new, n_heads):
    n_pg = page_table.shape[1]
    pps = _pick(n_pg, (PAGES_PER_STEP, 16, 8, 4, 2, 1))
    n_chains = _pick(pps, (MLA_SAMPLE_CHAINS, 1))
    n_steps = n_pg // pps
    qrows = t_new * n_heads
    assert row0 % t_new == 0
    row_spec = lambda w: pl.BlockSpec((qrows, w), lambda b, s, pt: (b, 0))
    tab_spec = pl.BlockSpec((qrows, ROPE_DIM), lambda b, s, pt: (0, 0))

    def page_spec(k):
        return pl.BlockSpec((None, None, KV_LORA + ROPE_DIM, PAGE_SIZE),
                            lambda b, s, pt, k=k: (layer, pt[b * n_pg + s * pps + k], 0, 0))

    grid_spec = pltpu.PrefetchScalarGridSpec(
        num_scalar_prefetch=1, grid=(n_dec, n_steps),
        in_specs=[row_spec(KV_LORA), row_spec(ROPE_DIM), row_spec(ROPE_DIM), tab_spec, tab_spec,
                  pl.BlockSpec((t_new, KV_LORA + ROPE_DIM), lambda b, s, pt: (row0 // t_new + b, 0))]
                 + [page_spec(k) for k in range(pps)],
        out_specs=row_spec(KV_LORA),
        scratch_shapes=[pltpu.VMEM((qrows, KV_LORA + ROPE_DIM), BF16), pltpu.VMEM((n_chains, qrows, 1), F32),
                        pltpu.VMEM((n_chains, qrows, 1), F32), pltpu.VMEM((n_chains, qrows, KV_LORA), F32)])
    return pl.pallas_call(
        functools.partial(_mla_sample_kernel, n_pages=pps, n_chains=n_chains, n_heads=n_heads,
                          n_steps=n_steps),
        grid_spec=grid_spec,
        out_shape=jax.ShapeDtypeStruct((n_dec * qrows, KV_LORA), BF16),
        compiler_params=_params(2), name="mla_sample",
    )(page_table.reshape(-1), q_lat, q_r, q_s, cos_t, sin_t, rows, *([cache_all] * pps))


def _router_kernel(x_ref, w_ref, b_ref, o_ref):
    o_ref[...] = _dot3(x_ref[...], w_ref[...]) + b_ref[...]


def router_logits(x, w, b):
    m, d = x.shape
    tm = _pick(m, (256, 128))
    n = w.shape[1]
    return pl.pallas_call(
        _router_kernel, grid=(m // tm,),
        in_specs=[pl.BlockSpec((tm, d), lambda i: (i, 0)), pl.BlockSpec((d, n), lambda i: (0, 0)),
                  pl.BlockSpec((1, n), lambda i: (0, 0))],
        out_specs=pl.BlockSpec((tm, n), lambda i: (i, 0)),
        out_shape=jax.ShapeDtypeStruct((m, n), F32),
        compiler_params=_params(1), name="router",
    )(x, w, b)


def _gmm_kernel(be_ref, nu_ref, x_ref, *rest, n_w, gated):
    w_refs = rest[:n_w]
    o_ref = rest[n_w]
    wb_refs = rest[n_w + 1:]
    j = pl.program_id(1)
    active = j < nu_ref[0]
    changed = (j == 0) | (be_ref[j] != be_ref[jnp.maximum(j - 1, 0)])

    @pl.when(active & changed)
    def _():
        for w_ref, wb_ref in zip(w_refs, wb_refs):
            wb_ref[...] = w_ref[...].astype(BF16)

    @pl.when(active)
    def _():
        x = x_ref[...]
        if gated:
            gate = _dot(x, wb_refs[0][...])
            up = _dot(x, wb_refs[1][...])
            o_ref[...] = (gate * jax.nn.sigmoid(gate) * up).astype(o_ref.dtype)
        else:
            o_ref[...] = _dot(x, wb_refs[0][...]).astype(o_ref.dtype)

    @pl.when(jnp.logical_not(active))
    def _():
        o_ref[...] = jnp.zeros(o_ref.shape, o_ref.dtype)


def grouped_matmul(x, weights, layer, block_expert, n_used, *, tn, out_dtype, gated):
    n_slots, k = x.shape
    n = weights[0].shape[3]
    n_w = len(weights)
    grid_spec = pltpu.PrefetchScalarGridSpec(
        num_scalar_prefetch=2, grid=(n // tn, n_slots // MOE_BLOCK),
        in_specs=[pl.BlockSpec((MOE_BLOCK, k), lambda c, j, be, nu: (j, 0))]
                 + [pl.BlockSpec((None, None, k, tn), lambda c, j, be, nu: (layer, be[j], 0, c))
                    for _ in range(n_w)],
        out_specs=pl.BlockSpec((MOE_BLOCK, tn), lambda c, j, be, nu: (j, c)),
        scratch_shapes=[pltpu.VMEM((k, tn), BF16) for _ in range(n_w)])
    return pl.pallas_call(
        functools.partial(_gmm_kernel, n_w=n_w, gated=gated),
        grid_spec=grid_spec,
        out_shape=jax.ShapeDtypeStruct((n_slots, n), out_dtype),
        compiler_params=_params(2), name="grouped_matmul",
    )(block_expert, n_used, x, *weights)


def moe_layer(x, xb, valid, rg_w, rg_b, re_w, re_b, w_gate, w_up, w_down, layer, ln_g, ln_b, alpha):
    m, d = x.shape
    n_route = N_GROUPS_E + N_EXPERTS
    w_r = jnp.zeros((d, BLOCK), F32).at[:, :N_GROUPS_E].set(rg_w.astype(F32))
    w_r = w_r.at[:, N_GROUPS_E:n_route].set(re_w.astype(F32))
    b_r = jnp.zeros((1, BLOCK), F32).at[0, :N_GROUPS_E].set(rg_b.astype(F32))
    b_r = b_r.at[0, N_GROUPS_E:n_route].set(re_b.astype(F32))
    logits = router_logits(x, w_r, b_r)
    rows = jnp.arange(m)
    g_logits = logits[:, :N_GROUPS_E]
    g_idx = jnp.argmax(g_logits, -1).astype(jnp.int32)
    g_gate = jax.nn.softmax(g_logits, -1)[rows, g_idx][:, None]
    e_logits = logits[:, N_GROUPS_E:n_route].reshape(m, N_GROUPS_E, EPG)[rows, g_idx]
    top_logit, top_j = lax.top_k(e_logits, TOP_K)
    gates = g_gate * jax.nn.softmax(top_logit, -1)
    experts = g_idx[:, None] * EPG + top_j.astype(jnp.int32)
    experts = jnp.where(valid[:, None], experts, N_EXPERTS)

    n_assign = m * TOP_K
    e_flat = experts.reshape(n_assign)
    order = jnp.argsort(e_flat)
    e_sorted = e_flat[order]
    counts = jnp.bincount(e_flat, length=N_EXPERTS + 1)[:N_EXPERTS]
    padded = (counts + MOE_BLOCK - 1) // MOE_BLOCK * MOE_BLOCK
    pad_end = jnp.cumsum(padded)
    pad_start = pad_end - padded
    start = jnp.cumsum(counts) - counts
    n_blocks = (n_assign + N_EXPERTS * (MOE_BLOCK - 1) + MOE_BLOCK - 1) // MOE_BLOCK
    n_slots = n_blocks * MOE_BLOCK
    e_clip = jnp.minimum(e_sorted, N_EXPERTS - 1)
    dest = jnp.where(e_sorted < N_EXPERTS,
                     pad_start[e_clip] + jnp.arange(n_assign, dtype=jnp.int32) - start[e_clip],
                     n_slots).astype(jnp.int32)
    slot_tok = jnp.zeros((n_slots,), jnp.int32).at[dest].set((order // TOP_K).astype(jnp.int32), mode='drop')
    block_expert = jnp.minimum(
        jnp.searchsorted(pad_end, jnp.arange(n_blocks, dtype=jnp.int32) * MOE_BLOCK, side='right'),
        N_EXPERTS - 1).astype(jnp.int32)
    n_used = (pad_end[-1] // MOE_BLOCK).astype(jnp.int32).reshape(1)
    slot_of = jnp.zeros((n_assign,), jnp.int32).at[order].set(jnp.minimum(dest, n_slots - 1))

    xs = xb[slot_tok]
    hidden = grouped_matmul(xs, [w_gate, w_up], layer, block_expert, n_used, tn=512, out_dtype=BF16,
                            gated=True)
    y_slots = grouped_matmul(hidden, [w_down], layer, block_expert, n_used, tn=w_down.shape[3],
                             out_dtype=BF16, gated=False)
    y01 = y_slots[slot_of.reshape(m, TOP_K).T.reshape(-1)]
    return moe_add_layer_norm(x, y01, gates, ln_g, ln_b, alpha)


def _rope_tables(pos):
    half = ROPE_DIM // 2
    inv = ROPE_THETA ** (-jnp.arange(half, dtype=F32) / half)
    ang = pos.astype(F32)[:, None] * inv
    cos, sin = jnp.cos(ang), jnp.sin(ang)
    return jnp.concatenate([cos, cos], -1), jnp.concatenate([sin, sin], -1)


def _rotate_half_cols(w):
    half = ROPE_DIM // 2
    return jnp.concatenate([-w[..., half:], w[..., :half]], -1)


def even_layer(x, xb, dims, layer, kv_cache_all, h0, w_in_all, sinks, ssm_p, w_out_all):
    n_batch, tp, padf, t_real, n_dec, t_new, past_len = dims
    m, d = x.shape
    mp = n_batch * tp
    ms = n_dec * t_new
    w_b = d // 2
    w_a = w_b
    kv_w = (w_in_all.shape[2] - w_a - 2 * w_b) // 2
    g_b = w_b // P_B
    z = matmul([(xb, 0)], w_in_all, kp=d, m_rows=m, tn=512, out_dtype=F32, layer=layer)

    attn = swa_prompt(z, sinks.astype(F32), n_batch=n_batch, tp=tp, padf=padf, w_a=w_a, kv_w=kv_w)
    attn, kv_s = swa_sample(z, kv_cache_all, layer, sinks, attn, row0=mp, n_dec=n_dec, t_new=t_new,
                            past_len=past_len, w_a=w_a, kv_w=kv_w)
    kv_p = jnp.stack(
        [jnp.stack([jnp.concatenate([z[b * tp + padf:b * tp + padf + N_META, w_a + o * kv_w:w_a + (o + 1) * kv_w],
                                     z[(b + 1) * tp - WINDOW:(b + 1) * tp, w_a + o * kv_w:w_a + (o + 1) * kv_w]], 0)
                    for o in range(2)], 0) for b in range(n_batch)],
        0).reshape(n_batch, 2, N_META + WINDOW, kv_w // DH_A, DH_A)
    kv_s = kv_s.reshape(n_dec, 2, N_META + WINDOW, kv_w // DH_A, DH_A)

    u0 = w_a + 2 * kv_w
    *ssm_abc, d_skip = ssm_p
    terms = ssm_terms(*ssm_abc, SSM_CHUNK)
    y_glu, h_p = ssm_glu_prompt(z, terms, d_skip, n_batch=n_batch, tp=tp, padf=padf, u_col0=u0, w_b=w_b)
    y_glu = y_glu.reshape(m, w_b)
    gl = BLOCK // P_B
    h_p = h_p.reshape(g_b // gl, n_batch, 2, gl, N_B).transpose(1, 0, 3, 4, 2).reshape(n_batch, g_b, N_B, 2)

    u_s = z[mp:, u0:u0 + w_b].reshape(n_dec, t_new, g_b, P_B).transpose(2, 0, 1, 3).reshape(g_b, n_dec, t_new * P_B)
    s0 = h0.astype(F32).transpose(1, 0, 3, 2).reshape(g_b, n_dec, 2 * N_B)
    y_s, h_s = ssm_sample(u_s, s0, ssm_sub_matrices(terms, d_skip, t_new))
    y_s = y_s.reshape(g_b, n_dec, t_new, P_B).transpose(1, 2, 0, 3).reshape(ms, w_b)
    h_s = h_s.reshape(g_b, n_dec, 2, N_B).transpose(1, 0, 3, 2)

    y_glu = glu(y_s, z, y_glu, row0=mp, gate_col0=u0 + w_b)
    mix = matmul([(attn, 0), (y_glu, 0)], w_out_all, kp=w_a, m_rows=m, tn=512, out_dtype=BF16, layer=layer)
    return mix, kv_p, kv_s, h_p, h_s


def odd_layer(x, xb, dims, tables, layer, pool_buf, cache_all, page_table, w_in, pool_w, pool_scale,
              q_norm, w_uq, kv_norm, w_uk, w_uv, w_out_all):
    n_batch, tp, padf, t_real, n_dec, t_new, past_len = dims
    cs_rows, cos_p, sin_p, cos_s, sin_s = tables
    m, d = x.shape
    mp = n_batch * tp
    ms = n_dec * t_new
    c_width = pool_scale.shape[0]
    q_lora = q_norm.shape[0]
    n_heads = w_uq.shape[1]
    kr0 = c_width + q_lora + KV_LORA
    w_in_x = jnp.concatenate([w_in, _rotate_half_cols(w_in[:, kr0:])], 1)
    z = matmul([(xb, 0)], w_in_x, kp=d, m_rows=m, tn=w_in_x.shape[1] // 3, out_dtype=F32)

    y_pool = pool_prompt(z, pool_w, pool_scale, n_batch=n_batch, tp=tp, padf=padf, c_width=c_width)
    buf16 = jnp.pad(pool_buf.astype(F32), ((0, 0), (1, 0), (0, 0)))
    y_pool = pool_sample(z, buf16, pool_w, pool_scale, y_pool, row0=mp, n_dec=n_dec, t_new=t_new,
                         c_width=c_width)
    new_pool_p = jnp.stack([z[(b + 1) * tp - (POOL_MAX - 1):(b + 1) * tp, :c_width] for b in range(n_batch)], 0)
    u_s = z[mp:, :c_width].reshape(n_dec, t_new, c_width)
    new_pool_s = jnp.concatenate([pool_buf.astype(F32), u_s], 1)[:, -(POOL_MAX - 1):]

    cqn, rows, ckvb, krb = mla_prep(z, cs_rows, q_norm, kv_norm, c_width=c_width, q_lora=q_lora)
    w_q = jnp.concatenate([w_uq[..., :NOPE].reshape(q_lora, n_heads * NOPE),
                           w_uq[..., NOPE:].reshape(q_lora, n_heads * ROPE_DIM),
                           _rotate_half_cols(w_uq[..., NOPE:]).reshape(q_lora, n_heads * ROPE_DIM)], 1)
    qfull = matmul([(cqn, 0)], w_q, kp=q_lora, m_rows=m, tn=1024, out_dtype=F32)
    w_kv = jnp.concatenate([w_uk.reshape(KV_LORA, n_heads * NOPE), w_uv.reshape(KV_LORA, n_heads * V_DIM)], 1)
    kv = matmul([(ckvb, 0)], w_kv, kp=KV_LORA, m_rows=mp, tn=1024, out_dtype=BF16)
    y_mla = mla_prompt(qfull, cos_p, sin_p, kv, krb, n_batch=n_batch, tp=tp, padf=padf, n_heads=n_heads)

    q_lat = head_proj(qfull, w_uk.transpose(1, 2, 0), row0=mp, m_rows=ms, k_head=NOPE, n_head=KV_LORA)
    nope_w = n_heads * NOPE
    rope_w = n_heads * ROPE_DIM
    q_r = qfull[mp:, nope_w:nope_w + rope_w].reshape(ms * n_heads, ROPE_DIM)
    q_s = qfull[mp:, nope_w + rope_w:].reshape(ms * n_heads, ROPE_DIM)
    o_lat = mla_sample(page_table, q_lat.reshape(ms * n_heads, KV_LORA), q_r, q_s, cos_s, sin_s, rows,
                       cache_all, layer, row0=mp, n_dec=n_dec, t_new=t_new, n_heads=n_heads)
    y_mla = head_proj(o_lat.reshape(ms, n_heads * KV_LORA), w_uv.transpose(1, 0, 2), row0=0, m_rows=ms,
                      k_head=KV_LORA, n_head=V_DIM, into=y_mla, into_row0=mp)

    pieces =[(y_pool, 0)] + [(y_mla, cb) for cb in range(y_mla.shape[1] // c_width)]
    mix = matmul(pieces, w_out_all, kp=c_width, m_rows=m, tn=512, out_dtype=BF16, layer=layer)
    rows_p = rows[:mp].reshape(n_batch, tp, -1)[:, padf:]
    rows_s = rows[mp:].reshape(n_dec, t_new, -1)
    return mix, new_pool_p, new_pool_s, rows_p, rows_s


def kernel(x_prompt, x_sample, cache_swa_kv, state_ssm, state_pool, cache_mla, page_table, meta_tokens, w_in_ab, attn_sinks, ssm_lambda_re, ssm_lambda_im, ssm_log_dt, ssm_b_re, ssm_b_im, ssm_c_re, ssm_c_im, ssm_d, w_out_ab, w_in_cd, pool_w, pool_scale, mla_q_norm, mla_w_uq, mla_kv_norm, mla_w_uk, mla_w_uv, w_out_cd, ln_mix_g, ln_mix_b, ln_ffn_g, ln_ffn_b, router_group_w, router_group_b, router_expert_w, router_expert_b, expert_w_gate, expert_w_up, expert_w_down):
    n_batch, seq, d = x_prompt.shape
    n_dec, t_new, _ = x_sample.shape
    depth = ln_mix_g.shape[0]
    past_len = page_table.shape[1] * PAGE_SIZE
    alpha = (2 * depth) ** 0.25
    t_real = N_META + seq
    padf = (-t_real) % BLOCK
    tp = t_real + padf
    assert t_real % SSM_CHUNK == 0 and padf % N_META == 0
    mp = n_batch * tp
    ms = n_dec * t_new
    dims = (n_batch, tp, padf, t_real, n_dec, t_new, past_len)

    meta = jnp.broadcast_to(meta_tokens.astype(F32)[None], (n_batch, N_META, d))
    xp = jnp.concatenate([jnp.zeros((n_batch, padf, d), F32), meta, x_prompt.astype(F32)], 1)
    x = jnp.concatenate([xp.reshape(mp, d), x_sample.astype(F32).reshape(ms, d)], 0)
    xb = x.astype(BF16)
    pos_p = jnp.arange(tp, dtype=jnp.int32) - padf
    valid = jnp.concatenate([jnp.tile(pos_p >= 0, n_batch), jnp.ones((ms,), bool)])

    pos_s = past_len + jnp.arange(t_new, dtype=jnp.int32)
    cos_p1, sin_p1 = _rope_tables(pos_p)
    cos_s1, sin_s1 = _rope_tables(pos_s)
    cs_rows = jnp.concatenate([jnp.tile(jnp.concatenate([cos_p1, sin_p1], 1), (n_batch, 1)),
                               jnp.tile(jnp.concatenate([cos_s1, sin_s1], 1), (n_dec, 1))], 0)
    n_heads_d = mla_w_uq.shape[2]
    tables = (cs_rows, jnp.tile(cos_p1, (1, 2)), jnp.tile(sin_p1, (1, 2)),
              jnp.repeat(cos_s1, n_heads_d, axis=0), jnp.repeat(sin_s1, n_heads_d, axis=0))

    cache_mla_t = jnp.swapaxes(cache_mla, 2, 3)

    swa_p, swa_s, ssm_p, ssm_s, pool_p, pool_s, mla_p, mla_s = [], [], [], [], [], [], [], []
    for l in range(depth):
        i = l // 2
        if l % 2 == 0:
            ssm_params = (ssm_lambda_re[i], ssm_lambda_im[i], ssm_log_dt[i], ssm_b_re[i], ssm_b_im[i],
                          ssm_c_re[i], ssm_c_im[i], ssm_d[i])
            mix, kv_p, kv_s, h_p, h_s = even_layer(x, xb, dims, i, cache_swa_kv, state_ssm[i], w_in_ab,
                                                   attn_sinks[i], ssm_params, w_out_ab)
            swa_p.append(kv_p)
            swa_s.append(kv_s)
            ssm_p.append(h_p)
            ssm_s.append(h_s)
        else:
            mix, np_p, np_s, rows_p, rows_s = odd_layer(
                x, xb, dims, tables, i, state_pool[i], cache_mla_t, page_table, w_in_cd[i], pool_w[i],
                pool_scale[i], mla_q_norm[i], mla_w_uq[i], mla_kv_norm[i], mla_w_uk[i], mla_w_uv[i],
                w_out_cd)
            pool_p.append(np_p)
            pool_s.append(np_s)
            mla_p.append(rows_p)
            mla_s.append(rows_s)
        x, xb = add_layer_norm(x, mix, ln_mix_g[l], ln_mix_b[l], alpha)
        x, xb = moe_layer(x, xb, valid, router_group_w[l], router_group_b[l], router_expert_w[l],
                          router_expert_b[l], expert_w_gate, expert_w_up, expert_w_down, l,
                          ln_ffn_g[l], ln_ffn_b[l], alpha)
    y_p = x[:mp].reshape(n_batch, tp, d)[:, padf + N_META:]
    y_s = x[mp:].reshape(n_dec, t_new, d)
    return (y_p, y_s, jnp.stack(swa_p), jnp.stack(swa_s), jnp.stack(ssm_p), jnp.stack(ssm_s),
            jnp.stack(pool_p), jnp.stack(pool_s), jnp.stack(mla_p), jnp.stack(mla_s))
```

```python
import functools
import math

import jax
import jax.numpy as jnp
from jax import lax
from jax.experimental import pallas as pl
from jax.experimental.pallas import tpu as pltpu

F32 = jnp.float32
BF16 = jnp.bfloat16

N_META = 16
BLOCK = 128
SUBLANES = 8
DH_A = 64
WINDOW = 128
P_B = 16
N_B = 64
POOL_WINDOWS = (2, 4, 8, 16)
POOL_MAX = 16
KV_LORA = 512
NOPE = 128
ROPE_DIM = 64
V_DIM = 128
ROPE_THETA = 10000.0
MLA_SCALE = (NOPE + ROPE_DIM) ** -0.5
PAGE_SIZE = 128
N_GROUPS_E = 4
EPG = 4
N_EXPERTS = N_GROUPS_E * EPG
TOP_K = 2
NEG = -1e30
EPS = 1e-5
SSM_CHUNK = 16
MOE_BLOCK = 256
PAGES_PER_STEP = 32
MLA_SAMPLE_CHAINS = 1
VMEM_LIMIT = 56 * 1024 * 1024
MATMUL_VMEM_BUDGET = 50 * 1024 * 1024


def _params(n_axes):
    return pltpu.CompilerParams(dimension_semantics=("arbitrary",) * n_axes,
                                vmem_limit_bytes=VMEM_LIMIT)


def _pick(n, cands):
    for c in cands:
        if n % c == 0:
            return c
    raise ValueError(f"no tile in {cands} divides {n}")


def _dot(a, b):
    return jnp.dot(a, b, preferred_element_type=F32)


def _dot_nt(a, b):
    return lax.dot_general(a, b, (((1,), (1,)), ((), ())), preferred_element_type=F32)


def _split(a):
    hi = a.astype(BF16)
    lo = (a - hi.astype(F32)).astype(BF16)
    return hi, lo


def _dot3(a, b):
    ah, al = _split(a)
    bh, bl = _split(b)
    return _dot(ah, bh) + (_dot(ah, bl) + _dot(al, bh))


def _mm_kernel(*refs, n_pieces):
    a_refs = refs[:n_pieces]
    w_refs = refs[n_pieces:2 * n_pieces]
    o_ref = refs[2 * n_pieces]
    wb_refs = refs[2 * n_pieces + 1:]

    @pl.when(pl.program_id(1) == 0)
    def _():
        for w_ref, wb_ref in zip(w_refs, wb_refs):
            wb_ref[...] = w_ref[...].astype(BF16)

    acc = _dot(a_refs[0][...].astype(BF16), wb_refs[0][...])
    for a_ref, wb_ref in zip(a_refs[1:], wb_refs[1:]):
        acc = acc + _dot(a_ref[...].astype(BF16), wb_ref[...])
    o_ref[...] = acc.astype(o_ref.dtype)


def matmul(pieces, w, *, kp, m_rows, tn, out_dtype, layer=None):
    n = w.shape[-1]
    n_p = len(pieces)
    a_bytes = jnp.dtype(pieces[0][0].dtype).itemsize
    o_bytes = jnp.dtype(out_dtype).itemsize

    def vmem_bytes(t):
        return n_p * kp * (2 * t * a_bytes + tn * (2 * 4 + 2)) + 2 * t * tn * o_bytes

    tm = _pick(m_rows, [t for t in (1216, 1024, 512, 256, 128) if vmem_bytes(t) <= MATMUL_VMEM_BUDGET or t == 128])
    assert n % tn == 0 and (w.ndim == 3) == (layer is not None)
    in_specs = []
    for _, cb in pieces:
        in_specs.append(pl.BlockSpec((tm, kp), lambda j, i, cb=cb: (i, cb)))
    for p in range(n_p):
        if layer is None:
            in_specs.append(pl.BlockSpec((kp, tn), lambda j, i, p=p: (p, j)))
        else:
            in_specs.append(pl.BlockSpec((None, kp, tn), lambda j, i, p=p: (layer, p, j)))
    return pl.pallas_call(
        functools.partial(_mm_kernel, n_pieces=n_p),
        grid=(n // tn, m_rows // tm),
        in_specs=in_specs,
        out_specs=pl.BlockSpec((tm, tn), lambda j, i: (i, j)),
        out_shape=jax.ShapeDtypeStruct((m_rows, n), out_dtype),
        scratch_shapes=[pltpu.VMEM((kp, tn), BF16) for _ in range(n_p)],
        compiler_params=_params(2),
        name="matmul",
    )(*[a for a, _ in pieces], *([w] * n_p))


def _ln_core(h, g_ref, b_ref, of_ref, ob_ref):
    mu = jnp.mean(h, -1, keepdims=True)
    hc = h - mu
    var = jnp.mean(hc * hc, -1, keepdims=True)
    y = hc * lax.rsqrt(var + EPS) * g_ref[...] + b_ref[...]
    of_ref[...] = y
    ob_ref[...] = y.astype(BF16)


def _ln_kernel(x_ref, m_ref, g_ref, b_ref, of_ref, ob_ref, *, alpha):
    _ln_core(alpha * x_ref[...] + m_ref[...].astype(F32), g_ref, b_ref, of_ref, ob_ref)


def _ln_moe_kernel(x_ref, y0_ref, y1_ref, gt_ref, g_ref, b_ref, of_ref, ob_ref, *, alpha):
    gt = gt_ref[...]
    moe = gt[:, 0:1] * y0_ref[...].astype(F32) + gt[:, 1:2] * y1_ref[...].astype(F32)
    _ln_core(alpha * x_ref[...] + moe, g_ref, b_ref, of_ref, ob_ref)


def _ln_call(kern, row_inputs, small_inputs, m, d):
    tm = _pick(m, (256, 128))
    row_spec = pl.BlockSpec((tm, d), lambda i: (i, 0))
    in_specs = [pl.BlockSpec((tm, a.shape[1]), lambda i, r0=r0: (r0 // tm + i, 0)) for a, r0 in row_inputs]
    in_specs += [pl.BlockSpec(a.shape, lambda i: (0, 0)) for a in small_inputs]
    assert all(r0 % tm == 0 for _, r0 in row_inputs)
    return pl.pallas_call(
        kern, grid=(m // tm,), in_specs=in_specs,
        out_specs=[row_spec, row_spec],
        out_shape=[jax.ShapeDtypeStruct((m, d), F32), jax.ShapeDtypeStruct((m, d), BF16)],
        compiler_params=_params(1), name="add_layer_norm",
    )(*[a for a, _ in row_inputs], *small_inputs)


def add_layer_norm(x, mix, g, b, alpha):
    m, d = x.shape
    return _ln_call(functools.partial(_ln_kernel, alpha=alpha), [(x, 0), (mix, 0)],
                    [g.reshape(1, d), b.reshape(1, d)], m, d)


def moe_add_layer_norm(x, y01, gates, g, b, alpha):
    m, d = x.shape
    return _ln_call(functools.partial(_ln_moe_kernel, alpha=alpha), [(x, 0), (y01, 0), (y01, m), (gates, 0)],
                    [g.reshape(1, d), b.reshape(1, d)], m, d)


def _alibi_slope(h, n_heads):
    return 2.0 ** (-8.0 * (h + 1) / n_heads)


def _swa_prompt_kernel(sink_ref, q_ref, kc_ref, vc_ref, kp_ref, vp_ref, km_ref, vm_ref, o_ref,
                       *, padf, n_heads, kvh):
    j = pl.program_id(1)
    base = j * BLOCK - padf
    kt = 3 * BLOCK
    r = lax.broadcasted_iota(jnp.int32, (BLOCK, kt), 0)
    c = lax.broadcasted_iota(jnp.int32, (BLOCK, kt), 1)
    is_band = c < 2 * BLOCK
    k_pos = jnp.where(is_band, base - BLOCK + c, c - 2 * BLOCK)
    dist = base + r - k_pos
    visible = (dist >= 0) & ((is_band & (k_pos >= N_META) & (dist <= WINDOW))
                             | (jnp.logical_not(is_band) & (k_pos < N_META)))
    dist_f = jnp.abs(dist).astype(F32)
    group = n_heads // kvh
    scale = DH_A ** -0.5
    fill = jnp.zeros((BLOCK - N_META, DH_A), F32)
    zeros = jnp.zeros((kt, DH_A), BF16)

    def pair_rows(ref_p, ref_c, ref_m, sl):
        one = jnp.concatenate([ref_p[:, sl], ref_c[:, sl], ref_m[:, sl], fill], 0).astype(BF16)
        return jnp.concatenate([jnp.concatenate([one, zeros], 1), jnp.concatenate([zeros, one], 1)], 0)

    for h in range(kvh):
        sl = slice(h * DH_A, (h + 1) * DH_A)
        k_pair = pair_rows(kp_ref, kc_ref, km_ref, sl)
        v_pair = pair_rows(vp_ref, vc_ref, vm_ref, sl)
        for g in range(0, group, 2):
            h_a = h * group + g
            hs = slice(h_a * DH_A, (h_a + 2) * DH_A)
            s = _dot_nt(q_ref[:, hs].astype(BF16), k_pair) * scale
            probs = []
            for i, hh in enumerate((h_a, h_a + 1)):
                sink = sink_ref[hh]
                logits = jnp.where(visible, s[:, i * kt:(i + 1) * kt] - _alibi_slope(hh, n_heads) * dist_f, NEG)
                mx = jnp.maximum(jnp.max(logits, -1, keepdims=True), sink)
                e = jnp.exp(logits - mx)
                den = jnp.sum(e, -1, keepdims=True) + jnp.exp(sink - mx)
                probs.append((e * (1.0 / den)).astype(BF16))
            o_ref[:, hs] = _dot(jnp.concatenate(probs, 1), v_pair).astype(o_ref.dtype)


def swa_prompt(z, sinks, *, n_batch, tp, padf, w_a, kv_w):
    nb = tp // BLOCK
    kcol = w_a // kv_w
    meta_blk = padf // N_META
    rows_meta = tp // N_META

    def cur(col):
        return pl.BlockSpec((BLOCK, kv_w), lambda b, j: (b * nb + j, col))

    def prev(col):
        return pl.BlockSpec((BLOCK, kv_w), lambda b, j: (jnp.maximum(b * nb + j - 1, 0), col))

    def meta(col):
        return pl.BlockSpec((N_META, kv_w), lambda b, j: (b * rows_meta + meta_blk, col))

    n_heads = w_a // DH_A
    return pl.pallas_call(
        functools.partial(_swa_prompt_kernel, padf=padf, n_heads=n_heads, kvh=kv_w // DH_A),
        grid=(n_batch, nb),
        in_specs=[pl.BlockSpec(memory_space=pltpu.SMEM),
                  pl.BlockSpec((BLOCK, w_a), lambda b, j: (b * nb + j, 0)),
                  cur(kcol), cur(kcol + 1), prev(kcol), prev(kcol + 1), meta(kcol), meta(kcol + 1)],
        out_specs=pl.BlockSpec((BLOCK, w_a), lambda b, j: (b * nb + j, 0)),
        out_shape=jax.ShapeDtypeStruct((z.shape[0], w_a), BF16),
        compiler_params=_params(2), name="swa_prompt",
    )(sinks, z, z, z, z, z, z, z)


def _swa_sample_kernel(sink_ref, slope_ref, q_ref, kn_ref, vn_ref, cache_ref, _into_ref, o_ref, kv_ref,
                       *, bb, t_new, past_len, n_heads, kvh):
    group = n_heads // kvh
    win = WINDOW
    n_keys = 2 * BLOCK
    n_fill = n_keys - (N_META + win + t_new)
    rows = group * t_new
    r = lax.broadcasted_iota(jnp.int32, (rows, n_keys), 0)
    c = lax.broadcasted_iota(jnp.int32, (rows, n_keys), 1)
    q_pos = past_len + r % t_new
    k_pos = jnp.where(c < N_META, c,
                      jnp.where(c < N_META + win, past_len - win + (c - N_META),
                                past_len + (c - N_META - win)))
    dist = q_pos - k_pos
    visible = (dist >= 0) & ((c < N_META) | ((k_pos >= N_META) & (dist <= WINDOW)))
    fill = jnp.zeros((n_fill, DH_A), F32)
    dist_f = jnp.abs(dist).astype(F32)
    scale = DH_A ** -0.5
    for b in range(bb):
        rs = slice(b * t_new, (b + 1) * t_new)
        for kv in range(2):
            new_ref = kn_ref if kv == 0 else vn_ref
            kv_ref[b, kv, 0:N_META, :] = cache_ref[b, kv, 0:N_META, :]
            kv_ref[b, kv, N_META:N_META + win - t_new, :] = cache_ref[b, kv, N_META + t_new:N_META + win, :]
            kv_ref[b, kv, N_META + win - t_new:N_META + win, :] = new_ref[rs, :]
        for h in range(kvh):
            sl = slice(h * DH_A, (h + 1) * DH_A)
            k_all = jnp.concatenate([cache_ref[b, 0, :, sl], kn_ref[rs, sl], fill], 0).astype(BF16)
            v_all = jnp.concatenate([cache_ref[b, 1, :, sl], vn_ref[rs, sl], fill], 0).astype(BF16)
            q = jnp.concatenate(
                [q_ref[rs, (h * group + g) * DH_A:(h * group + g + 1) * DH_A] for g in range(group)],
                0).astype(BF16)
            sink = sink_ref[h]
            logits = jnp.where(visible, _dot_nt(q, k_all) * scale - slope_ref[h] * dist_f, NEG)
            mx = jnp.maximum(jnp.max(logits, -1, keepdims=True), sink)
            e = jnp.exp(logits - mx)
            den = jnp.sum(e, -1, keepdims=True) + jnp.exp(sink - mx)
            o = _dot((e * (1.0 / den)).astype(BF16), v_all)
            for g in range(group):
                hh = h * group + g
                o_ref[rs, hh * DH_A:(hh + 1) * DH_A] = o[g * t_new:(g + 1) * t_new].astype(o_ref.dtype)


def swa_sample(z, cache_all, layer, sinks, into, *, row0, n_dec, t_new, past_len, w_a, kv_w):
    bb = _pick(n_dec, (8, 4, 2, 1))
    n_heads = w_a // DH_A
    kvh = kv_w // DH_A
    group = n_heads // kvh
    rb0 = row0 // (bb * t_new)
    assert row0 % (bb * t_new) == 0 and t_new == SUBLANES and cache_all.shape[3] == N_META + WINDOW
    kcol = w_a // kv_w
    sink_rows = jnp.repeat(sinks.astype(F32).reshape(kvh, group), t_new, axis=1)[..., None]
    slopes = jnp.asarray([_alibi_slope(h, n_heads) for h in range(n_heads)], F32)
    slope_rows = jnp.repeat(slopes.reshape(kvh, group), t_new, axis=1)[..., None]
    cache5 = cache_all.reshape(cache_all.shape[0], n_dec, 2, N_META + WINDOW, kv_w)
    small = pl.BlockSpec((kvh, group * t_new, 1), lambda i: (0, 0, 0))
    cache_spec = pl.BlockSpec((bb, 2, N_META + WINDOW, kv_w), lambda i: (i, 0, 0, 0))
    cache_in_spec = pl.BlockSpec((None, bb, 2, N_META + WINDOW, kv_w), lambda i: (layer, i, 0, 0, 0))
    return pl.pallas_call(
        functools.partial(_swa_sample_kernel, bb=bb, t_new=t_new, past_len=past_len,
                          n_heads=n_heads, kvh=kvh),
        grid=(n_dec // bb,),
        in_specs=[small, small,
                  pl.BlockSpec((bb * t_new, w_a), lambda i: (rb0 + i, 0)),
                  pl.BlockSpec((bb * t_new, kv_w), lambda i: (rb0 + i, kcol)),
                  pl.BlockSpec((bb * t_new, kv_w), lambda i: (rb0 + i, kcol + 1)),
                  cache_in_spec, pl.BlockSpec(memory_space=pl.ANY)],
        out_specs=[pl.BlockSpec((bb * t_new, w_a), lambda i: (rb0 + i, 0)), cache_spec],
        out_shape=[jax.ShapeDtypeStruct(into.shape, into.dtype),
                   jax.ShapeDtypeStruct(cache5.shape[1:], F32)],
        input_output_aliases={6: 0},
        compiler_params=_params(1), name="swa_sample",
    )(sink_rows, slope_rows, z, z, z, cache5, into)


def ssm_terms(lam_re, lam_im, log_dt, b_re, b_im, c_re, c_im, length):
    g_n, n_n = lam_re.shape
    lam = lax.complex(lam_re.astype(F32), lam_im.astype(F32))
    lam_dt = lam * jnp.exp(log_dt.astype(F32))[:, None]
    lam_bar = jnp.exp(lam_dt)
    b_bar = ((lam_bar - 1.0) / lam)[..., None] * lax.complex(b_re.astype(F32), b_im.astype(F32))
    cc = lax.complex(c_re.astype(F32), c_im.astype(F32))
    steps = jnp.arange(length + 1, dtype=F32)
    pw = jnp.exp(lam_dt[None] * steps[:, None, None])
    cb = cc[None] * pw[:length, :, None, :]
    bt = b_bar.transpose(0, 2, 1)[None, :, None]
    k_lag = jnp.sum(cb.real[:, :, :, None, :] * bt.real - cb.imag[:, :, :, None, :] * bt.imag, axis=-1)
    t = jnp.arange(length)
    sin_c = (pw[length - 1 - t][:, :, :, None] * b_bar[None]).transpose(1, 0, 3, 2)
    g_c = (cc[None] * pw[1:length + 1][:, :, None, :]).transpose(1, 3, 0, 2)
    return k_lag, sin_c, g_c, pw


def ssm_sub_matrices(terms, d_skip, steps):
    k_lag, sin_c, g_c, pw = terms
    g_n, n_n = pw.shape[1:]
    length = k_lag.shape[0]
    w = steps * P_B
    t = jnp.arange(steps)
    lag = t[None, :] - t[:, None]
    k_st = jnp.where((lag >= 0)[:, :, None, None, None], k_lag[jnp.clip(lag, 0)], 0.0)
    mt = k_st.transpose(2, 0, 4, 1, 3)
    eye_t = jnp.eye(steps, dtype=F32)
    eye_p = jnp.eye(P_B, dtype=F32)
    mt = mt + (eye_t[None, :, None, :, None] * eye_p[None, None, :, None, :]
               * d_skip.astype(F32).reshape(g_n, P_B)[:, None, None, None, :])
    mt = mt.reshape(g_n, w, w)
    sin_s = sin_c[:, length - steps:].reshape(g_n, w, n_n)
    sin_t = jnp.concatenate([sin_s.real, sin_s.imag], -1)
    g_s = g_c[:, :, :steps].reshape(g_n, n_n, w)
    sout_t = jnp.concatenate([g_s.real, -g_s.imag], 1)
    lam_l = pw[steps]
    lam_a = jnp.concatenate([lam_l.real, lam_l.real], -1)[:, None, :]
    lam_b = jnp.concatenate([-lam_l.imag, lam_l.imag], -1)[:, None, :]
    return mt, sin_t, sout_t, lam_a, lam_b


def ssm_block_diag(terms, d_skip, gl):
    k_lag, sin_c, g_c, pw = terms
    length, g_n = k_lag.shape[:2]
    n_n = pw.shape[2]
    sg = g_n // gl
    lanes = gl * P_B

    def block_diag(m, r, c):
        rows = lax.broadcasted_iota(jnp.int32, (gl * r, gl * c), 0) // r
        cols = lax.broadcasted_iota(jnp.int32, (gl * r, gl * c), 1) // c
        return jnp.where(rows == cols, jnp.tile(m.astype(BF16), (1, 1, 1, gl)), 0)

    bdk = block_diag(k_lag.reshape(length, sg, gl, P_B, P_B).transpose(1, 0, 2, 4, 3)
                     .reshape(sg, length, lanes, P_B), P_B, P_B)

    def expand_in(m):
        m = m.reshape(sg, gl, length, P_B, n_n).transpose(0, 2, 1, 3, 4).reshape(sg, length, lanes, n_n)
        return block_diag(m, P_B, n_n)

    def expand_out(m):
        m = m.reshape(sg, gl, n_n, length, P_B).transpose(0, 3, 1, 2, 4).reshape(sg, length, gl * n_n, P_B)
        return block_diag(m, n_n, P_B)

    bd_in = jnp.concatenate([expand_in(sin_c.real), expand_in(sin_c.imag)], -1)
    bd_out = jnp.concatenate([expand_out(g_c.real), expand_out(-g_c.imag)], 2)
    lam_l = pw[length].reshape(sg, 1, gl * n_n)
    return bdk, bd_in, bd_out, d_skip.astype(F32).reshape(sg, 1, lanes), lam_l.real, lam_l.imag


def _gelu_tanh(y):
    return 0.5 * y * (1.0 + jnp.tanh(math.sqrt(2.0 / math.pi) * (y + 0.044715 * (y * y * y))))


def _ssm_prompt_kernel(u_ref, gate_ref, bdk_ref, bdin_ref, bdout_ref, d_ref, lre_ref, lim_ref, o_ref, h_ref,
                       x_s, e_s, sp_s, y_s, *, n_seq, cps, c_first):
    length = u_ref.shape[1]
    half = lre_ref.shape[1]
    for s in range(length):
        x_s[s] = u_ref[:, s, :].astype(BF16)
    e = _dot(x_s[0], bdin_ref[0])
    for s in range(1, length):
        e = e + _dot(x_s[s], bdin_ref[s])
    e_s[...] = e
    sp_s[...] = jnp.zeros(sp_s.shape, F32)
    lre = lre_ref[...]
    lim = lim_ref[...]
    for b in range(n_seq):
        sr = jnp.zeros((1, half), F32)
        si = jnp.zeros((1, half), F32)
        for c in range(c_first, cps):
            row = slice(b * cps + c, b * cps + c + 1)
            sp_s[row, :] = jnp.concatenate([sr, si], 1)
            ec = e_s[row, :]
            sr, si = lre * sr - lim * si + ec[:, :half], lre * si + lim * sr + ec[:, half:]
        h_ref[b:b + 1, :] = jnp.concatenate([sr, si], 1)
    sp = sp_s[...].astype(BF16)
    d_skip = d_ref[...]
    for t in range(length):
        acc = _dot(sp, bdout_ref[t])
        for j in range(t + 1):
            acc = acc + _dot(x_s[t - j], bdk_ref[j])
        y_s[:, t, :] = acc + d_skip * u_ref[:, t, :]
    o_ref[...] = (_gelu_tanh(y_s[...]) * jax.nn.sigmoid(gate_ref[...])).astype(o_ref.dtype)


def ssm_glu_prompt(z, terms, d_skip, *, n_batch, tp, padf, u_col0, w_b):
    gl = BLOCK // P_B
    bdk, bd_in, bd_out, d_l, lam_re, lam_im = ssm_block_diag(terms, d_skip, gl)
    sg = bdk.shape[0]
    m = z.shape[0]
    cps = tp // SSM_CHUNK
    n_seq = _pick(n_batch, (2, 1))
    rows = n_seq * cps
    states = bd_in.shape[3]
    assert m % SSM_CHUNK == 0 and u_col0 % BLOCK == 0 and w_b == sg * BLOCK and padf % SSM_CHUNK == 0
    z3 = z.reshape(m // SSM_CHUNK, SSM_CHUNK, z.shape[1])
    ucb, gcb = u_col0 // BLOCK, (u_col0 + w_b) // BLOCK

    def rows_spec(cb0):
        return pl.BlockSpec((rows, SSM_CHUNK, BLOCK), lambda a, i: (i, 0, cb0 + a))

    def per_sg(shape):
        return pl.BlockSpec((None,) + shape, lambda a, i: (a,) + (0,) * len(shape))

    return pl.pallas_call(
        functools.partial(_ssm_prompt_kernel, n_seq=n_seq, cps=cps, c_first=padf // SSM_CHUNK),
        grid=(sg, n_batch // n_seq),
        in_specs=[rows_spec(ucb), rows_spec(gcb), per_sg((SSM_CHUNK, BLOCK, BLOCK)),
                  per_sg((SSM_CHUNK, BLOCK, states)), per_sg((SSM_CHUNK, states, BLOCK)),
                  per_sg((1, BLOCK)), per_sg((1, states // 2)), per_sg((1, states // 2))],
        out_specs=[rows_spec(0), pl.BlockSpec((None, None, n_seq, states), lambda a, i: (a, i, 0, 0))],
        out_shape=[jax.ShapeDtypeStruct((m // SSM_CHUNK, SSM_CHUNK, w_b), BF16),
                   jax.ShapeDtypeStruct((sg, n_batch // n_seq, n_seq, states), F32)],
        scratch_shapes=[pltpu.VMEM((SSM_CHUNK, rows, BLOCK), BF16), pltpu.VMEM((rows, states), F32),
                        pltpu.VMEM((rows, states), F32), pltpu.VMEM((rows, SSM_CHUNK, BLOCK), F32)],
        compiler_params=_params(2), name="ssm_glu_prompt",
    )(z3, z3, bdk, bd_in, bd_out, d_l, lam_re, lam_im)


def _ssm_sample_kernel(u_ref, s0_ref, mt_ref, sin_ref, sout_ref, la_ref, lb_ref, y_ref, h_ref):
    u = u_ref[...]
    s0 = s0_ref[...]
    y_ref[...] = _dot3(u, mt_ref[...]) + _dot3(s0, sout_ref[...])
    h_ref[...] = (la_ref[...] * s0 + lb_ref[...] * pltpu.roll(s0, N_B, 1)
                  + _dot3(u, sin_ref[...]))


def ssm_sample(u, s0, mats):
    mt, sin_t, sout_t, lam_a, lam_b = mats
    g_n, n_dec, lp = u.shape
    n2 = 2 * N_B

    def per_group(shape):
        return pl.BlockSpec((None,) + shape, lambda g: (g, 0, 0))

    return pl.pallas_call(
        _ssm_sample_kernel, grid=(g_n,),
        in_specs=[per_group((n_dec, lp)), per_group((n_dec, n2)), per_group((lp, lp)),
                  per_group((lp, n2)), per_group((n2, lp)), per_group((1, n2)), per_group((1, n2))],
        out_specs=[per_group((n_dec, lp)), per_group((n_dec, n2))],
        out_shape=[jax.ShapeDtypeStruct((g_n, n_dec, lp), F32),
                   jax.ShapeDtypeStruct((g_n, n_dec, n2), F32)],
        compiler_params=_params(1), name="ssm_sample",
    )(u, s0, mt, sin_t, sout_t, lam_a, lam_b)


def _glu_kernel(y_ref, gate_ref, _into_ref, o_ref):
    o_ref[...] = (_gelu_tanh(y_ref[...]) * jax.nn.sigmoid(gate_ref[...])).astype(o_ref.dtype)


def glu(y, z, into, *, row0, gate_col0):
    m, w = y.shape
    tm = _pick(math.gcd(m, row0) if row0 else m, (512, 256, 128))
    tc = 512
    assert gate_col0 % tc == 0 and w % tc == 0
    return pl.pallas_call(
        _glu_kernel, grid=(m // tm, w // tc),
        in_specs=[pl.BlockSpec((tm, tc), lambda i, c: (i, c)),
                  pl.BlockSpec((tm, tc), lambda i, c: (row0 // tm + i, gate_col0 // tc + c)),
                  pl.BlockSpec(memory_space=pl.ANY)],
        out_specs=pl.BlockSpec((tm, tc), lambda i, c: (row0 // tm + i, c)),
        out_shape=jax.ShapeDtypeStruct(into.shape, into.dtype),
        input_output_aliases={2: 0},
        compiler_params=_params(2), name="glu",
    )(y, z, into)


def _window_sums(ext, width):
    n = ext.shape[0]
    acc = ext
    k = 1
    while k < width:
        acc = acc[:n - 2 * k + 1] + acc[k:n - k + 1]
        k *= 2
    first = POOL_MAX - (width - 1)
    return acc[first:first + n - POOL_MAX]


def _pool_mix(ext, cur, count_fn, pw_ref, sc_ref, o_ref):
    cg = cur.shape[1] // len(POOL_WINDOWS)
    for g, width in enumerate(POOL_WINDOWS):
        sl = slice(g * cg, (g + 1) * cg)
        pooled = _window_sums(ext[:, sl], width) / count_fn(width) - cur[:, sl]
        mixed = _dot(pooled.astype(BF16), pw_ref[g].astype(BF16))
        o_ref[:, sl] = (mixed * sc_ref[:, sl]).astype(o_ref.dtype)


def _pool_prompt_kernel(cur_ref, prev_ref, pw_ref, sc_ref, o_ref, *, padf):
    j = pl.program_id(1)
    base = j * BLOCK - padf
    pos = base + lax.broadcasted_iota(jnp.int32, (BLOCK, 1), 0)
    pos_prev = base - POOL_MAX + lax.broadcasted_iota(jnp.int32, (POOL_MAX, 1), 0)
    cur = jnp.where(pos >= 0, cur_ref[...], 0.0)
    prev = jnp.where(pos_prev >= 0, prev_ref[...], 0.0)
    ext = jnp.concatenate([prev, cur], 0)

    def count(width):
        return jnp.maximum(jnp.minimum(pos + 1, width), 1).astype(F32)

    _pool_mix(ext, cur, count, pw_ref, sc_ref, o_ref)


def pool_prompt(z, pool_w, pool_scale, *, n_batch, tp, padf, c_width):
    nb = tp // BLOCK
    per = BLOCK // POOL_MAX
    return pl.pallas_call(
        functools.partial(_pool_prompt_kernel, padf=padf),
        grid=(n_batch, nb),
        in_specs=[pl.BlockSpec((BLOCK, c_width), lambda b, j: (b * nb + j, 0)),
                  pl.BlockSpec((POOL_MAX, c_width),
                               lambda b, j: (jnp.maximum((b * nb + j) * per - 1, 0), 0)),
                  pl.BlockSpec(pool_w.shape, lambda b, j: (0, 0, 0)),
                  pl.BlockSpec((1, c_width), lambda b, j: (0, 0))],
        out_specs=pl.BlockSpec((BLOCK, c_width), lambda b, j: (b * nb + j, 0)),
        out_shape=jax.ShapeDtypeStruct((z.shape[0], c_width), BF16),
        compiler_params=_params(2), name="pool_prompt",
    )(z, z, pool_w, pool_scale.reshape(1, c_width))


def _pool_sample_kernel(cur_ref, buf_ref, pw_ref, sc_ref, _into_ref, o_ref, *, bb, t_new):
    cur = cur_ref[...]
    cg = cur.shape[1] // len(POOL_WINDOWS)
    for g, width in enumerate(POOL_WINDOWS):
        sl = slice(g * cg, (g + 1) * cg)
        sums = [_window_sums(jnp.concatenate([buf_ref[b, :, sl], cur[b * t_new:(b + 1) * t_new, sl]], 0),
                             width) for b in range(bb)]
        pooled = jnp.concatenate(sums, 0) / float(width) - cur[:, sl]
        mixed = _dot(pooled.astype(BF16), pw_ref[g].astype(BF16))
        o_ref[:, sl] = (mixed * sc_ref[:, sl]).astype(o_ref.dtype)


def pool_sample(z, buf16, pool_w, pool_scale, into, *, row0, n_dec, t_new, c_width):
    bb = _pick(n_dec, (8, 4, 2, 1))
    rb0 = row0 // (bb * t_new)
    assert row0 % (bb * t_new) == 0
    return pl.pallas_call(
        functools.partial(_pool_sample_kernel, bb=bb, t_new=t_new),
        grid=(n_dec // bb,),
        in_specs=[pl.BlockSpec((bb * t_new, c_width), lambda i: (rb0 + i, 0)),
                  pl.BlockSpec((bb, POOL_MAX, c_width), lambda i: (i, 0, 0)),
                  pl.BlockSpec(pool_w.shape, lambda i: (0, 0, 0)),
                  pl.BlockSpec((1, c_width), lambda i: (0, 0)), pl.BlockSpec(memory_space=pl.ANY)],
        out_specs=pl.BlockSpec((bb * t_new, c_width), lambda i: (rb0 + i, 0)),
        out_shape=jax.ShapeDtypeStruct(into.shape, into.dtype),
        input_output_aliases={4: 0},
        compiler_params=_params(1), name="pool_sample",
    )(z, buf16, pool_w, pool_scale.reshape(1, c_width), into)


def _rms(x, g):
    return x * lax.rsqrt(jnp.mean(x * x, -1, keepdims=True) + EPS) * g


def _mla_prep_kernel(cq_ref, ckv_ref, kr_ref, cs_ref, qg_ref, kg_ref, cqn_ref, rows_ref, ckvb_ref, krb_ref):
    cqn_ref[...] = _rms(cq_ref[...], qg_ref[...]).astype(BF16)
    ckvn = _rms(ckv_ref[...], kg_ref[...])
    kr2 = kr_ref[...]
    cs = cs_ref[...]
    krr = kr2[:, :ROPE_DIM] * cs[:, :ROPE_DIM] + kr2[:, ROPE_DIM:] * cs[:, ROPE_DIM:]
    rows_ref[:, :KV_LORA] = ckvn
    rows_ref[:, KV_LORA:] = krr
    ckvb_ref[...] = ckvn.astype(BF16)
    krb_ref[...] = krr.astype(BF16)


def mla_prep(z, cs_rows, q_norm, kv_norm, *, c_width, q_lora):
    m = z.shape[0]
    tm = _pick(m, (256, 128))
    assert c_width % q_lora == 0 and (c_width + q_lora) % KV_LORA == 0
    kr_col = (c_width + q_lora + KV_LORA) // (2 * ROPE_DIM)
    row = lambda w, cb: pl.BlockSpec((tm, w), lambda i: (i, cb))
    return pl.pallas_call(
        _mla_prep_kernel, grid=(m // tm,),
        in_specs=[row(q_lora, c_width // q_lora), row(KV_LORA, (c_width + q_lora) // KV_LORA),
                  row(2 * ROPE_DIM, kr_col), row(2 * ROPE_DIM, 0),
                  pl.BlockSpec((1, q_lora), lambda i: (0, 0)), pl.BlockSpec((1, KV_LORA), lambda i: (0, 0))],
        out_specs=[row(q_lora, 0), row(KV_LORA + ROPE_DIM, 0), row(KV_LORA, 0), row(ROPE_DIM, 0)],
        out_shape=[jax.ShapeDtypeStruct((m, q_lora), BF16),
                   jax.ShapeDtypeStruct((m, KV_LORA + ROPE_DIM), F32),
                   jax.ShapeDtypeStruct((m, KV_LORA), BF16),
                   jax.ShapeDtypeStruct((m, ROPE_DIM), BF16)],
        compiler_params=_params(1), name="mla_prep",
    )(z, z, z, cs_rows, q_norm.reshape(1, q_lora), kv_norm.reshape(1, KV_LORA))


def _mla_prompt_kernel(qn_ref, qr_ref, qs_ref, cos_ref, sin_ref, kn_ref, v_ref, kr_ref, o_ref,
                       s_s, p_s, *, padf, heads, qb, exts):
    jq = pl.program_id(2)
    q_rope = (qr_ref[...] * cos_ref[...] + qs_ref[...] * sin_ref[...]).astype(BF16)
    q_nope = qn_ref[...].astype(BF16)
    rs_max = qb // 4
    rs_sum = qb // 2
    exp2_scale = MLA_SCALE * math.log2(math.e)

    def block(jv, ext):
        kr = kr_ref[0:ext, :]
        for h in range(heads):
            s_s[h, :, 0:ext] = (_dot_nt(q_nope[:, h * NOPE:(h + 1) * NOPE], kn_ref[0:ext, h * NOPE:(h + 1) * NOPE])
                                + _dot_nt(q_rope[:, h * ROPE_DIM:(h + 1) * ROPE_DIM], kr))
        for h in range(heads):
            m_rows = []
            for r0 in range(0, qb, rs_max):
                rows = slice(r0, r0 + rs_max)
                r = r0 + lax.broadcasted_iota(jnp.int32, (rs_max, BLOCK), 0)
                c = lax.broadcasted_iota(jnp.int32, (rs_max, BLOCK), 1)
                m_run = None
                for c0 in range(0, ext, BLOCK):
                    cols = slice(c0, c0 + BLOCK)
                    sc = s_s[h, rows, cols]
                    if c0 < padf or c0 + BLOCK - 1 > jv * qb + r0:
                        sc = jnp.where((c0 + c >= padf) & (c0 + c <= jv * qb + r), sc, NEG)
                        s_s[h, rows, cols] = sc
                    m_run = sc if m_run is None else jnp.maximum(m_run, sc)
                m_rows.append(jnp.max(m_run, -1, keepdims=True))
            m = jnp.concatenate(m_rows, 0)
            l_rows = []
            for r0 in range(0, qb, rs_sum):
                rows = slice(r0, r0 + rs_sum)
                l_run = None
                for c0 in range(0, ext, BLOCK):
                    cols = slice(c0, c0 + BLOCK)
                    p = jnp.exp2((s_s[h, rows, cols] - m[rows]) * exp2_scale)
                    l_run = p if l_run is None else l_run + p
                    p_s[h, rows, cols] = p.astype(BF16)
                l_rows.append(jnp.sum(l_run, -1, keepdims=True))
            o = _dot(p_s[h, :, 0:ext], v_ref[0:ext, h * V_DIM:(h + 1) * V_DIM])
            o_ref[:, h * V_DIM:(h + 1) * V_DIM] = (o / jnp.concatenate(l_rows, 0)).astype(o_ref.dtype)

    for jv, ext in enumerate(exts):
        pl.when(jq == jv)(functools.partial(block, jv, ext))


def mla_prompt(qfull, cos_t, sin_t, kv, krb, *, n_batch, tp, padf, n_heads):
    heads = 2
    n_qb = 4
    qb = tp // n_qb
    assert tp % n_qb == 0 and qb % 32 == 0
    exts = tuple(min(tp, -(-((jv + 1) * qb) // BLOCK) * BLOCK) for jv in range(n_qb))
    hp_n = n_heads // heads
    nope_w = n_heads * NOPE
    rope_blk0 = nope_w // (heads * ROPE_DIM)
    return pl.pallas_call(
        functools.partial(_mla_prompt_kernel, padf=padf, heads=heads, qb=qb, exts=exts),
        grid=(n_batch, hp_n, n_qb),
        in_specs=[pl.BlockSpec((qb, heads * NOPE), lambda b, hp, j: (b * n_qb + j, hp)),
                  pl.BlockSpec((qb, heads * ROPE_DIM), lambda b, hp, j: (b * n_qb + j, rope_blk0 + hp)),
                  pl.BlockSpec((qb, heads * ROPE_DIM),
                               lambda b, hp, j: (b * n_qb + j, rope_blk0 + hp_n + hp)),
                  pl.BlockSpec((qb, heads * ROPE_DIM), lambda b, hp, j: (j, 0)),
                  pl.BlockSpec((qb, heads * ROPE_DIM), lambda b, hp, j: (j, 0)),
                  pl.BlockSpec((tp, heads * NOPE), lambda b, hp, j: (b, hp)),
                  pl.BlockSpec((tp, heads * V_DIM), lambda b, hp, j: (b, hp_n + hp)),
                  pl.BlockSpec((tp, ROPE_DIM), lambda b, hp, j: (b, 0))],
        out_specs=pl.BlockSpec((qb, heads * V_DIM), lambda b, hp, j: (b * n_qb + j, hp)),
        out_shape=jax.ShapeDtypeStruct((qfull.shape[0], n_heads * V_DIM), BF16),
        scratch_shapes=[pltpu.VMEM((heads, qb, tp), F32), pltpu.VMEM((heads, qb, tp), BF16)],
        compiler_params=_params(3), name="mla_prompt",
    )(qfull, qfull, qfull, cos_t, sin_t, kv, kv, krb)


def _head_proj_kernel(a_ref, w_ref, *rest):
    o_ref = rest[-1]
    o_ref[...] = _dot(a_ref[...].astype(BF16), w_ref[...].astype(BF16)).astype(o_ref.dtype)


def head_proj(a, w, *, row0, m_rows, k_head, n_head, a_col0=0, into=None, into_row0=0):
    n_heads = w.shape[0]
    tm = _pick(math.gcd(math.gcd(row0, into_row0), m_rows), (512, 256, 128))
    assert a_col0 % k_head == 0
    aliased = into is not None
    out_shape = (jax.ShapeDtypeStruct(into.shape, into.dtype) if aliased
                 else jax.ShapeDtypeStruct((m_rows, n_heads * n_head), BF16))
    return pl.pallas_call(
        _head_proj_kernel, grid=(n_heads, m_rows // tm),
        in_specs=[pl.BlockSpec((tm, k_head), lambda h, i: (row0 // tm + i, a_col0 // k_head + h)),
                  pl.BlockSpec((None, k_head, n_head), lambda h, i: (h, 0, 0))]
                 + ([pl.BlockSpec(memory_space=pl.ANY)] if aliased else []),
        out_specs=pl.BlockSpec((tm, n_head), lambda h, i: (into_row0 // tm + i, h)),
        out_shape=out_shape,
        input_output_aliases={2: 0} if aliased else {},
        compiler_params=_params(2), name="head_proj",
    )(a, w, *([into] if aliased else []))


def _mla_sample_kernel(pt_ref, ql_ref, qr_ref, qs_ref, cos_ref, sin_ref, new_ref, *rest,
                       n_pages, n_chains, n_heads, n_steps):
    page_refs = rest[:n_pages]
    o_ref = rest[n_pages]
    q_s, m_s, l_s, acc_s = rest[n_pages + 1:]
    step = pl.program_id(1)
    rows = q_s.shape[0]
    exp2_scale = MLA_SCALE * math.log2(math.e)

    @pl.when(step == 0)
    def _():
        q_s[:, :KV_LORA] = ql_ref[...]
        q_s[:, KV_LORA:] = (qr_ref[...] * cos_ref[...] + qs_ref[...] * sin_ref[...]).astype(BF16)
        new = new_ref[...]
        new = jnp.concatenate([new, jnp.zeros_like(new)], 0).astype(BF16)
        t_pad = new.shape[0]
        s = _dot_nt(q_s[...], new)
        t_q = lax.broadcasted_iota(jnp.int32, (rows, t_pad), 0) // n_heads
        t_k = lax.broadcasted_iota(jnp.int32, (rows, t_pad), 1)
        s = jnp.where(t_k <= t_q, s, NEG)
        mx = jnp.max(s, -1, keepdims=True)
        p = jnp.exp2((s - mx) * exp2_scale)
        m_s[0] = mx
        l_s[0] = jnp.sum(p, -1, keepdims=True)
        acc_s[0] = _dot(p.astype(BF16), new[:, :KV_LORA])
        for ch in range(1, n_chains):
            m_s[ch] = jnp.full((rows, 1), NEG, F32)
            l_s[ch] = jnp.zeros((rows, 1), F32)
            acc_s[ch] = jnp.zeros((rows, KV_LORA), F32)

    q = q_s[...]
    per = n_pages // n_chains
    for ch in range(n_chains):
        keys_t = jnp.concatenate([ref[...].astype(BF16) for ref in page_refs[ch * per:(ch + 1) * per]], 1)
        s = _dot(q, keys_t)
        m_old = m_s[ch]
        m_new = jnp.maximum(m_old, jnp.max(s, -1, keepdims=True))
        a = jnp.exp2((m_old - m_new) * exp2_scale)
        p = jnp.exp2((s - m_new) * exp2_scale)
        m_s[ch] = m_new
        l_s[ch] = a * l_s[ch] + jnp.sum(p, -1, keepdims=True)
        acc_s[ch] = a * acc_s[ch] + _dot_nt(p.astype(BF16), keys_t[:KV_LORA])

    @pl.when(step == n_steps - 1)
    def _():
        m = m_s[0]
        for ch in range(1, n_chains):
            m = jnp.maximum(m, m_s[ch])
        l = jnp.zeros((rows, 1), F32)
        acc = jnp.zeros((rows, KV_LORA), F32)
        for ch in range(n_chains):
            w = jnp.exp2((m_s[ch] - m) * exp2_scale)
            l = l + w * l_s[ch]
            acc = acc + w * acc_s[ch]
        o_ref[...] = (acc / l).astype(o_ref.dtype)


def mla_sample(page_table, q_lat, q_r, q_s, cos_t, sin_t, rows, cache_all, layer,
               *, row0, n_dec, t_new, n_heads):
    n_pg = page_table.shape[1]
    pps = _pick(n_pg, (PAGES_PER_STEP, 16, 8, 4, 2, 1))
    n_chains = _pick(pps, (MLA_SAMPLE_CHAINS, 1))
    n_steps = n_pg // pps
    qrows = t_new * n_heads
    assert row0 % t_new == 0
    row_spec = lambda w: pl.BlockSpec((qrows, w), lambda b, s, pt: (b, 0))
    tab_spec = pl.BlockSpec((qrows, ROPE_DIM), lambda b, s, pt: (0, 0))

    def page_spec(k):
        return pl.BlockSpec((None, None, KV_LORA + ROPE_DIM, PAGE_SIZE),
                            lambda b, s, pt, k=k: (layer, pt[b * n_pg + s * pps + k], 0, 0))

    grid_spec = pltpu.PrefetchScalarGridSpec(
        num_scalar_prefetch=1, grid=(n_dec, n_steps),
        in_specs=[row_spec(KV_LORA), row_spec(ROPE_DIM), row_spec(ROPE_DIM), tab_spec, tab_spec,
                  pl.BlockSpec((t_new, KV_LORA + ROPE_DIM), lambda b, s, pt: (row0 // t_new + b, 0))]
                 + [page_spec(k) for k in range(pps)],
        out_specs=row_spec(KV_LORA),
        scratch_shapes=[pltpu.VMEM((qrows, KV_LORA + ROPE_DIM), BF16), pltpu.VMEM((n_chains, qrows, 1), F32),
                        pltpu.VMEM((n_chains, qrows, 1), F32), pltpu.VMEM((n_chains, qrows, KV_LORA), F32)])
    return pl.pallas_call(
        functools.partial(_mla_sample_kernel, n_pages=pps, n_chains=n_chains, n_heads=n_heads,
                          n_steps=n_steps),
        grid_spec=grid_spec,
        out_shape=jax.ShapeDtypeStruct((n_dec * qrows, KV_LORA), BF16),
        compiler_params=_params(2), name="mla_sample",
    )(page_table.reshape(-1), q_lat, q_r, q_s, cos_t, sin_t, rows, *([cache_all] * pps))


def _router_kernel(x_ref, w_ref, b_ref, o_ref):
    o_ref[...] = _dot3(x_ref[...], w_ref[...]) + b_ref[...]


def router_logits(x, w, b):
    m, d = x.shape
    tm = _pick(m, (256, 128))
    n = w.shape[1]
    return pl.pallas_call(
        _router_kernel, grid=(m // tm,),
        in_specs=[pl.BlockSpec((tm, d), lambda i: (i, 0)), pl.BlockSpec((d, n), lambda i: (0, 0)),
                  pl.BlockSpec((1, n), lambda i: (0, 0))],
        out_specs=pl.BlockSpec((tm, n), lambda i: (i, 0)),
        out_shape=jax.ShapeDtypeStruct((m, n), F32),
        compiler_params=_params(1), name="router",
    )(x, w, b)


def _gmm_kernel(be_ref, nu_ref, x_ref, *rest, n_w, gated):
    w_refs = rest[:n_w]
    o_ref = rest[n_w]
    wb_refs = rest[n_w + 1:]
    j = pl.program_id(1)
    active = j < nu_ref[0]
    changed = (j == 0) | (be_ref[j] != be_ref[jnp.maximum(j - 1, 0)])

    @pl.when(active & changed)
    def _():
        for w_ref, wb_ref in zip(w_refs, wb_refs):
            wb_ref[...] = w_ref[...].astype(BF16)

    @pl.when(active)
    def _():
        x = x_ref[...]
        if gated:
            gate = _dot(x, wb_refs[0][...])
            up = _dot(x, wb_refs[1][...])
            o_ref[...] = (gate * jax.nn.sigmoid(gate) * up).astype(o_ref.dtype)
        else:
            o_ref[...] = _dot(x, wb_refs[0][...]).astype(o_ref.dtype)

    @pl.when(jnp.logical_not(active))
    def _():
        o_ref[...] = jnp.zeros(o_ref.shape, o_ref.dtype)


def grouped_matmul(x, weights, layer, block_expert, n_used, *, tn, out_dtype, gated):
    n_slots, k = x.shape
    n = weights[0].shape[3]
    n_w = len(weights)
    grid_spec = pltpu.PrefetchScalarGridSpec(
        num_scalar_prefetch=2, grid=(n // tn, n_slots // MOE_BLOCK),
        in_specs=[pl.BlockSpec((MOE_BLOCK, k), lambda c, j, be, nu: (j, 0))]
                 + [pl.BlockSpec((None, None, k, tn), lambda c, j, be, nu: (layer, be[j], 0, c))
                    for _ in range(n_w)],
        out_specs=pl.BlockSpec((MOE_BLOCK, tn), lambda c, j, be, nu: (j, c)),
        scratch_shapes=[pltpu.VMEM((k, tn), BF16) for _ in range(n_w)])
    return pl.pallas_call(
        functools.partial(_gmm_kernel, n_w=n_w, gated=gated),
        grid_spec=grid_spec,
        out_shape=jax.ShapeDtypeStruct((n_slots, n), out_dtype),
        compiler_params=_params(2), name="grouped_matmul",
    )(block_expert, n_used, x, *weights)


def moe_layer(x, xb, valid, rg_w, rg_b, re_w, re_b, w_gate, w_up, w_down, layer, ln_g, ln_b, alpha):
    m, d = x.shape
    n_route = N_GROUPS_E + N_EXPERTS
    w_r = jnp.zeros((d, BLOCK), F32).at[:, :N_GROUPS_E].set(rg_w.astype(F32))
    w_r = w_r.at[:, N_GROUPS_E:n_route].set(re_w.astype(F32))
    b_r = jnp.zeros((1, BLOCK), F32).at[0, :N_GROUPS_E].set(rg_b.astype(F32))
    b_r = b_r.at[0, N_GROUPS_E:n_route].set(re_b.astype(F32))
    logits = router_logits(x, w_r, b_r)
    rows = jnp.arange(m)
    g_logits = logits[:, :N_GROUPS_E]
    g_idx = jnp.argmax(g_logits, -1).astype(jnp.int32)
    g_gate = jax.nn.softmax(g_logits, -1)[rows, g_idx][:, None]
    e_logits = logits[:, N_GROUPS_E:n_route].reshape(m, N_GROUPS_E, EPG)[rows, g_idx]
    top_logit, top_j = lax.top_k(e_logits, TOP_K)
    gates = g_gate * jax.nn.softmax(top_logit, -1)
    experts = g_idx[:, None] * EPG + top_j.astype(jnp.int32)
    experts = jnp.where(valid[:, None], experts, N_EXPERTS)

    n_assign = m * TOP_K
    e_flat = experts.reshape(n_assign)
    order = jnp.argsort(e_flat)
    e_sorted = e_flat[order]
    counts = jnp.bincount(e_flat, length=N_EXPERTS + 1)[:N_EXPERTS]
    padded = (counts + MOE_BLOCK - 1) // MOE_BLOCK * MOE_BLOCK
    pad_end = jnp.cumsum(padded)
    pad_start = pad_end - padded
    start = jnp.cumsum(counts) - counts
    n_blocks = (n_assign + N_EXPERTS * (MOE_BLOCK - 1) + MOE_BLOCK - 1) // MOE_BLOCK
    n_slots = n_blocks * MOE_BLOCK
    e_clip = jnp.minimum(e_sorted, N_EXPERTS - 1)
    dest = jnp.where(e_sorted < N_EXPERTS,
                     pad_start[e_clip] + jnp.arange(n_assign, dtype=jnp.int32) - start[e_clip],
                     n_slots - 1).astype(jnp.int32)
    slot_of = dest[jnp.argsort(order)]
    block_start = jnp.arange(n_blocks, dtype=jnp.int32) * MOE_BLOCK
    block_expert = jnp.minimum(jnp.sum(pad_end[None, :] <= block_start[:, None], axis=1),
                               N_EXPERTS - 1).astype(jnp.int32)
    n_used = (pad_end[-1] // MOE_BLOCK).astype(jnp.int32).reshape(1)
    slot_e = jnp.repeat(block_expert, MOE_BLOCK)
    rank = jnp.arange(n_slots, dtype=jnp.int32) - pad_start[slot_e]
    src = jnp.clip(start[slot_e] + rank, 0, n_assign - 1)
    slot_tok = jnp.where((rank >= 0) & (rank < counts[slot_e]), order[src] // TOP_K, 0).astype(jnp.int32)

    xs = xb[slot_tok]
    hidden = grouped_matmul(xs, [w_gate, w_up], layer, block_expert, n_used, tn=512, out_dtype=BF16,
                            gated=True)
    y_slots = grouped_matmul(hidden, [w_down], layer, block_expert, n_used, tn=w_down.shape[3],
                             out_dtype=BF16, gated=False)
    y01 = y_slots[slot_of.reshape(m, TOP_K).T.reshape(-1)]
    return moe_add_layer_norm(x, y01, gates, ln_g, ln_b, alpha)


def _rope_tables(pos):
    half = ROPE_DIM // 2
    inv = ROPE_THETA ** (-jnp.arange(half, dtype=F32) / half)
    ang = pos.astype(F32)[:, None] * inv
    cos, sin = jnp.cos(ang), jnp.sin(ang)
    return jnp.concatenate([cos, cos], -1), jnp.concatenate([sin, sin], -1)


def _rotate_half_cols(w):
    half = ROPE_DIM // 2
    return jnp.concatenate([-w[..., half:], w[..., :half]], -1)


def even_layer(x, xb, dims, layer, kv_cache_all, h0, w_in_all, sinks, ssm_p, w_out_all):
    n_batch, tp, padf, t_real, n_dec, t_new, past_len = dims
    m, d = x.shape
    mp = n_batch * tp
    ms = n_dec * t_new
    w_b = d // 2
    w_a = w_b
    kv_w = (w_in_all.shape[2] - w_a - 2 * w_b) // 2
    g_b = w_b // P_B
    z = matmul([(xb, 0)], w_in_all, kp=d, m_rows=m, tn=512, out_dtype=F32, layer=layer)

    attn = swa_prompt(z, sinks.astype(F32), n_batch=n_batch, tp=tp, padf=padf, w_a=w_a, kv_w=kv_w)
    attn, kv_s = swa_sample(z, kv_cache_all, layer, sinks, attn, row0=mp, n_dec=n_dec, t_new=t_new,
                            past_len=past_len, w_a=w_a, kv_w=kv_w)
    kv_p = jnp.stack(
        [jnp.stack([jnp.concatenate([z[b * tp + padf:b * tp + padf + N_META, w_a + o * kv_w:w_a + (o + 1) * kv_w],
                                     z[(b + 1) * tp - WINDOW:(b + 1) * tp, w_a + o * kv_w:w_a + (o + 1) * kv_w]], 0)
                    for o in range(2)], 0) for b in range(n_batch)],
        0).reshape(n_batch, 2, N_META + WINDOW, kv_w // DH_A, DH_A)
    kv_s = kv_s.reshape(n_dec, 2, N_META + WINDOW, kv_w // DH_A, DH_A)

    u0 = w_a + 2 * kv_w
    *ssm_abc, d_skip = ssm_p
    terms = ssm_terms(*ssm_abc, SSM_CHUNK)
    y_glu, h_p = ssm_glu_prompt(z, terms, d_skip, n_batch=n_batch, tp=tp, padf=padf, u_col0=u0, w_b=w_b)
    y_glu = y_glu.reshape(m, w_b)
    gl = BLOCK // P_B
    h_p = h_p.reshape(g_b // gl, n_batch, 2, gl, N_B).transpose(1, 0, 3, 4, 2).reshape(n_batch, g_b, N_B, 2)

    u_s = z[mp:, u0:u0 + w_b].reshape(n_dec, t_new, g_b, P_B).transpose(2, 0, 1, 3).reshape(g_b, n_dec, t_new * P_B)
    s0 = h0.astype(F32).transpose(1, 0, 3, 2).reshape(g_b, n_dec, 2 * N_B)
    y_s, h_s = ssm_sample(u_s, s0, ssm_sub_matrices(terms, d_skip, t_new))
    y_s = y_s.reshape(g_b, n_dec, t_new, P_B).transpose(1, 2, 0, 3).reshape(ms, w_b)
    h_s = h_s.reshape(g_b, n_dec, 2, N_B).transpose(1, 0, 3, 2)

    y_glu = glu(y_s, z, y_glu, row0=mp, gate_col0=u0 + w_b)
    mix = matmul([(attn, 0), (y_glu, 0)], w_out_all, kp=w_a, m_rows=m, tn=512, out_dtype=BF16, layer=layer)
    return mix, kv_p, kv_s, h_p, h_s


def odd_layer(x, xb, dims, tables, layer, pool_buf, cache_all, page_table, w_in, pool_w, pool_scale,
              q_norm, w_uq, kv_norm, w_uk, w_uv, w_out_all):
    n_batch, tp, padf, t_real, n_dec, t_new, past_len = dims
    cs_rows, cos_p, sin_p, cos_s, sin_s = tables
    m, d = x.shape
    mp = n_batch * tp
    ms = n_dec * t_new
    c_width = pool_scale.shape[0]
    q_lora = q_norm.shape[0]
    n_heads = w_uq.shape[1]
    kr0 = c_width + q_lora + KV_LORA
    w_in_x = jnp.concatenate([w_in, _rotate_half_cols(w_in[:, kr0:])], 1)
    z = matmul([(xb, 0)], w_in_x, kp=d, m_rows=m, tn=w_in_x.shape[1] // 3, out_dtype=F32)

    y_pool = pool_prompt(z, pool_w, pool_scale, n_batch=n_batch, tp=tp, padf=padf, c_width=c_width)
    buf16 = jnp.pad(pool_buf.astype(F32), ((0, 0), (1, 0), (0, 0)))
    y_pool = pool_sample(z, buf16, pool_w, pool_scale, y_pool, row0=mp, n_dec=n_dec, t_new=t_new,
                         c_width=c_width)
    new_pool_p = jnp.stack([z[(b + 1) * tp - (POOL_MAX - 1):(b + 1) * tp, :c_width] for b in range(n_batch)], 0)
    u_s = z[mp:, :c_width].reshape(n_dec, t_new, c_width)
    new_pool_s = jnp.concatenate([pool_buf.astype(F32), u_s], 1)[:, -(POOL_MAX - 1):]

    cqn, rows, ckvb, krb = mla_prep(z, cs_rows, q_norm, kv_norm, c_width=c_width, q_lora=q_lora)
    w_q = jnp.concatenate([w_uq[..., :NOPE].reshape(q_lora, n_heads * NOPE),
                           w_uq[..., NOPE:].reshape(q_lora, n_heads * ROPE_DIM),
                           _rotate_half_cols(w_uq[..., NOPE:]).reshape(q_lora, n_heads * ROPE_DIM)], 1)
    qfull = matmul([(cqn, 0)], w_q, kp=q_lora, m_rows=m, tn=1024, out_dtype=F32)
    w_kv = jnp.concatenate([w_uk.reshape(KV_LORA, n_heads * NOPE), w_uv.reshape(KV_LORA, n_heads * V_DIM)], 1)
    kv = matmul([(ckvb, 0)], w_kv, kp=KV_LORA, m_rows=mp, tn=1024, out_dtype=BF16)
    y_mla = mla_prompt(qfull, cos_p, sin_p, kv, krb, n_batch=n_batch, tp=tp, padf=padf, n_heads=n_heads)

    q_lat = head_proj(qfull, w_uk.transpose(1, 2, 0), row0=mp, m_rows=ms, k_head=NOPE, n_head=KV_LORA)
    nope_w = n_heads * NOPE
    rope_w = n_heads * ROPE_DIM
    q_r = qfull[mp:, nope_w:nope_w + rope_w].reshape(ms * n_heads, ROPE_DIM)
    q_s = qfull[mp:, nope_w + rope_w:].reshape(ms * n_heads, ROPE_DIM)
    o_lat = mla_sample(page_table, q_lat.reshape(ms * n_heads, KV_LORA), q_r, q_s, cos_s, sin_s, rows,
                       cache_all, layer, row0=mp, n_dec=n_dec, t_new=t_new, n_heads=n_heads)
    y_mla = head_proj(o_lat.reshape(ms, n_heads * KV_LORA), w_uv.transpose(1, 0, 2), row0=0, m_rows=ms,
                      k_head=KV_LORA, n_head=V_DIM, into=y_mla, into_row0=mp)

    pieces =[(y_pool, 0)] + [(y_mla, cb) for cb in range(y_mla.shape[1] // c_width)]
    mix = matmul(pieces, w_out_all, kp=c_width, m_rows=m, tn=512, out_dtype=BF16, layer=layer)
    rows_p = rows[:mp].reshape(n_batch, tp, -1)[:, padf:]
    rows_s = rows[mp:].reshape(n_dec, t_new, -1)
    return mix, new_pool_p, new_pool_s, rows_p, rows_s


def kernel(x_prompt, x_sample, cache_swa_kv, state_ssm, state_pool, cache_mla, page_table, meta_tokens, w_in_ab, attn_sinks, ssm_lambda_re, ssm_lambda_im, ssm_log_dt, ssm_b_re, ssm_b_im, ssm_c_re, ssm_c_im, ssm_d, w_out_ab, w_in_cd, pool_w, pool_scale, mla_q_norm, mla_w_uq, mla_kv_norm, mla_w_uk, mla_w_uv, w_out_cd, ln_mix_g, ln_mix_b, ln_ffn_g, ln_ffn_b, router_group_w, router_group_b, router_expert_w, router_expert_b, expert_w_gate, expert_w_up, expert_w_down):
    n_batch, seq, d = x_prompt.shape
    n_dec, t_new, _ = x_sample.shape
    depth = ln_mix_g.shape[0]
    past_len = page_table.shape[1] * PAGE_SIZE
    alpha = (2 * depth) ** 0.25
    t_real = N_META + seq
    padf = (-t_real) % BLOCK
    tp = t_real + padf
    assert t_real % SSM_CHUNK == 0 and padf % N_META == 0
    mp = n_batch * tp
    ms = n_dec * t_new
    dims = (n_batch, tp, padf, t_real, n_dec, t_new, past_len)

    seq_rows = []
    for b in range(n_batch):
        seq_rows += [jnp.zeros((padf, d), F32), meta_tokens.astype(F32), x_prompt[b].astype(F32)]
    x = jnp.concatenate(seq_rows + [x_sample.astype(F32).reshape(ms, d)], 0)
    xb = x.astype(BF16)
    pos_p = jnp.arange(tp, dtype=jnp.int32) - padf
    valid = jnp.concatenate([jnp.tile(pos_p >= 0, n_batch), jnp.ones((ms,), bool)])

    pos_s = past_len + jnp.arange(t_new, dtype=jnp.int32)
    cos_p1, sin_p1 = _rope_tables(pos_p)
    cos_s1, sin_s1 = _rope_tables(pos_s)
    cs_rows = jnp.concatenate([jnp.tile(jnp.concatenate([cos_p1, sin_p1], 1), (n_batch, 1)),
                               jnp.tile(jnp.concatenate([cos_s1, sin_s1], 1), (n_dec, 1))], 0)
    n_heads_d = mla_w_uq.shape[2]
    tables = (cs_rows, jnp.tile(cos_p1, (1, 2)), jnp.tile(sin_p1, (1, 2)),
              jnp.repeat(cos_s1, n_heads_d, axis=0), jnp.repeat(sin_s1, n_heads_d, axis=0))

    cache_mla_t = jnp.swapaxes(cache_mla, 2, 3)

    swa_p, swa_s, ssm_p, ssm_s, pool_p, pool_s, mla_p, mla_s = [], [], [], [], [], [], [], []
    for l in range(depth):
        i = l // 2
        if l % 2 == 0:
            ssm_params = (ssm_lambda_re[i], ssm_lambda_im[i], ssm_log_dt[i], ssm_b_re[i], ssm_b_im[i],
                          ssm_c_re[i], ssm_c_im[i], ssm_d[i])
            mix, kv_p, kv_s, h_p, h_s = even_layer(x, xb, dims, i, cache_swa_kv, state_ssm[i], w_in_ab,
                                                   attn_sinks[i], ssm_params, w_out_ab)
            swa_p.append(kv_p)
            swa_s.append(kv_s)
            ssm_p.append(h_p)
            ssm_s.append(h_s)
        else:
            mix, np_p, np_s, rows_p, rows_s = odd_layer(
                x, xb, dims, tables, i, state_pool[i], cache_mla_t, page_table, w_in_cd[i], pool_w[i],
                pool_scale[i], mla_q_norm[i], mla_w_uq[i], mla_kv_norm[i], mla_w_uk[i], mla_w_uv[i],
                w_out_cd)
            pool_p.append(np_p)
            pool_s.append(np_s)
            mla_p.append(rows_p)
            mla_s.append(rows_s)
        x, xb = add_layer_norm(x, mix, ln_mix_g[l], ln_mix_b[l], alpha)
        x, xb = moe_layer(x, xb, valid, router_group_w[l], router_group_b[l], router_expert_w[l],
                          router_expert_b[l], expert_w_gate, expert_w_up, expert_w_down, l,
                          ln_ffn_g[l], ln_ffn_b[l], alpha)
    y_p = x[:mp].reshape(n_batch, tp, d)[:, padf + N_META:]
    y_s = x[mp:].reshape(n_dec, t_new, d)
    return (y_p, y_s, jnp.stack(swa_p), jnp.stack(swa_s), jnp.stack(ssm_p), jnp.stack(ssm_s),
            jnp.stack(pool_p), jnp.stack(pool_s), jnp.stack(mla_p), jnp.stack(mla_s))
```

```python
import functools
import math

import jax
import jax.numpy as jnp
from jax import lax
from jax.experimental import pallas as pl
from jax.experimental.pallas import tpu as pltpu

F32 = jnp.float32
BF16 = jnp.bfloat16

N_META = 16
BLOCK = 128
SUBLANES = 8
DH_A = 64
WINDOW = 128
P_B = 16
N_B = 64
POOL_WINDOWS = (2, 4, 8, 16)
POOL_MAX = 16
KV_LORA = 512
NOPE = 128
ROPE_DIM = 64
V_DIM = 128
ROPE_THETA = 10000.0
MLA_SCALE = (NOPE + ROPE_DIM) ** -0.5
PAGE_SIZE = 128
N_GROUPS_E = 4
EPG = 4
N_EXPERTS = N_GROUPS_E * EPG
TOP_K = 2
NEG = -1e30
EPS = 1e-5
SSM_CHUNK = 16
MOE_BLOCK = 256
PAGES_PER_STEP = 32
MLA_SAMPLE_CHAINS = 1
VMEM_LIMIT = 56 * 1024 * 1024
MATMUL_VMEM_BUDGET = 50 * 1024 * 1024


def _params(n_axes):
    return pltpu.CompilerParams(dimension_semantics=("arbitrary",) * n_axes,
                                vmem_limit_bytes=VMEM_LIMIT)


def _pick(n, cands):
    for c in cands:
        if n % c == 0:
            return c
    raise ValueError(f"no tile in {cands} divides {n}")


def _dot(a, b):
    return jnp.dot(a, b, preferred_element_type=F32)


def _dot_nt(a, b):
    return lax.dot_general(a, b, (((1,), (1,)), ((), ())), preferred_element_type=F32)


def _split(a):
    hi = a.astype(BF16)
    lo = (a - hi.astype(F32)).astype(BF16)
    return hi, lo


def _dot3(a, b):
    ah, al = _split(a)
    bh, bl = _split(b)
    return _dot(ah, bh) + (_dot(ah, bl) + _dot(al, bh))


def _mm_kernel(*refs, n_pieces):
    a_refs = refs[:n_pieces]
    w_refs = refs[n_pieces:2 * n_pieces]
    o_ref = refs[2 * n_pieces]
    wb_refs = refs[2 * n_pieces + 1:]

    @pl.when(pl.program_id(1) == 0)
    def _():
        for w_ref, wb_ref in zip(w_refs, wb_refs):
            wb_ref[...] = w_ref[...].astype(BF16)

    acc = _dot(a_refs[0][...].astype(BF16), wb_refs[0][...])
    for a_ref, wb_ref in zip(a_refs[1:], wb_refs[1:]):
        acc = acc + _dot(a_ref[...].astype(BF16), wb_ref[...])
    o_ref[...] = acc.astype(o_ref.dtype)


def matmul(pieces, w, *, kp, m_rows, tn, out_dtype, layer=None):
    n = w.shape[-1]
    n_p = len(pieces)
    a_bytes = jnp.dtype(pieces[0][0].dtype).itemsize
    o_bytes = jnp.dtype(out_dtype).itemsize

    def vmem_bytes(t):
        return n_p * kp * (2 * t * a_bytes + tn * (2 * 4 + 2)) + 2 * t * tn * o_bytes

    tm = _pick(m_rows, [t for t in (1216, 1024, 512, 256, 128) if vmem_bytes(t) <= MATMUL_VMEM_BUDGET or t == 128])
    assert n % tn == 0 and (w.ndim == 3) == (layer is not None)
    in_specs = []
    for _, cb in pieces:
        in_specs.append(pl.BlockSpec((tm, kp), lambda j, i, cb=cb: (i, cb)))
    for p in range(n_p):
        if layer is None:
            in_specs.append(pl.BlockSpec((kp, tn), lambda j, i, p=p: (p, j)))
        else:
            in_specs.append(pl.BlockSpec((None, kp, tn), lambda j, i, p=p: (layer, p, j)))
    return pl.pallas_call(
        functools.partial(_mm_kernel, n_pieces=n_p),
        grid=(n // tn, m_rows // tm),
        in_specs=in_specs,
        out_specs=pl.BlockSpec((tm, tn), lambda j, i: (i, j)),
        out_shape=jax.ShapeDtypeStruct((m_rows, n), out_dtype),
        scratch_shapes=[pltpu.VMEM((kp, tn), BF16) for _ in range(n_p)],
        compiler_params=_params(2),
        name="matmul",
    )(*[a for a, _ in pieces], *([w] * n_p))


def _ln_core(h, g_ref, b_ref, of_ref, ob_ref):
    mu = jnp.mean(h, -1, keepdims=True)
    hc = h - mu
    var = jnp.mean(hc * hc, -1, keepdims=True)
    y = hc * lax.rsqrt(var + EPS) * g_ref[...] + b_ref[...]
    of_ref[...] = y
    ob_ref[...] = y.astype(BF16)


def _ln_kernel(x_ref, m_ref, g_ref, b_ref, of_ref, ob_ref, *, alpha):
    _ln_core(alpha * x_ref[...] + m_ref[...].astype(F32), g_ref, b_ref, of_ref, ob_ref)


def _ln_moe_kernel(x_ref, y0_ref, y1_ref, gt_ref, g_ref, b_ref, of_ref, ob_ref, *, alpha):
    gt = gt_ref[...]
    moe = gt[:, 0:1] * y0_ref[...].astype(F32) + gt[:, 1:2] * y1_ref[...].astype(F32)
    _ln_core(alpha * x_ref[...] + moe, g_ref, b_ref, of_ref, ob_ref)


def _ln_call(kern, row_inputs, small_inputs, m, d):
    tm = _pick(m, (256, 128))
    row_spec = pl.BlockSpec((tm, d), lambda i: (i, 0))
    in_specs = [pl.BlockSpec((tm, a.shape[1]), lambda i, r0=r0: (r0 // tm + i, 0)) for a, r0 in row_inputs]
    in_specs += [pl.BlockSpec(a.shape, lambda i: (0, 0)) for a in small_inputs]
    assert all(r0 % tm == 0 for _, r0 in row_inputs)
    return pl.pallas_call(
        kern, grid=(m // tm,), in_specs=in_specs,
        out_specs=[row_spec, row_spec],
        out_shape=[jax.ShapeDtypeStruct((m, d), F32), jax.ShapeDtypeStruct((m, d), BF16)],
        compiler_params=_params(1), name="add_layer_norm",
    )(*[a for a, _ in row_inputs], *small_inputs)


def add_layer_norm(x, mix, g, b, alpha):
    m, d = x.shape
    return _ln_call(functools.partial(_ln_kernel, alpha=alpha), [(x, 0), (mix, 0)],
                    [g.reshape(1, d), b.reshape(1, d)], m, d)


def moe_add_layer_norm(x, y01, gates, g, b, alpha):
    m, d = x.shape
    return _ln_call(functools.partial(_ln_moe_kernel, alpha=alpha), [(x, 0), (y01, 0), (y01, m), (gates, 0)],
                    [g.reshape(1, d), b.reshape(1, d)], m, d)


def _alibi_slope(h, n_heads):
    return 2.0 ** (-8.0 * (h + 1) / n_heads)


def _swa_prompt_kernel(sink_ref, q_ref, kc_ref, vc_ref, kp_ref, vp_ref, km_ref, vm_ref, o_ref,
                       *, padf, n_heads, kvh):
    j = pl.program_id(1)
    base = j * BLOCK - padf
    kt = 3 * BLOCK
    r = lax.broadcasted_iota(jnp.int32, (BLOCK, kt), 0)
    c = lax.broadcasted_iota(jnp.int32, (BLOCK, kt), 1)
    is_band = c < 2 * BLOCK
    k_pos = jnp.where(is_band, base - BLOCK + c, c - 2 * BLOCK)
    dist = base + r - k_pos
    visible = (dist >= 0) & ((is_band & (k_pos >= N_META) & (dist <= WINDOW))
                             | (jnp.logical_not(is_band) & (k_pos < N_META)))
    dist_f = jnp.abs(dist).astype(F32)
    group = n_heads // kvh
    scale = DH_A ** -0.5
    fill = jnp.zeros((BLOCK - N_META, DH_A), F32)
    zeros = jnp.zeros((kt, DH_A), BF16)

    def pair_rows(ref_p, ref_c, ref_m, sl):
        one = jnp.concatenate([ref_p[:, sl], ref_c[:, sl], ref_m[:, sl], fill], 0).astype(BF16)
        return jnp.concatenate([jnp.concatenate([one, zeros], 1), jnp.concatenate([zeros, one], 1)], 0)

    for h in range(kvh):
        sl = slice(h * DH_A, (h + 1) * DH_A)
        k_pair = pair_rows(kp_ref, kc_ref, km_ref, sl)
        v_pair = pair_rows(vp_ref, vc_ref, vm_ref, sl)
        for g in range(0, group, 2):
            h_a = h * group + g
            hs = slice(h_a * DH_A, (h_a + 2) * DH_A)
            s = _dot_nt(q_ref[:, hs].astype(BF16), k_pair) * scale
            probs = []
            for i, hh in enumerate((h_a, h_a + 1)):
                sink = sink_ref[hh]
                logits = jnp.where(visible, s[:, i * kt:(i + 1) * kt] - _alibi_slope(hh, n_heads) * dist_f, NEG)
                mx = jnp.maximum(jnp.max(logits, -1, keepdims=True), sink)
                e = jnp.exp(logits - mx)
                den = jnp.sum(e, -1, keepdims=True) + jnp.exp(sink - mx)
                probs.append((e * (1.0 / den)).astype(BF16))
            o_ref[:, hs] = _dot(jnp.concatenate(probs, 1), v_pair).astype(o_ref.dtype)


def swa_prompt(z, sinks, *, n_batch, tp, padf, w_a, kv_w):
    nb = tp // BLOCK
    kcol = w_a // kv_w
    meta_blk = padf // N_META
    rows_meta = tp // N_META

    def cur(col):
        return pl.BlockSpec((BLOCK, kv_w), lambda b, j: (b * nb + j, col))

    def prev(col):
        return pl.BlockSpec((BLOCK, kv_w), lambda b, j: (jnp.maximum(b * nb + j - 1, 0), col))

    def meta(col):
        return pl.BlockSpec((N_META, kv_w), lambda b, j: (b * rows_meta + meta_blk, col))

    n_heads = w_a // DH_A
    return pl.pallas_call(
        functools.partial(_swa_prompt_kernel, padf=padf, n_heads=n_heads, kvh=kv_w // DH_A),
        grid=(n_batch, nb),
        in_specs=[pl.BlockSpec(memory_space=pltpu.SMEM),
                  pl.BlockSpec((BLOCK, w_a), lambda b, j: (b * nb + j, 0)),
                  cur(kcol), cur(kcol + 1), prev(kcol), prev(kcol + 1), meta(kcol), meta(kcol + 1)],
        out_specs=pl.BlockSpec((BLOCK, w_a), lambda b, j: (b * nb + j, 0)),
        out_shape=jax.ShapeDtypeStruct((z.shape[0], w_a), BF16),
        compiler_params=_params(2), name="swa_prompt",
    )(sinks, z, z, z, z, z, z, z)


def _swa_sample_kernel(sink_ref, slope_ref, q_ref, kn_ref, vn_ref, cache_ref, _into_ref, o_ref, kv_ref,
                       *, bb, t_new, past_len, n_heads, kvh):
    group = n_heads // kvh
    win = WINDOW
    n_keys = 2 * BLOCK
    n_fill = n_keys - (N_META + win + t_new)
    rows = group * t_new
    r = lax.broadcasted_iota(jnp.int32, (rows, n_keys), 0)
    c = lax.broadcasted_iota(jnp.int32, (rows, n_keys), 1)
    q_pos = past_len + r % t_new
    k_pos = jnp.where(c < N_META, c,
                      jnp.where(c < N_META + win, past_len - win + (c - N_META),
                                past_len + (c - N_META - win)))
    dist = q_pos - k_pos
    visible = (dist >= 0) & ((c < N_META) | ((k_pos >= N_META) & (dist <= WINDOW)))
    fill = jnp.zeros((n_fill, DH_A), F32)
    dist_f = jnp.abs(dist).astype(F32)
    scale = DH_A ** -0.5
    for b in range(bb):
        rs = slice(b * t_new, (b + 1) * t_new)
        for kv in range(2):
            new_ref = kn_ref if kv == 0 else vn_ref
            kv_ref[b, kv, 0:N_META, :] = cache_ref[b, kv, 0:N_META, :]
            kv_ref[b, kv, N_META:N_META + win - t_new, :] = cache_ref[b, kv, N_META + t_new:N_META + win, :]
            kv_ref[b, kv, N_META + win - t_new:N_META + win, :] = new_ref[rs, :]
        for h in range(kvh):
            sl = slice(h * DH_A, (h + 1) * DH_A)
            k_all = jnp.concatenate([cache_ref[b, 0, :, sl], kn_ref[rs, sl], fill], 0).astype(BF16)
            v_all = jnp.concatenate([cache_ref[b, 1, :, sl], vn_ref[rs, sl], fill], 0).astype(BF16)
            q = jnp.concatenate(
                [q_ref[rs, (h * group + g) * DH_A:(h * group + g + 1) * DH_A] for g in range(group)],
                0).astype(BF16)
            sink = sink_ref[h]
            logits = jnp.where(visible, _dot_nt(q, k_all) * scale - slope_ref[h] * dist_f, NEG)
            mx = jnp.maximum(jnp.max(logits, -1, keepdims=True), sink)
            e = jnp.exp(logits - mx)
            den = jnp.sum(e, -1, keepdims=True) + jnp.exp(sink - mx)
            o = _dot((e * (1.0 / den)).astype(BF16), v_all)
            for g in range(group):
                hh = h * group + g
                o_ref[rs, hh * DH_A:(hh + 1) * DH_A] = o[g * t_new:(g + 1) * t_new].astype(o_ref.dtype)


def swa_sample(z, cache_all, layer, sinks, into, *, row0, n_dec, t_new, past_len, w_a, kv_w):
    bb = _pick(n_dec, (8, 4, 2, 1))
    n_heads = w_a // DH_A
    kvh = kv_w // DH_A
    group = n_heads // kvh
    rb0 = row0 // (bb * t_new)
    assert row0 % (bb * t_new) == 0 and t_new == SUBLANES and cache_all.shape[3] == N_META + WINDOW
    kcol = w_a // kv_w
    sink_rows = jnp.repeat(sinks.astype(F32).reshape(kvh, group), t_new, axis=1)[..., None]
    slopes = jnp.asarray([_alibi_slope(h, n_heads) for h in range(n_heads)], F32)
    slope_rows = jnp.repeat(slopes.reshape(kvh, group), t_new, axis=1)[..., None]
    cache5 = cache_all.reshape(cache_all.shape[0], n_dec, 2, N_META + WINDOW, kv_w)
    small = pl.BlockSpec((kvh, group * t_new, 1), lambda i: (0, 0, 0))
    cache_spec = pl.BlockSpec((bb, 2, N_META + WINDOW, kv_w), lambda i: (i, 0, 0, 0))
    cache_in_spec = pl.BlockSpec((None, bb, 2, N_META + WINDOW, kv_w), lambda i: (layer, i, 0, 0, 0))
    return pl.pallas_call(
        functools.partial(_swa_sample_kernel, bb=bb, t_new=t_new, past_len=past_len,
                          n_heads=n_heads, kvh=kvh),
        grid=(n_dec // bb,),
        in_specs=[small, small,
                  pl.BlockSpec((bb * t_new, w_a), lambda i: (rb0 + i, 0)),
                  pl.BlockSpec((bb * t_new, kv_w), lambda i: (rb0 + i, kcol)),
                  pl.BlockSpec((bb * t_new, kv_w), lambda i: (rb0 + i, kcol + 1)),
                  cache_in_spec, pl.BlockSpec(memory_space=pl.ANY)],
        out_specs=[pl.BlockSpec((bb * t_new, w_a), lambda i: (rb0 + i, 0)), cache_spec],
        out_shape=[jax.ShapeDtypeStruct(into.shape, into.dtype),
                   jax.ShapeDtypeStruct(cache5.shape[1:], F32)],
        input_output_aliases={6: 0},
        compiler_params=_params(1), name="swa_sample",
    )(sink_rows, slope_rows, z, z, z, cache5, into)


def ssm_terms(lam_re, lam_im, log_dt, b_re, b_im, c_re, c_im, length):
    g_n, n_n = lam_re.shape
    lam = lax.complex(lam_re.astype(F32), lam_im.astype(F32))
    lam_dt = lam * jnp.exp(log_dt.astype(F32))[:, None]
    lam_bar = jnp.exp(lam_dt)
    b_bar = ((lam_bar - 1.0) / lam)[..., None] * lax.complex(b_re.astype(F32), b_im.astype(F32))
    cc = lax.complex(c_re.astype(F32), c_im.astype(F32))
    steps = jnp.arange(length + 1, dtype=F32)
    pw = jnp.exp(lam_dt[None] * steps[:, None, None])
    cb = cc[None] * pw[:length, :, None, :]
    bt = b_bar.transpose(0, 2, 1)[None, :, None]
    k_lag = jnp.sum(cb.real[:, :, :, None, :] * bt.real - cb.imag[:, :, :, None, :] * bt.imag, axis=-1)
    t = jnp.arange(length)
    sin_c = (pw[length - 1 - t][:, :, :, None] * b_bar[None]).transpose(1, 0, 3, 2)
    g_c = (cc[None] * pw[1:length + 1][:, :, None, :]).transpose(1, 3, 0, 2)
    return k_lag, sin_c, g_c, pw


def ssm_sub_matrices(terms, d_skip, steps):
    k_lag, sin_c, g_c, pw = terms
    g_n, n_n = pw.shape[1:]
    length = k_lag.shape[0]
    w = steps * P_B
    t = jnp.arange(steps)
    lag = t[None, :] - t[:, None]
    k_st = jnp.where((lag >= 0)[:, :, None, None, None], k_lag[jnp.clip(lag, 0)], 0.0)
    mt = k_st.transpose(2, 0, 4, 1, 3)
    eye_t = jnp.eye(steps, dtype=F32)
    eye_p = jnp.eye(P_B, dtype=F32)
    mt = mt + (eye_t[None, :, None, :, None] * eye_p[None, None, :, None, :]
               * d_skip.astype(F32).reshape(g_n, P_B)[:, None, None, None, :])
    mt = mt.reshape(g_n, w, w)
    sin_s = sin_c[:, length - steps:].reshape(g_n, w, n_n)
    sin_t = jnp.concatenate([sin_s.real, sin_s.imag], -1)
    g_s = g_c[:, :, :steps].reshape(g_n, n_n, w)
    sout_t = jnp.concatenate([g_s.real, -g_s.imag], 1)
    lam_l = pw[steps]
    lam_a = jnp.concatenate([lam_l.real, lam_l.real], -1)[:, None, :]
    lam_b = jnp.concatenate([-lam_l.imag, lam_l.imag], -1)[:, None, :]
    return mt, sin_t, sout_t, lam_a, lam_b


def ssm_block_diag(terms, d_skip, gl):
    k_lag, sin_c, g_c, pw = terms
    length, g_n = k_lag.shape[:2]
    n_n = pw.shape[2]
    sg = g_n // gl
    lanes = gl * P_B

    def block_diag(m, r, c):
        rows = lax.broadcasted_iota(jnp.int32, (gl * r, gl * c), 0) // r
        cols = lax.broadcasted_iota(jnp.int32, (gl * r, gl * c), 1) // c
        return jnp.where(rows == cols, jnp.tile(m.astype(BF16), (1, 1, 1, gl)), 0)

    bdk = block_diag(k_lag.reshape(length, sg, gl, P_B, P_B).transpose(1, 0, 2, 4, 3)
                     .reshape(sg, length, lanes, P_B), P_B, P_B)

    def expand_in(m):
        m = m.reshape(sg, gl, length, P_B, n_n).transpose(0, 2, 1, 3, 4).reshape(sg, length, lanes, n_n)
        return block_diag(m, P_B, n_n)

    def expand_out(m):
        m = m.reshape(sg, gl, n_n, length, P_B).transpose(0, 3, 1, 2, 4).reshape(sg, length, gl * n_n, P_B)
        return block_diag(m, n_n, P_B)

    bd_in = jnp.concatenate([expand_in(sin_c.real), expand_in(sin_c.imag)], -1)
    bd_out = jnp.concatenate([expand_out(g_c.real), expand_out(-g_c.imag)], 2)
    lam_l = pw[length].reshape(sg, 1, gl * n_n)
    return bdk, bd_in, bd_out, d_skip.astype(F32).reshape(sg, 1, lanes), lam_l.real, lam_l.imag


def _gelu_tanh(y):
    return 0.5 * y * (1.0 + jnp.tanh(math.sqrt(2.0 / math.pi) * (y + 0.044715 * (y * y * y))))


def _ssm_prompt_kernel(u_ref, gate_ref, bdk_ref, bdin_ref, bdout_ref, d_ref, lre_ref, lim_ref, o_ref, h_ref,
                       x_s, e_s, sp_s, y_s, *, n_seq, cps, c_first):
    length = u_ref.shape[1]
    half = lre_ref.shape[1]
    for s in range(length):
        x_s[s] = u_ref[:, s, :].astype(BF16)
    e = _dot(x_s[0], bdin_ref[0])
    for s in range(1, length):
        e = e + _dot(x_s[s], bdin_ref[s])
    e_s[...] = e
    sp_s[...] = jnp.zeros(sp_s.shape, F32)
    lre = lre_ref[...]
    lim = lim_ref[...]
    for b in range(n_seq):
        sr = jnp.zeros((1, half), F32)
        si = jnp.zeros((1, half), F32)
        for c in range(c_first, cps):
            row = slice(b * cps + c, b * cps + c + 1)
            sp_s[row, :] = jnp.concatenate([sr, si], 1)
            ec = e_s[row, :]
            sr, si = lre * sr - lim * si + ec[:, :half], lre * si + lim * sr + ec[:, half:]
        h_ref[b:b + 1, :] = jnp.concatenate([sr, si], 1)
    sp = sp_s[...].astype(BF16)
    d_skip = d_ref[...]
    for t in range(length):
        acc = _dot(sp, bdout_ref[t])
        for j in range(t + 1):
            acc = acc + _dot(x_s[t - j], bdk_ref[j])
        y_s[:, t, :] = acc + d_skip * u_ref[:, t, :]
    o_ref[...] = (_gelu_tanh(y_s[...]) * jax.nn.sigmoid(gate_ref[...])).astype(o_ref.dtype)


def ssm_glu_prompt(z, terms, d_skip, *, n_batch, tp, padf, u_col0, w_b):
    gl = BLOCK // P_B
    bdk, bd_in, bd_out, d_l, lam_re, lam_im = ssm_block_diag(terms, d_skip, gl)
    sg = bdk.shape[0]
    m = z.shape[0]
    cps = tp // SSM_CHUNK
    n_seq = _pick(n_batch, (2, 1))
    rows = n_seq * cps
    states = bd_in.shape[3]
    assert m % SSM_CHUNK == 0 and u_col0 % BLOCK == 0 and w_b == sg * BLOCK and padf % SSM_CHUNK == 0
    z3 = z.reshape(m // SSM_CHUNK, SSM_CHUNK, z.shape[1])
    ucb, gcb = u_col0 // BLOCK, (u_col0 + w_b) // BLOCK

    def rows_spec(cb0):
        return pl.BlockSpec((rows, SSM_CHUNK, BLOCK), lambda a, i: (i, 0, cb0 + a))

    def per_sg(shape):
        return pl.BlockSpec((None,) + shape, lambda a, i: (a,) + (0,) * len(shape))

    return pl.pallas_call(
        functools.partial(_ssm_prompt_kernel, n_seq=n_seq, cps=cps, c_first=padf // SSM_CHUNK),
        grid=(sg, n_batch // n_seq),
        in_specs=[rows_spec(ucb), rows_spec(gcb), per_sg((SSM_CHUNK, BLOCK, BLOCK)),
                  per_sg((SSM_CHUNK, BLOCK, states)), per_sg((SSM_CHUNK, states, BLOCK)),
                  per_sg((1, BLOCK)), per_sg((1, states // 2)), per_sg((1, states // 2))],
        out_specs=[rows_spec(0), pl.BlockSpec((None, None, n_seq, states), lambda a, i: (a, i, 0, 0))],
        out_shape=[jax.ShapeDtypeStruct((m // SSM_CHUNK, SSM_CHUNK, w_b), BF16),
                   jax.ShapeDtypeStruct((sg, n_batch // n_seq, n_seq, states), F32)],
        scratch_shapes=[pltpu.VMEM((SSM_CHUNK, rows, BLOCK), BF16), pltpu.VMEM((rows, states), F32),
                        pltpu.VMEM((rows, states), F32), pltpu.VMEM((rows, SSM_CHUNK, BLOCK), F32)],
        compiler_params=_params(2), name="ssm_glu_prompt",
    )(z3, z3, bdk, bd_in, bd_out, d_l, lam_re, lam_im)


def _ssm_sample_kernel(u_ref, s0_ref, mt_ref, sin_ref, sout_ref, la_ref, lb_ref, y_ref, h_ref):
    u = u_ref[...]
    s0 = s0_ref[...]
    y_ref[...] = _dot3(u, mt_ref[...]) + _dot3(s0, sout_ref[...])
    h_ref[...] = (la_ref[...] * s0 + lb_ref[...] * pltpu.roll(s0, N_B, 1)
                  + _dot3(u, sin_ref[...]))


def ssm_sample(u, s0, mats):
    mt, sin_t, sout_t, lam_a, lam_b = mats
    g_n, n_dec, lp = u.shape
    n2 = 2 * N_B

    def per_group(shape):
        return pl.BlockSpec((None,) + shape, lambda g: (g, 0, 0))

    return pl.pallas_call(
        _ssm_sample_kernel, grid=(g_n,),
        in_specs=[per_group((n_dec, lp)), per_group((n_dec, n2)), per_group((lp, lp)),
                  per_group((lp, n2)), per_group((n2, lp)), per_group((1, n2)), per_group((1, n2))],
        out_specs=[per_group((n_dec, lp)), per_group((n_dec, n2))],
        out_shape=[jax.ShapeDtypeStruct((g_n, n_dec, lp), F32),
                   jax.ShapeDtypeStruct((g_n, n_dec, n2), F32)],
        compiler_params=_params(1), name="ssm_sample",
    )(u, s0, mt, sin_t, sout_t, lam_a, lam_b)


def _glu_kernel(y_ref, gate_ref, _into_ref, o_ref):
    o_ref[...] = (_gelu_tanh(y_ref[...]) * jax.nn.sigmoid(gate_ref[...])).astype(o_ref.dtype)


def glu(y, z, into, *, row0, gate_col0):
    m, w = y.shape
    tm = _pick(math.gcd(m, row0) if row0 else m, (512, 256, 128))
    tc = 512
    assert gate_col0 % tc == 0 and w % tc == 0
    return pl.pallas_call(
        _glu_kernel, grid=(m // tm, w // tc),
        in_specs=[pl.BlockSpec((tm, tc), lambda i, c: (i, c)),
                  pl.BlockSpec((tm, tc), lambda i, c: (row0 // tm + i, gate_col0 // tc + c)),
                  pl.BlockSpec(memory_space=pl.ANY)],
        out_specs=pl.BlockSpec((tm, tc), lambda i, c: (row0 // tm + i, c)),
        out_shape=jax.ShapeDtypeStruct(into.shape, into.dtype),
        input_output_aliases={2: 0},
        compiler_params=_params(2), name="glu",
    )(y, z, into)


def _window_sums(ext, width):
    n = ext.shape[0]
    acc = ext
    k = 1
    while k < width:
        acc = acc[:n - 2 * k + 1] + acc[k:n - k + 1]
        k *= 2
    first = POOL_MAX - (width - 1)
    return acc[first:first + n - POOL_MAX]


def _pool_mix(ext, cur, count_fn, pw_ref, sc_ref, o_ref):
    cg = cur.shape[1] // len(POOL_WINDOWS)
    for g, width in enumerate(POOL_WINDOWS):
        sl = slice(g * cg, (g + 1) * cg)
        pooled = _window_sums(ext[:, sl], width) / count_fn(width) - cur[:, sl]
        mixed = _dot(pooled.astype(BF16), pw_ref[g].astype(BF16))
        o_ref[:, sl] = (mixed * sc_ref[:, sl]).astype(o_ref.dtype)


def _pool_prompt_kernel(cur_ref, prev_ref, pw_ref, sc_ref, o_ref, *, padf):
    j = pl.program_id(1)
    base = j * BLOCK - padf
    pos = base + lax.broadcasted_iota(jnp.int32, (BLOCK, 1), 0)
    pos_prev = base - POOL_MAX + lax.broadcasted_iota(jnp.int32, (POOL_MAX, 1), 0)
    cur = jnp.where(pos >= 0, cur_ref[...], 0.0)
    prev = jnp.where(pos_prev >= 0, prev_ref[...], 0.0)
    ext = jnp.concatenate([prev, cur], 0)

    def count(width):
        return jnp.maximum(jnp.minimum(pos + 1, width), 1).astype(F32)

    _pool_mix(ext, cur, count, pw_ref, sc_ref, o_ref)


def pool_prompt(z, pool_w, pool_scale, *, n_batch, tp, padf, c_width):
    nb = tp // BLOCK
    per = BLOCK // POOL_MAX
    return pl.pallas_call(
        functools.partial(_pool_prompt_kernel, padf=padf),
        grid=(n_batch, nb),
        in_specs=[pl.BlockSpec((BLOCK, c_width), lambda b, j: (b * nb + j, 0)),
                  pl.BlockSpec((POOL_MAX, c_width),
                               lambda b, j: (jnp.maximum((b * nb + j) * per - 1, 0), 0)),
                  pl.BlockSpec(pool_w.shape, lambda b, j: (0, 0, 0)),
                  pl.BlockSpec((1, c_width), lambda b, j: (0, 0))],
        out_specs=pl.BlockSpec((BLOCK, c_width), lambda b, j: (b * nb + j, 0)),
        out_shape=jax.ShapeDtypeStruct((z.shape[0], c_width), BF16),
        compiler_params=_params(2), name="pool_prompt",
    )(z, z, pool_w, pool_scale.reshape(1, c_width))


def _pool_sample_kernel(cur_ref, buf_ref, pw_ref, sc_ref, _into_ref, o_ref, *, bb, t_new):
    cur = cur_ref[...]
    cg = cur.shape[1] // len(POOL_WINDOWS)
    for g, width in enumerate(POOL_WINDOWS):
        sl = slice(g * cg, (g + 1) * cg)
        sums = [_window_sums(jnp.concatenate([buf_ref[b, :, sl], cur[b * t_new:(b + 1) * t_new, sl]], 0),
                             width) for b in range(bb)]
        pooled = jnp.concatenate(sums, 0) / float(width) - cur[:, sl]
        mixed = _dot(pooled.astype(BF16), pw_ref[g].astype(BF16))
        o_ref[:, sl] = (mixed * sc_ref[:, sl]).astype(o_ref.dtype)


def pool_sample(z, buf16, pool_w, pool_scale, into, *, row0, n_dec, t_new, c_width):
    bb = _pick(n_dec, (8, 4, 2, 1))
    rb0 = row0 // (bb * t_new)
    assert row0 % (bb * t_new) == 0
    return pl.pallas_call(
        functools.partial(_pool_sample_kernel, bb=bb, t_new=t_new),
        grid=(n_dec // bb,),
        in_specs=[pl.BlockSpec((bb * t_new, c_width), lambda i: (rb0 + i, 0)),
                  pl.BlockSpec((bb, POOL_MAX, c_width), lambda i: (i, 0, 0)),
                  pl.BlockSpec(pool_w.shape, lambda i: (0, 0, 0)),
                  pl.BlockSpec((1, c_width), lambda i: (0, 0)), pl.BlockSpec(memory_space=pl.ANY)],
        out_specs=pl.BlockSpec((bb * t_new, c_width), lambda i: (rb0 + i, 0)),
        out_shape=jax.ShapeDtypeStruct(into.shape, into.dtype),
        input_output_aliases={4: 0},
        compiler_params=_params(1), name="pool_sample",
    )(z, buf16, pool_w, pool_scale.reshape(1, c_width), into)


def _rms(x, g):
    return x * lax.rsqrt(jnp.mean(x * x, -1, keepdims=True) + EPS) * g


def _mla_prep_kernel(cq_ref, ckv_ref, kr_ref, cs_ref, qg_ref, kg_ref, cqn_ref, rows_ref, ckvb_ref, krb_ref):
    cqn_ref[...] = _rms(cq_ref[...], qg_ref[...]).astype(BF16)
    ckvn = _rms(ckv_ref[...], kg_ref[...])
    kr2 = kr_ref[...]
    cs = cs_ref[...]
    krr = kr2[:, :ROPE_DIM] * cs[:, :ROPE_DIM] + kr2[:, ROPE_DIM:] * cs[:, ROPE_DIM:]
    rows_ref[:, :KV_LORA] = ckvn
    rows_ref[:, KV_LORA:] = krr
    ckvb_ref[...] = ckvn.astype(BF16)
    krb_ref[...] = krr.astype(BF16)


def mla_prep(z, cs_rows, q_norm, kv_norm, *, c_width, q_lora):
    m = z.shape[0]
    tm = _pick(m, (256, 128))
    assert c_width % q_lora == 0 and (c_width + q_lora) % KV_LORA == 0
    kr_col = (c_width + q_lora + KV_LORA) // (2 * ROPE_DIM)
    row = lambda w, cb: pl.BlockSpec((tm, w), lambda i: (i, cb))
    return pl.pallas_call(
        _mla_prep_kernel, grid=(m // tm,),
        in_specs=[row(q_lora, c_width // q_lora), row(KV_LORA, (c_width + q_lora) // KV_LORA),
                  row(2 * ROPE_DIM, kr_col), row(2 * ROPE_DIM, 0),
                  pl.BlockSpec((1, q_lora), lambda i: (0, 0)), pl.BlockSpec((1, KV_LORA), lambda i: (0, 0))],
        out_specs=[row(q_lora, 0), row(KV_LORA + ROPE_DIM, 0), row(KV_LORA, 0), row(ROPE_DIM, 0)],
        out_shape=[jax.ShapeDtypeStruct((m, q_lora), BF16),
                   jax.ShapeDtypeStruct((m, KV_LORA + ROPE_DIM), F32),
                   jax.ShapeDtypeStruct((m, KV_LORA), BF16),
                   jax.ShapeDtypeStruct((m, ROPE_DIM), BF16)],
        compiler_params=_params(1), name="mla_prep",
    )(z, z, z, cs_rows, q_norm.reshape(1, q_lora), kv_norm.reshape(1, KV_LORA))


def _mla_prompt_kernel(qn_ref, qr_ref, qs_ref, cos_ref, sin_ref, kn_ref, v_ref, kr_ref, o_ref,
                       s_s, p_s, kcat_s, *, padf, heads, qb, exts):
    jq = pl.program_id(2)
    q_rope = (qr_ref[...] * cos_ref[...] + qs_ref[...] * sin_ref[...]).astype(BF16)
    q_nope = qn_ref[...].astype(BF16)

    @pl.when(jq == 0)
    def _():
        for h in range(heads):
            kcat_s[h, :, :NOPE] = kn_ref[:, h * NOPE:(h + 1) * NOPE]
            kcat_s[h, :, NOPE:] = kr_ref[...]
    rs_max = qb // 4
    rs_sum = qb // 2
    exp2_scale = MLA_SCALE * math.log2(math.e)

    def block(jv, ext):
        for h in range(heads):
            q_cat = jnp.concatenate([q_nope[:, h * NOPE:(h + 1) * NOPE],
                                     q_rope[:, h * ROPE_DIM:(h + 1) * ROPE_DIM]], 1)
            s_s[h, :, 0:ext] = _dot_nt(q_cat, kcat_s[h, 0:ext, :])
        for h in range(heads):
            m_rows = []
            for r0 in range(0, qb, rs_max):
                rows = slice(r0, r0 + rs_max)
                r = r0 + lax.broadcasted_iota(jnp.int32, (rs_max, BLOCK), 0)
                c = lax.broadcasted_iota(jnp.int32, (rs_max, BLOCK), 1)
                m_run = None
                for c0 in range(0, ext, BLOCK):
                    cols = slice(c0, c0 + BLOCK)
                    sc = s_s[h, rows, cols]
                    if c0 < padf or c0 + BLOCK - 1 > jv * qb + r0:
                        sc = jnp.where((c0 + c >= padf) & (c0 + c <= jv * qb + r), sc, NEG)
                        s_s[h, rows, cols] = sc
                    m_run = sc if m_run is None else jnp.maximum(m_run, sc)
                m_rows.append(jnp.max(m_run, -1, keepdims=True))
            m = jnp.concatenate(m_rows, 0)
            l_rows = []
            for r0 in range(0, qb, rs_sum):
                rows = slice(r0, r0 + rs_sum)
                l_run = None
                for c0 in range(0, ext, BLOCK):
                    cols = slice(c0, c0 + BLOCK)
                    p = jnp.exp2((s_s[h, rows, cols] - m[rows]) * exp2_scale)
                    l_run = p if l_run is None else l_run + p
                    p_s[h, rows, cols] = p.astype(BF16)
                l_rows.append(jnp.sum(l_run, -1, keepdims=True))
            o = _dot(p_s[h, :, 0:ext], v_ref[0:ext, h * V_DIM:(h + 1) * V_DIM])
            o_ref[:, h * V_DIM:(h + 1) * V_DIM] = (o / jnp.concatenate(l_rows, 0)).astype(o_ref.dtype)

    for jv, ext in enumerate(exts):
        pl.when(jq == jv)(functools.partial(block, jv, ext))


def mla_prompt(qfull, cos_t, sin_t, kv, krb, *, n_batch, tp, padf, n_heads):
    heads = 2
    n_qb = 4
    qb = tp // n_qb
    assert tp % n_qb == 0 and qb % 32 == 0
    exts = tuple(min(tp, -(-((jv + 1) * qb) // BLOCK) * BLOCK) for jv in range(n_qb))
    hp_n = n_heads // heads
    nope_w = n_heads * NOPE
    rope_blk0 = nope_w // (heads * ROPE_DIM)
    return pl.pallas_call(
        functools.partial(_mla_prompt_kernel, padf=padf, heads=heads, qb=qb, exts=exts),
        grid=(n_batch, hp_n, n_qb),
        in_specs=[pl.BlockSpec((qb, heads * NOPE), lambda b, hp, j: (b * n_qb + j, hp)),
                  pl.BlockSpec((qb, heads * ROPE_DIM), lambda b, hp, j: (b * n_qb + j, rope_blk0 + hp)),
                  pl.BlockSpec((qb, heads * ROPE_DIM),
                               lambda b, hp, j: (b * n_qb + j, rope_blk0 + hp_n + hp)),
                  pl.BlockSpec((qb, heads * ROPE_DIM), lambda b, hp, j: (j, 0)),
                  pl.BlockSpec((qb, heads * ROPE_DIM), lambda b, hp, j: (j, 0)),
                  pl.BlockSpec((tp, heads * NOPE), lambda b, hp, j: (b, hp)),
                  pl.BlockSpec((tp, heads * V_DIM), lambda b, hp, j: (b, hp_n + hp)),
                  pl.BlockSpec((tp, ROPE_DIM), lambda b, hp, j: (b, 0))],
        out_specs=pl.BlockSpec((qb, heads * V_DIM), lambda b, hp, j: (b * n_qb + j, hp)),
        out_shape=jax.ShapeDtypeStruct((qfull.shape[0], n_heads * V_DIM), BF16),
        scratch_shapes=[pltpu.VMEM((heads, qb, tp), F32), pltpu.VMEM((heads, qb, tp), BF16),
                        pltpu.VMEM((heads, tp, NOPE + ROPE_DIM), BF16)],
        compiler_params=_params(3), name="mla_prompt",
    )(qfull, qfull, qfull, cos_t, sin_t, kv, kv, krb)


def _head_proj_kernel(a_ref, w_ref, *rest):
    o_ref = rest[-1]
    o_ref[...] = _dot(a_ref[...].astype(BF16), w_ref[...].astype(BF16)).astype(o_ref.dtype)


def head_proj(a, w, *, row0, m_rows, k_head, n_head, a_col0=0, into=None, into_row0=0):
    n_heads = w.shape[0]
    tm = _pick(math.gcd(math.gcd(row0, into_row0), m_rows), (512, 256, 128))
    assert a_col0 % k_head == 0
    aliased = into is not None
    out_shape = (jax.ShapeDtypeStruct(into.shape, into.dtype) if aliased
                 else jax.ShapeDtypeStruct((m_rows, n_heads * n_head), BF16))
    return pl.pallas_call(
        _head_proj_kernel, grid=(n_heads, m_rows // tm),
        in_specs=[pl.BlockSpec((tm, k_head), lambda h, i: (row0 // tm + i, a_col0 // k_head + h)),
                  pl.BlockSpec((None, k_head, n_head), lambda h, i: (h, 0, 0))]
                 + ([pl.BlockSpec(memory_space=pl.ANY)] if aliased else []),
        out_specs=pl.BlockSpec((tm, n_head), lambda h, i: (into_row0 // tm + i, h)),
        out_shape=out_shape,
        input_output_aliases={2: 0} if aliased else {},
        compiler_params=_params(2), name="head_proj",
    )(a, w, *([into] if aliased else []))


def _mla_sample_kernel(pt_ref, ql_ref, qr_ref, qs_ref, cos_ref, sin_ref, new_ref, *rest,
                       n_pages, n_chains, n_heads, n_steps):
    page_refs = rest[:n_pages]
    o_ref = rest[n_pages]
    q_s, m_s, l_s, acc_s = rest[n_pages + 1:]
    step = pl.program_id(1)
    rows = q_s.shape[0]
    exp2_scale = MLA_SCALE * math.log2(math.e)

    @pl.when(step == 0)
    def _():
        q_s[:, :KV_LORA] = ql_ref[...]
        q_s[:, KV_LORA:] = (qr_ref[...] * cos_ref[...] + qs_ref[...] * sin_ref[...]).astype(BF16)
        new = new_ref[...]
        new = jnp.concatenate([new, jnp.zeros_like(new)], 0).astype(BF16)
        t_pad = new.shape[0]
        s = _dot_nt(q_s[...], new)
        t_q = lax.broadcasted_iota(jnp.int32, (rows, t_pad), 0) // n_heads
        t_k = lax.broadcasted_iota(jnp.int32, (rows, t_pad), 1)
        s = jnp.where(t_k <= t_q, s, NEG)
        mx = jnp.max(s, -1, keepdims=True)
        p = jnp.exp2((s - mx) * exp2_scale)
        m_s[0] = mx
        l_s[0] = jnp.sum(p, -1, keepdims=True)
        acc_s[0] = _dot(p.astype(BF16), new[:, :KV_LORA])
        for ch in range(1, n_chains):
            m_s[ch] = jnp.full((rows, 1), NEG, F32)
            l_s[ch] = jnp.zeros((rows, 1), F32)
            acc_s[ch] = jnp.zeros((rows, KV_LORA), F32)

    q = q_s[...]
    per = n_pages // n_chains
    for ch in range(n_chains):
        keys_t = jnp.concatenate([ref[...].astype(BF16) for ref in page_refs[ch * per:(ch + 1) * per]], 1)
        s = _dot(q, keys_t)
        m_old = m_s[ch]
        m_new = jnp.maximum(m_old, jnp.max(s, -1, keepdims=True))
        a = jnp.exp2((m_old - m_new) * exp2_scale)
        p = jnp.exp2((s - m_new) * exp2_scale)
        m_s[ch] = m_new
        l_s[ch] = a * l_s[ch] + jnp.sum(p, -1, keepdims=True)
        acc_s[ch] = a * acc_s[ch] + _dot_nt(p.astype(BF16), keys_t[:KV_LORA])

    @pl.when(step == n_steps - 1)
    def _():
        m = m_s[0]
        for ch in range(1, n_chains):
            m = jnp.maximum(m, m_s[ch])
        l = jnp.zeros((rows, 1), F32)
        acc = jnp.zeros((rows, KV_LORA), F32)
        for ch in range(n_chains):
            w = jnp.exp2((m_s[ch] - m) * exp2_scale)
            l = l + w * l_s[ch]
            acc = acc + w * acc_s[ch]
        o_ref[...] = (acc / l).astype(o_ref.dtype)


def mla_sample(page_table, q_lat, q_r, q_s, cos_t, sin_t, rows, cache_all, layer,
               *, row0, n_dec, t_new, n_heads):
    n_pg = page_table.shape[1]
    pps = _pick(n_pg, (PAGES_PER_STEP, 16, 8, 4, 2, 1))
    n_chains = _pick(pps, (MLA_SAMPLE_CHAINS, 1))
    n_steps = n_pg // pps
    qrows = t_new * n_heads
    assert row0 % t_new == 0
    row_spec = lambda w: pl.BlockSpec((qrows, w), lambda b, s, pt: (b, 0))
    tab_spec = pl.BlockSpec((qrows, ROPE_DIM), lambda b, s, pt: (0, 0))

    def page_spec(k):
        return pl.BlockSpec((None, None, KV_LORA + ROPE_DIM, PAGE_SIZE),
                            lambda b, s, pt, k=k: (layer, pt[b * n_pg + s * pps + k], 0, 0))

    grid_spec = pltpu.PrefetchScalarGridSpec(
        num_scalar_prefetch=1, grid=(n_dec, n_steps),
        in_specs=[row_spec(KV_LORA), row_spec(ROPE_DIM), row_spec(ROPE_DIM), tab_spec, tab_spec,
                  pl.BlockSpec((t_new, KV_LORA + ROPE_DIM), lambda b, s, pt: (row0 // t_new + b, 0))]
                 + [page_spec(k) for k in range(pps)],
        out_specs=row_spec(KV_LORA),
        scratch_shapes=[pltpu.VMEM((qrows, KV_LORA + ROPE_DIM), BF16), pltpu.VMEM((n_chains, qrows, 1), F32),
                        pltpu.VMEM((n_chains, qrows, 1), F32), pltpu.VMEM((n_chains, qrows, KV_LORA), F32)])
    return pl.pallas_call(
        functools.partial(_mla_sample_kernel, n_pages=pps, n_chains=n_chains, n_heads=n_heads,
                          n_steps=n_steps),
        grid_spec=grid_spec,
        out_shape=jax.ShapeDtypeStruct((n_dec * qrows, KV_LORA), BF16),
        compiler_params=_params(2), name="mla_sample",
    )(page_table.reshape(-1), q_lat, q_r, q_s, cos_t, sin_t, rows, *([cache_all] * pps))


def _router_kernel(x_ref, w_ref, b_ref, o_ref):
    o_ref[...] = _dot3(x_ref[...], w_ref[...]) + b_ref[...]


def router_logits(x, w, b):
    m, d = x.shape
    tm = _pick(m, (256, 128))
    n = w.shape[1]
    return pl.pallas_call(
        _router_kernel, grid=(m // tm,),
        in_specs=[pl.BlockSpec((tm, d), lambda i: (i, 0)), pl.BlockSpec((d, n), lambda i: (0, 0)),
                  pl.BlockSpec((1, n), lambda i: (0, 0))],
        out_specs=pl.BlockSpec((tm, n), lambda i: (i, 0)),
        out_shape=jax.ShapeDtypeStruct((m, n), F32),
        compiler_params=_params(1), name="router",
    )(x, w, b)


def _gmm_kernel(be_ref, nu_ref, x_ref, *rest, n_w, gated):
    w_refs = rest[:n_w]
    o_ref = rest[n_w]
    wb_refs = rest[n_w + 1:]
    j = pl.program_id(1)
    active = j < nu_ref[0]
    changed = (j == 0) | (be_ref[j] != be_ref[jnp.maximum(j - 1, 0)])

    @pl.when(active & changed)
    def _():
        for w_ref, wb_ref in zip(w_refs, wb_refs):
            wb_ref[...] = w_ref[...].astype(BF16)

    @pl.when(active)
    def _():
        x = x_ref[...]
        if gated:
            gate = _dot(x, wb_refs[0][...])
            up = _dot(x, wb_refs[1][...])
            o_ref[...] = (gate * jax.nn.sigmoid(gate) * up).astype(o_ref.dtype)
        else:
            o_ref[...] = _dot(x, wb_refs[0][...]).astype(o_ref.dtype)

    @pl.when(jnp.logical_not(active))
    def _():
        o_ref[...] = jnp.zeros(o_ref.shape, o_ref.dtype)


def grouped_matmul(x, weights, layer, block_expert, n_used, *, tn, out_dtype, gated):
    n_slots, k = x.shape
    n = weights[0].shape[3]
    n_w = len(weights)
    grid_spec = pltpu.PrefetchScalarGridSpec(
        num_scalar_prefetch=2, grid=(n // tn, n_slots // MOE_BLOCK),
        in_specs=[pl.BlockSpec((MOE_BLOCK, k), lambda c, j, be, nu: (j, 0))]
                 + [pl.BlockSpec((None, None, k, tn), lambda c, j, be, nu: (layer, be[j], 0, c))
                    for _ in range(n_w)],
        out_specs=pl.BlockSpec((MOE_BLOCK, tn), lambda c, j, be, nu: (j, c)),
        scratch_shapes=[pltpu.VMEM((k, tn), BF16) for _ in range(n_w)])
    return pl.pallas_call(
        functools.partial(_gmm_kernel, n_w=n_w, gated=gated),
        grid_spec=grid_spec,
        out_shape=jax.ShapeDtypeStruct((n_slots, n), out_dtype),
        compiler_params=_params(2), name="grouped_matmul",
    )(block_expert, n_used, x, *weights)


def moe_layer(x, xb, valid, rg_w, rg_b, re_w, re_b, w_gate, w_up, w_down, layer, ln_g, ln_b, alpha):
    m, d = x.shape
    n_route = N_GROUPS_E + N_EXPERTS
    w_r = jnp.zeros((d, BLOCK), F32).at[:, :N_GROUPS_E].set(rg_w.astype(F32))
    w_r = w_r.at[:, N_GROUPS_E:n_route].set(re_w.astype(F32))
    b_r = jnp.zeros((1, BLOCK), F32).at[0, :N_GROUPS_E].set(rg_b.astype(F32))
    b_r = b_r.at[0, N_GROUPS_E:n_route].set(re_b.astype(F32))
    logits = router_logits(x, w_r, b_r)
    rows = jnp.arange(m)
    g_logits = logits[:, :N_GROUPS_E]
    g_idx = jnp.argmax(g_logits, -1).astype(jnp.int32)
    g_gate = jax.nn.softmax(g_logits, -1)[rows, g_idx][:, None]
    e_logits = logits[:, N_GROUPS_E:n_route].reshape(m, N_GROUPS_E, EPG)[rows, g_idx]
    top_logit, top_j = lax.top_k(e_logits, TOP_K)
    gates = g_gate * jax.nn.softmax(top_logit, -1)
    experts = g_idx[:, None] * EPG + top_j.astype(jnp.int32)
    experts = jnp.where(valid[:, None], experts, N_EXPERTS)

    n_assign = m * TOP_K
    e_flat = experts.reshape(n_assign)
    order = jnp.argsort(e_flat)
    e_sorted = e_flat[order]
    counts = jnp.bincount(e_flat, length=N_EXPERTS + 1)[:N_EXPERTS]
    padded = (counts + MOE_BLOCK - 1) // MOE_BLOCK * MOE_BLOCK
    pad_end = jnp.cumsum(padded)
    pad_start = pad_end - padded
    start = jnp.cumsum(counts) - counts
    n_blocks = (n_assign + N_EXPERTS * (MOE_BLOCK - 1) + MOE_BLOCK - 1) // MOE_BLOCK
    n_slots = n_blocks * MOE_BLOCK
    e_clip = jnp.minimum(e_sorted, N_EXPERTS - 1)
    dest = jnp.where(e_sorted < N_EXPERTS,
                     pad_start[e_clip] + jnp.arange(n_assign, dtype=jnp.int32) - start[e_clip],
                     n_slots - 1).astype(jnp.int32)
    slot_of = dest[jnp.argsort(order)]
    block_start = jnp.arange(n_blocks, dtype=jnp.int32) * MOE_BLOCK
    block_expert = jnp.minimum(jnp.sum(pad_end[None, :] <= block_start[:, None], axis=1),
                               N_EXPERTS - 1).astype(jnp.int32)
    n_used = (pad_end[-1] // MOE_BLOCK).astype(jnp.int32).reshape(1)
    slot_e = jnp.repeat(block_expert, MOE_BLOCK)
    rank = jnp.arange(n_slots, dtype=jnp.int32) - pad_start[slot_e]
    src = jnp.clip(start[slot_e] + rank, 0, n_assign - 1)
    slot_tok = jnp.where((rank >= 0) & (rank < counts[slot_e]), order[src] // TOP_K, 0).astype(jnp.int32)

    xs = xb[slot_tok]
    hidden = grouped_matmul(xs, [w_gate, w_up], layer, block_expert, n_used, tn=512, out_dtype=BF16,
                            gated=True)
    y_slots = grouped_matmul(hidden, [w_down], layer, block_expert, n_used, tn=w_down.shape[3],
                             out_dtype=BF16, gated=False)
    y01 = y_slots[slot_of.reshape(m, TOP_K).T.reshape(-1)]
    return moe_add_layer_norm(x, y01, gates, ln_g, ln_b, alpha)


def _rope_tables(pos):
    half = ROPE_DIM // 2
    inv = ROPE_THETA ** (-jnp.arange(half, dtype=F32) / half)
    ang = pos.astype(F32)[:, None] * inv
    cos, sin = jnp.cos(ang), jnp.sin(ang)
    return jnp.concatenate([cos, cos], -1), jnp.concatenate([sin, sin], -1)


def _rotate_half_cols(w):
    half = ROPE_DIM // 2
    return jnp.concatenate([-w[..., half:], w[..., :half]], -1)


def even_layer(x, xb, dims, layer, kv_cache_all, h0, w_in_all, sinks, ssm_p, w_out_all):
    n_batch, tp, padf, t_real, n_dec, t_new, past_len = dims
    m, d = x.shape
    mp = n_batch * tp
    ms = n_dec * t_new
    w_b = d // 2
    w_a = w_b
    kv_w = (w_in_all.shape[2] - w_a - 2 * w_b) // 2
    g_b = w_b // P_B
    z = matmul([(xb, 0)], w_in_all, kp=d, m_rows=m, tn=512, out_dtype=F32, layer=layer)

    attn = swa_prompt(z, sinks.astype(F32), n_batch=n_batch, tp=tp, padf=padf, w_a=w_a, kv_w=kv_w)
    attn, kv_s = swa_sample(z, kv_cache_all, layer, sinks, attn, row0=mp, n_dec=n_dec, t_new=t_new,
                            past_len=past_len, w_a=w_a, kv_w=kv_w)
    kv_p = jnp.stack(
        [jnp.stack([jnp.concatenate([z[b * tp + padf:b * tp + padf + N_META, w_a + o * kv_w:w_a + (o + 1) * kv_w],
                                     z[(b + 1) * tp - WINDOW:(b + 1) * tp, w_a + o * kv_w:w_a + (o + 1) * kv_w]], 0)
                    for o in range(2)], 0) for b in range(n_batch)],
        0).reshape(n_batch, 2, N_META + WINDOW, kv_w // DH_A, DH_A)
    kv_s = kv_s.reshape(n_dec, 2, N_META + WINDOW, kv_w // DH_A, DH_A)

    u0 = w_a + 2 * kv_w
    *ssm_abc, d_skip = ssm_p
    terms = ssm_terms(*ssm_abc, SSM_CHUNK)
    y_glu, h_p = ssm_glu_prompt(z, terms, d_skip, n_batch=n_batch, tp=tp, padf=padf, u_col0=u0, w_b=w_b)
    y_glu = y_glu.reshape(m, w_b)
    gl = BLOCK // P_B
    h_p = h_p.reshape(g_b // gl, n_batch, 2, gl, N_B).transpose(1, 0, 3, 4, 2).reshape(n_batch, g_b, N_B, 2)

    u_s = z[mp:, u0:u0 + w_b].reshape(n_dec, t_new, g_b, P_B).transpose(2, 0, 1, 3).reshape(g_b, n_dec, t_new * P_B)
    s0 = h0.astype(F32).transpose(1, 0, 3, 2).reshape(g_b, n_dec, 2 * N_B)
    y_s, h_s = ssm_sample(u_s, s0, ssm_sub_matrices(terms, d_skip, t_new))
    y_s = y_s.reshape(g_b, n_dec, t_new, P_B).transpose(1, 2, 0, 3).reshape(ms, w_b)
    h_s = h_s.reshape(g_b, n_dec, 2, N_B).transpose(1, 0, 3, 2)

    y_glu = glu(y_s, z, y_glu, row0=mp, gate_col0=u0 + w_b)
    mix = matmul([(attn, 0), (y_glu, 0)], w_out_all, kp=w_a, m_rows=m, tn=512, out_dtype=BF16, layer=layer)
    return mix, kv_p, kv_s, h_p, h_s


def odd_layer(x, xb, dims, tables, layer, pool_buf, cache_all, page_table, w_in, pool_w, pool_scale,
              q_norm, w_uq, kv_norm, w_uk, w_uv, w_out_all):
    n_batch, tp, padf, t_real, n_dec, t_new, past_len = dims
    cs_rows, cos_p, sin_p, cos_s, sin_s = tables
    m, d = x.shape
    mp = n_batch * tp
    ms = n_dec * t_new
    c_width = pool_scale.shape[0]
    q_lora = q_norm.shape[0]
    n_heads = w_uq.shape[1]
    kr0 = c_width + q_lora + KV_LORA
    w_in_x = jnp.concatenate([w_in, _rotate_half_cols(w_in[:, kr0:])], 1)
    z = matmul([(xb, 0)], w_in_x, kp=d, m_rows=m, tn=w_in_x.shape[1] // 3, out_dtype=F32)

    y_pool = pool_prompt(z, pool_w, pool_scale, n_batch=n_batch, tp=tp, padf=padf, c_width=c_width)
    buf16 = jnp.pad(pool_buf.astype(F32), ((0, 0), (1, 0), (0, 0)))
    y_pool = pool_sample(z, buf16, pool_w, pool_scale, y_pool, row0=mp, n_dec=n_dec, t_new=t_new,
                         c_width=c_width)
    new_pool_p = jnp.stack([z[(b + 1) * tp - (POOL_MAX - 1):(b + 1) * tp, :c_width] for b in range(n_batch)], 0)
    u_s = z[mp:, :c_width].reshape(n_dec, t_new, c_width)
    new_pool_s = jnp.concatenate([pool_buf.astype(F32), u_s], 1)[:, -(POOL_MAX - 1):]

    cqn, rows, ckvb, krb = mla_prep(z, cs_rows, q_norm, kv_norm, c_width=c_width, q_lora=q_lora)
    w_q = jnp.concatenate([w_uq[..., :NOPE].reshape(q_lora, n_heads * NOPE),
                           w_uq[..., NOPE:].reshape(q_lora, n_heads * ROPE_DIM),
                           _rotate_half_cols(w_uq[..., NOPE:]).reshape(q_lora, n_heads * ROPE_DIM)], 1)
    qfull = matmul([(cqn, 0)], w_q, kp=q_lora, m_rows=m, tn=1024, out_dtype=F32)
    w_kv = jnp.concatenate([w_uk.reshape(KV_LORA, n_heads * NOPE), w_uv.reshape(KV_LORA, n_heads * V_DIM)], 1)
    kv = matmul([(ckvb, 0)], w_kv, kp=KV_LORA, m_rows=mp, tn=1024, out_dtype=BF16)
    y_mla = mla_prompt(qfull, cos_p, sin_p, kv, krb, n_batch=n_batch, tp=tp, padf=padf, n_heads=n_heads)

    q_lat = head_proj(qfull, w_uk.transpose(1, 2, 0), row0=mp, m_rows=ms, k_head=NOPE, n_head=KV_LORA)
    nope_w = n_heads * NOPE
    rope_w = n_heads * ROPE_DIM
    q_r = qfull[mp:, nope_w:nope_w + rope_w].reshape(ms * n_heads, ROPE_DIM)
    q_s = qfull[mp:, nope_w + rope_w:].reshape(ms * n_heads, ROPE_DIM)
    o_lat = mla_sample(page_table, q_lat.reshape(ms * n_heads, KV_LORA), q_r, q_s, cos_s, sin_s, rows,
                       cache_all, layer, row0=mp, n_dec=n_dec, t_new=t_new, n_heads=n_heads)
    y_mla = head_proj(o_lat.reshape(ms, n_heads * KV_LORA), w_uv.transpose(1, 0, 2), row0=0, m_rows=ms,
                      k_head=KV_LORA, n_head=V_DIM, into=y_mla, into_row0=mp)

    pieces =[(y_pool, 0)] + [(y_mla, cb) for cb in range(y_mla.shape[1] // c_width)]
    mix = matmul(pieces, w_out_all, kp=c_width, m_rows=m, tn=512, out_dtype=BF16, layer=layer)
    rows_p = rows[:mp].reshape(n_batch, tp, -1)[:, padf:]
    rows_s = rows[mp:].reshape(n_dec, t_new, -1)
    return mix, new_pool_p, new_pool_s, rows_p, rows_s


def kernel(x_prompt, x_sample, cache_swa_kv, state_ssm, state_pool, cache_mla, page_table, meta_tokens, w_in_ab, attn_sinks, ssm_lambda_re, ssm_lambda_im, ssm_log_dt, ssm_b_re, ssm_b_im, ssm_c_re, ssm_c_im, ssm_d, w_out_ab, w_in_cd, pool_w, pool_scale, mla_q_norm, mla_w_uq, mla_kv_norm, mla_w_uk, mla_w_uv, w_out_cd, ln_mix_g, ln_mix_b, ln_ffn_g, ln_ffn_b, router_group_w, router_group_b, router_expert_w, router_expert_b, expert_w_gate, expert_w_up, expert_w_down):
    n_batch, seq, d = x_prompt.shape
    n_dec, t_new, _ = x_sample.shape
    depth = ln_mix_g.shape[0]
    past_len = page_table.shape[1] * PAGE_SIZE
    alpha = (2 * depth) ** 0.25
    t_real = N_META + seq
    padf = (-t_real) % BLOCK
    tp = t_real + padf
    assert t_real % SSM_CHUNK == 0 and padf % N_META == 0
    mp = n_batch * tp
    ms = n_dec * t_new
    dims = (n_batch, tp, padf, t_real, n_dec, t_new, past_len)

    seq_rows = []
    for b in range(n_batch):
        seq_rows += [jnp.zeros((padf, d), F32), meta_tokens.astype(F32), x_prompt[b].astype(F32)]
    x = jnp.concatenate(seq_rows + [x_sample.astype(F32).reshape(ms, d)], 0)
    xb = x.astype(BF16)
    pos_p = jnp.arange(tp, dtype=jnp.int32) - padf
    valid = jnp.concatenate([jnp.tile(pos_p >= 0, n_batch), jnp.ones((ms,), bool)])

    pos_s = past_len + jnp.arange(t_new, dtype=jnp.int32)
    cos_p1, sin_p1 = _rope_tables(pos_p)
    cos_s1, sin_s1 = _rope_tables(pos_s)
    cs_rows = jnp.concatenate([jnp.tile(jnp.concatenate([cos_p1, sin_p1], 1), (n_batch, 1)),
                               jnp.tile(jnp.concatenate([cos_s1, sin_s1], 1), (n_dec, 1))], 0)
    n_heads_d = mla_w_uq.shape[2]
    tables = (cs_rows, jnp.tile(cos_p1, (1, 2)), jnp.tile(sin_p1, (1, 2)),
              jnp.repeat(cos_s1, n_heads_d, axis=0), jnp.repeat(sin_s1, n_heads_d, axis=0))

    cache_mla_t = jnp.swapaxes(cache_mla, 2, 3)

    swa_p, swa_s, ssm_p, ssm_s, pool_p, pool_s, mla_p, mla_s = [], [], [], [], [], [], [], []
    for l in range(depth):
        i = l // 2
        if l % 2 == 0:
            ssm_params = (ssm_lambda_re[i], ssm_lambda_im[i], ssm_log_dt[i], ssm_b_re[i], ssm_b_im[i],
                          ssm_c_re[i], ssm_c_im[i], ssm_d[i])
            mix, kv_p, kv_s, h_p, h_s = even_layer(x, xb, dims, i, cache_swa_kv, state_ssm[i], w_in_ab,
                                                   attn_sinks[i], ssm_params, w_out_ab)
            swa_p.append(kv_p)
            swa_s.append(kv_s)
            ssm_p.append(h_p)
            ssm_s.append(h_s)
        else:
            mix, np_p, np_s, rows_p, rows_s = odd_layer(
                x, xb, dims, tables, i, state_pool[i], cache_mla_t, page_table, w_in_cd[i], pool_w[i],
                pool_scale[i], mla_q_norm[i], mla_w_uq[i], mla_kv_norm[i], mla_w_uk[i], mla_w_uv[i],
                w_out_cd)
            pool_p.append(np_p)
            pool_s.append(np_s)
            mla_p.append(rows_p)
            mla_s.append(rows_s)
        x, xb = add_layer_norm(x, mix, ln_mix_g[l], ln_mix_b[l], alpha)
        x, xb = moe_layer(x, xb, valid, router_group_w[l], router_group_b[l], router_expert_w[l],
                          router_expert_b[l], expert_w_gate, expert_w_up, expert_w_down, l,
                          ln_ffn_g[l], ln_ffn_b[l], alpha)
    y_p = x[:mp].reshape(n_batch, tp, d)[:, padf + N_META:]
    y_s = x[mp:].reshape(n_dec, t_new, d)
    return (y_p, y_s, jnp.stack(swa_p), jnp.stack(swa_s), jnp.stack(ssm_p), jnp.stack(ssm_s),
            jnp.stack(pool_p), jnp.stack(pool_s), jnp.stack(mla_p), jnp.stack(mla_s))
```
